```python
import jax, jax.numpy as jnp
from jax import lax
import numpy as np

D_MODEL = 4096
BATCH = 2
SEQ = 8192
DEPTH = 1

HEAD_DIM = 128
N_RET_HEADS = D_MODEL // (2 * HEAD_DIM)
N_SB_HEADS = D_MODEL // (2 * HEAD_DIM)
RET_WIDTH = N_RET_HEADS * HEAD_DIM
SB_WIDTH = N_SB_HEADS * HEAD_DIM
MIX_WIDTH = RET_WIDTH + SB_WIDTH
IN_SPLITS = [RET_WIDTH] * 4 + [SB_WIDTH] * 3
IN_WIDTH = sum(IN_SPLITS)
RET_CHUNK = 128
SB_BLOCK = 128
N_MEM = 256
N_CROSS_HEADS = 4
CROSS_WIDTH = N_CROSS_HEADS * HEAD_DIM
D_FF = 11008
CONV_WIDTH = 3
ROPE_BASE = 10000.0
EPS = 1e-6

kernel_name = "hybrid_retention_stickbreaking_layer"


def rmsnorm(x, g):
    xf = x.astype(jnp.float32)
    y = xf * lax.rsqrt(jnp.mean(xf * xf, axis=-1, keepdims=True) + EPS)
    return (y * g.astype(jnp.float32)).astype(x.dtype)


def rope_tables(S):
    inv_freq = ROPE_BASE ** (-jnp.linspace(0.0, 1.0, HEAD_DIM // 2, dtype=jnp.float32))
    ang = jnp.arange(S, dtype=jnp.float32)[:, None] * inv_freq[None, :]
    return jnp.cos(ang), jnp.sin(ang)


def apply_rope(x, cos, sin):
    c = cos[None, :, None, :].astype(x.dtype)
    s = sin[None, :, None, :].astype(x.dtype)
    x1, x2 = jnp.split(x, 2, axis=-1)
    return jnp.concatenate([x1 * c - x2 * s, x2 * c + x1 * s], axis=-1)


def retention_chunkwise(q, k, v):
    B, H, S, d = q.shape
    C = RET_CHUNK
    N = S // C
    dt = q.dtype
    log_g = jnp.log1p(-jnp.exp2(-5.0 - jnp.arange(H, dtype=jnp.float32)))
    idx = jnp.arange(C, dtype=jnp.float32)
    diff = idx[:, None] - idx[None, :]
    intra_decay = jnp.where(diff >= 0, jnp.exp(jnp.maximum(diff, 0.0)[None] * log_g[:, None, None]), 0.0)
    qc = q.reshape(B, H, N, C, d)
    kc = k.reshape(B, H, N, C, d)
    vc = v.reshape(B, H, N, C, d)
    scores = jnp.einsum('bhncd,bhnmd->bhncm', qc, kc) * intra_decay[:, None].astype(dt)
    intra = jnp.einsum('bhncm,bhnmd->bhncd', scores, vc)
    k_decay = jnp.exp((C - 1 - idx)[None, :] * log_g[:, None])[:, None, :, None].astype(dt)
    chunk_kv = jnp.einsum('bhncd,bhnce->nbhde', kc * k_decay, vc)
    chunk_decay = jnp.exp(C * log_g)[None, :, None, None].astype(dt)

    def step(state, kv):
        return state * chunk_decay + kv, state

    _, prev_state = lax.scan(step, jnp.zeros((B, H, d, d), dt), chunk_kv)
    q_decay = jnp.exp((idx + 1.0)[None, :] * log_g[:, None])[:, None, :, None].astype(dt)
    cross = jnp.einsum('bhncd,nbhde->bhnce', qc * q_decay, prev_state)
    return (intra + cross).reshape(B, H, S, d)


def stick_breaking_attention(q, k, v):
    B, H, S, d = q.shape
    Q = SB_BLOCK
    N = S // Q
    scale = HEAD_DIM ** -0.5
    q_blocks = q.reshape(B, H, N, Q, d).transpose(2, 0, 1, 3, 4)
    starts = jnp.arange(N, dtype=jnp.int32) * Q
    key_pos = jnp.arange(S, dtype=jnp.int32)

    def block(args):
        qb, t0 = args
        z = jnp.einsum('bhqd,bhkd->bhqk', qb, k).astype(jnp.float32) * scale
        q_pos = t0 + jnp.arange(Q, dtype=jnp.int32)
        mask = key_pos[None, :] < q_pos[:, None]
        log_stay = jnp.where(mask, jax.nn.log_sigmoid(-z), 0.0)
        log_a = jax.nn.log_sigmoid(z) + lax.cumsum(log_stay, axis=3, reverse=True) - log_stay
        a = jnp.where(mask, jnp.exp(log_a), 0.0).astype(v.dtype)
        return jnp.einsum('bhqk,bhkd->bhqd', a, v)

    out = lax.map(block, (q_blocks, starts))
    return out.transpose(1, 2, 0, 3, 4).reshape(B, H, S, d)


def hybrid_mixer(xn, w_in, ret_norm, sb_q_norm, sb_k_norm, sb_out_norm, w_out):
    B, S, _ = xn.shape
    proj = xn @ w_in
    offsets = [int(o) for o in np.cumsum(IN_SPLITS)[:-1]]
    rq, rk, rv, rg, sq, sk, sv = jnp.split(proj, offsets, axis=-1)

    def heads(t, h):
        return t.reshape(B, S, h, HEAD_DIM)

    cos, sin = rope_tables(S)
    rq = apply_rope(heads(rq, N_RET_HEADS), cos, sin) * (HEAD_DIM ** -0.5)
    rk = apply_rope(heads(rk, N_RET_HEADS), cos, sin)
    rv = heads(rv, N_RET_HEADS)
    ret = retention_chunkwise(rq.transpose(0, 2, 1, 3), rk.transpose(0, 2, 1, 3), rv.transpose(0, 2, 1, 3))
    ret = rmsnorm(ret.transpose(0, 2, 1, 3), ret_norm).reshape(B, S, RET_WIDTH)
    ret = ret * jax.nn.silu(rg)

    sq = rmsnorm(heads(sq, N_SB_HEADS), sb_q_norm)
    sk = rmsnorm(heads(sk, N_SB_HEADS), sb_k_norm)
    sv = heads(sv, N_SB_HEADS)
    sb = stick_breaking_attention(sq.transpose(0, 2, 1, 3), sk.transpose(0, 2, 1, 3), sv.transpose(0, 2, 1, 3))
    sb = rmsnorm(sb.transpose(0, 2, 1, 3), sb_out_norm).reshape(B, S, SB_WIDTH)

    return jnp.concatenate([ret, sb], axis=-1) @ w_out


def memory_cross_attention(xn, memn, w_q, w_kv, q_norm, k_norm, w_o):
    B, S, _ = xn.shape
    M = memn.shape[1]
    q = rmsnorm((xn @ w_q).reshape(B, S, N_CROSS_HEADS, HEAD_DIM), q_norm)
    k, v = jnp.split(memn @ w_kv, 2, axis=-1)
    k = rmsnorm(k.reshape(B, M, N_CROSS_HEADS, HEAD_DIM), k_norm)
    v = v.reshape(B, M, N_CROSS_HEADS, HEAD_DIM)
    s = jnp.einsum('bshd,bmhd->bhsm', q, k).astype(jnp.float32) * (HEAD_DIM ** -0.5)
    p = jax.nn.softmax(s, axis=-1).astype(v.dtype)
    o = jnp.einsum('bhsm,bmhd->bshd', p, v).reshape(B, S, CROSS_WIDTH)
    return o @ w_o


def conv_ffn(xn, w_up, conv_w, conv_b, w_down):
    S = xn.shape[1]
    u = xn @ w_up
    up = jnp.pad(u, ((0, 0), (CONV_WIDTH - 1, 0), (0, 0)))
    c = conv_b
    for j in range(CONV_WIDTH):
        c = c + up[:, j:j + S] * conv_w[j]
    g, val = jnp.split(c, 2, axis=-1)
    return (jax.nn.silu(g) * val) @ w_down


def setup_inputs(seed: int = 0) -> dict:
    key = jax.random.key(seed)
    ks = jax.random.split(key, 21)
    f32 = jnp.float32

    def nrm(k, shape, scale):
        return jax.random.normal(k, shape, f32) * scale

    def gain(k, n):
        return 1.0 + 0.02 * jax.random.normal(k, (DEPTH, n), f32)

    return {
        "x": nrm(ks[0], (BATCH, SEQ, D_MODEL), 1.0),
        "mem": nrm(ks[1], (BATCH, N_MEM, D_MODEL), 1.0),
        "attn_norm": gain(ks[2], D_MODEL),
        "w_in": nrm(ks[3], (DEPTH, D_MODEL, IN_WIDTH), D_MODEL ** -0.5),
        "ret_norm": gain(ks[4], HEAD_DIM),
        "sb_q_norm": gain(ks[5], HEAD_DIM),
        "sb_k_norm": gain(ks[6], HEAD_DIM),
        "sb_out_norm": gain(ks[7], HEAD_DIM),
        "w_out": nrm(ks[8], (DEPTH, MIX_WIDTH, D_MODEL), MIX_WIDTH ** -0.5),
        "cross_norm": gain(ks[9], D_MODEL),
        "mem_norm": gain(ks[10], D_MODEL),
        "cross_w_q": nrm(ks[11], (DEPTH, D_MODEL, CROSS_WIDTH), D_MODEL ** -0.5),
        "cross_w_kv": nrm(ks[12], (DEPTH, D_MODEL, 2 * CROSS_WIDTH), D_MODEL ** -0.5),
        "cross_q_norm": gain(ks[13], HEAD_DIM),
        "cross_k_norm": gain(ks[14], HEAD_DIM),
        "cross_w_o": nrm(ks[15], (DEPTH, CROSS_WIDTH, D_MODEL), CROSS_WIDTH ** -0.5),
        "ffn_norm": gain(ks[16], D_MODEL),
        "ffn_w_up": nrm(ks[17], (DEPTH, D_MODEL, 2 * D_FF), D_MODEL ** -0.5),
        "ffn_conv_w": nrm(ks[18], (DEPTH, CONV_WIDTH, 2 * D_FF), CONV_WIDTH ** -0.5),
        "ffn_conv_b": nrm(ks[19], (DEPTH, 2 * D_FF), 0.01),
        "ffn_w_down": nrm(ks[20], (DEPTH, D_FF, D_MODEL), D_FF ** -0.5),
    }


def reference(x, mem, attn_norm, w_in, ret_norm, sb_q_norm, sb_k_norm, sb_out_norm, w_out,
              cross_norm, mem_norm, cross_w_q, cross_w_kv, cross_q_norm, cross_k_norm, cross_w_o,
              ffn_norm, ffn_w_up, ffn_conv_w, ffn_conv_b, ffn_w_down):
    h = x
    for l in range(DEPTH):
        h = h + hybrid_mixer(rmsnorm(h, attn_norm[l]), w_in[l], ret_norm[l], sb_q_norm[l],
                             sb_k_norm[l], sb_out_norm[l], w_out[l])
        h = h + memory_cross_attention(rmsnorm(h, cross_norm[l]), rmsnorm(mem, mem_norm[l]),
                                       cross_w_q[l], cross_w_kv[l], cross_q_norm[l],
                                       cross_k_norm[l], cross_w_o[l])
        h = h + conv_ffn(rmsnorm(h, ffn_norm[l]), ffn_w_up[l], ffn_conv_w[l], ffn_conv_b[l],
                         ffn_w_down[l])
    return h
```

```python
import functools

import jax
import jax.numpy as jnp
from jax import lax
from jax.experimental import pallas as pl
from jax.experimental.pallas import tpu as pltpu

HEAD_DIM = 128
EPS = 1e-6
ROPE_BASE = 10000.0
RET_CHUNK = 128
N_CROSS_HEADS = 4
CONV_WIDTH = 3
SUBLANES = 8

VMEM_LIMIT_BYTES = 56 * 1024 * 1024

SB_LOG_UNDERFLOW = -104.0

F32 = jnp.float32
BF16 = jnp.bfloat16


def _cparams(*sem):
    return pltpu.CompilerParams(dimension_semantics=sem, vmem_limit_bytes=VMEM_LIMIT_BYTES)


def _rms(x, g):
    return x * lax.rsqrt(jnp.mean(x * x, axis=-1, keepdims=True) + EPS) * g


def _silu(x):
    return x * (1.0 / (1.0 + jnp.exp(-x)))


def _dot(a, b):
    return jnp.dot(a, b, preferred_element_type=F32)


def _dot_nt(a, b):
    return lax.dot_general(a, b, (((1,), (1,)), ((), ())), preferred_element_type=F32)


def _rmsnorm_kernel(x_ref, g_ref, o_ref):
    o_ref[...] = _rms(x_ref[...], g_ref[...]).astype(o_ref.dtype)


def _rmsnorm_rows(x2d, g, out_dtype, tm=256):
    T, D = x2d.shape
    tm = min(tm, T)
    return pl.pallas_call(
        _rmsnorm_kernel,
        grid=(T // tm,),
        in_specs=[pl.BlockSpec((tm, D), lambda i: (i, 0)), pl.BlockSpec((1, D), lambda i: (0, 0))],
        out_specs=pl.BlockSpec((tm, D), lambda i: (i, 0)),
        out_shape=jax.ShapeDtypeStruct((T, D), out_dtype),
        compiler_params=_cparams("parallel"),
        name="rmsnorm_rows",
    )(x2d, g.reshape(1, D))


def _inproj_kernel(a_ref, w_ref, *rest, mode, heads_per_tile, n_scaled_tiles):
    acc = _dot(a_ref[...], w_ref[...])
    j = pl.program_id(1)
    o_ref = rest[-1]
    if mode == "rope":
        cos = rest[0][...]
        sin = rest[1][...]
        scale = jnp.where(j < n_scaled_tiles, HEAD_DIM**-0.5, 1.0).astype(F32)
    elif mode == "norm":
        gain = rest[0][0]
    for hh in range(heads_per_tile):
        xh = acc[:, hh * HEAD_DIM : (hh + 1) * HEAD_DIM]
        if mode == "rope":
            xh = (xh * cos + pltpu.roll(xh, HEAD_DIM // 2, axis=1) * sin) * scale
        elif mode == "norm":
            xh = _rms(xh, gain)
        o_ref[0, hh] = xh.astype(o_ref.dtype)


def _inproj(xn, w, batch, seq, *, n_col_tiles, col_map, mode, out_dtype, extra=(), extra_specs=(),
            n_scaled_tiles=0, tm=1024, tn=512, name):
    T, D = xn.shape
    tm = min(tm, seq)
    hpt = tn // HEAD_DIM
    tiles_per_seq = seq // tm
    kern = functools.partial(_inproj_kernel, mode=mode, heads_per_tile=hpt, n_scaled_tiles=n_scaled_tiles)
    return pl.pallas_call(
        kern,
        grid=(T // tm, n_col_tiles),
        in_specs=[
            pl.BlockSpec((tm, D), lambda i, j: (i, 0)),
            pl.BlockSpec((D, tn), lambda i, j: (0, col_map(j))),
            *extra_specs,
        ],
        out_specs=pl.BlockSpec(
            (1, hpt, tm, HEAD_DIM), lambda i, j: (i // tiles_per_seq, j, i % tiles_per_seq, 0)
        ),
        out_shape=jax.ShapeDtypeStruct((batch, n_col_tiles * hpt, seq, HEAD_DIM), out_dtype),
        compiler_params=_cparams("parallel", "arbitrary"),
        name=name,
    )(xn, w, *extra)


def _retention_kernel(q_ref, k_ref, v_ref, g_ref, dint_ref, kdec_ref, qdec_ref, cdec_ref, gn_ref, o_ref,
                      state_ref, *, n_chunks):
    @pl.when(pl.program_id(2) == 0)
    def _():
        state_ref[...] = jnp.zeros_like(state_ref)

    dint = dint_ref[0]
    kdec = kdec_ref[0]
    qdec = qdec_ref[0]
    cdec = cdec_ref[0]
    gn = gn_ref[...]
    state = state_ref[...]
    C = RET_CHUNK
    for n in range(n_chunks):
        rows = pl.ds(n * C, C)
        q = q_ref[0, 0, rows, :]
        k = k_ref[0, 0, rows, :]
        v = v_ref[0, 0, rows, :]
        scores = _dot_nt(q.astype(BF16), k.astype(BF16))
        intra = _dot((scores * dint).astype(BF16), v)
        cross = _dot((q * qdec).astype(BF16), state.astype(BF16))
        kd_t = (k * kdec).T.astype(BF16)
        state = state * cdec + _dot(kd_t, v)
        y = _rms(intra + cross, gn) * _silu(g_ref[0, 0, rows, :])
        o_ref[0, rows, :] = y.astype(o_ref.dtype)
    state_ref[...] = state


def _retention(qk, v_all, gate, tables, gn, *, rows_per_step=1024):
    B, H2, S, d = qk.shape
    H = H2 // 2
    tr = min(rows_per_step, S)
    dint, kdec, qdec, cdec = tables
    C = RET_CHUNK
    kern = functools.partial(_retention_kernel, n_chunks=tr // C)
    head_blk = (1, 1, tr, d)
    return pl.pallas_call(
        kern,
        grid=(B, H, S // tr),
        in_specs=[
            pl.BlockSpec(head_blk, lambda b, h, r: (b, h, r, 0)),
            pl.BlockSpec(head_blk, lambda b, h, r: (b, H + h, r, 0)),
            pl.BlockSpec(head_blk, lambda b, h, r: (b, h, r, 0)),
            pl.BlockSpec(head_blk, lambda b, h, r: (b, h, r, 0)),
            pl.BlockSpec((1, C, C), lambda b, h, r: (h, 0, 0)),
            pl.BlockSpec((1, C, d), lambda b, h, r: (h, 0, 0)),
            pl.BlockSpec((1, C, d), lambda b, h, r: (h, 0, 0)),
            pl.BlockSpec((1, 1, d), lambda b, h, r: (h, 0, 0)),
            pl.BlockSpec((1, d), lambda b, h, r: (0, 0)),
        ],
        out_specs=pl.BlockSpec((1, tr, d), lambda b, h, r: (b, r, h)),
        out_shape=jax.ShapeDtypeStruct((B, S, H * d), BF16),
        scratch_shapes=[pltpu.VMEM((d, d), F32)],
        compiler_params=_cparams("parallel", "parallel", "arbitrary"),
        name="retention",
    )(qk, qk, v_all, gate, dint, kdec, qdec, cdec, gn.reshape(1, d))


def _sb_kernel(q_ref, k_ref, v_ref, w2_ref, gn_ref, o_ref, *, tq, tk, n_qblocks):
    scale = HEAD_DIM**-0.5
    w2 = w2_ref[...]
    gn = gn_ref[...]
    n_diag = tq // tk
    row = lax.broadcasted_iota(jnp.int32, (tq, tk), 0)
    col = lax.broadcasted_iota(jnp.int32, (tq, tk), 1)

    def block_update(qb, kstart, run, acc, mask):
        kb = k_ref[0, 0, pl.ds(kstart, tk), :]
        vb = v_ref[0, 0, pl.ds(kstart, tk), :]
        z = _dot_nt(qb, kb) * scale
        nz = -z
        ls = jnp.minimum(nz, 0.0) - jnp.log(1.0 + jnp.exp(jnp.minimum(z, nz)))
        if mask is not None:
            ls = jnp.where(mask, ls, 0.0)
        hi = ls.astype(BF16)
        lo = (ls - hi.astype(F32)).astype(BF16)
        cr = _dot(hi, w2) + _dot(lo, w2)
        a = jnp.exp(z + cr[:, :tk] + run)
        if mask is not None:
            a = jnp.where(mask, a, 0.0)
        return run + cr[:, tk:], acc + _dot(a.astype(BF16), vb)

    def q_body(qi, carry):
        q0 = pl.multiple_of(qi * tq, tq)
        qb = q_ref[0, 0, pl.ds(q0, tq), :]
        run = jnp.zeros((tq, tk), F32)
        acc = jnp.zeros((tq, HEAD_DIM), F32)
        for dblk in reversed(range(n_diag)):
            mask = (col + dblk * tk) < row
            run, acc = block_update(qb, pl.multiple_of(q0 + dblk * tk, tk), run, acc, mask)

        def cond(c):
            return jnp.logical_and(c[0] >= 0, jnp.max(c[1]) > SB_LOG_UNDERFLOW)

        def body(c):
            kblk, run, acc = c
            run, acc = block_update(qb, pl.multiple_of(kblk * tk, tk), run, acc, None)
            return kblk - 1, run, acc

        _, _, acc = lax.while_loop(cond, body, (qi * n_diag - 1, run, acc))
        o_ref[0, pl.ds(q0, tq), :] = _rms(acc, gn).astype(o_ref.dtype)
        return carry

    lax.fori_loop(0, n_qblocks, q_body, 0)


def _stick_breaking(sqk, v_all, gn, *, tq=256, tk=128):
    B, H2, S, d = sqk.shape
    H = H2 // 2
    tq = min(tq, S)
    idx = jnp.arange(tk)
    w2 = jnp.concatenate(
        [(idx[:, None] >= idx[None, :]).astype(BF16), jnp.ones((tk, tk), BF16)], axis=1
    )
    kern = functools.partial(_sb_kernel, tq=tq, tk=tk, n_qblocks=S // tq)
    head_blk = (1, 1, S, d)
    return pl.pallas_call(
        kern,
        grid=(B, H),
        in_specs=[
            pl.BlockSpec(head_blk, lambda b, h: (b, h, 0, 0)),
            pl.BlockSpec(head_blk, lambda b, h: (b, H + h, 0, 0)),
            pl.BlockSpec(head_blk, lambda b, h: (b, H + h, 0, 0)),
            pl.BlockSpec((tk, 2 * tk), lambda b, h: (0, 0)),
            pl.BlockSpec((1, d), lambda b, h: (0, 0)),
        ],
        out_specs=pl.BlockSpec((1, S, d), lambda b, h: (b, 0, h)),
        out_shape=jax.ShapeDtypeStruct((B, S, H * d), BF16),
        compiler_params=_cparams("parallel", "parallel"),
        name="stick_breaking",
    )(sqk, sqk, v_all, w2, gn.reshape(1, d))


def _outproj_kernel(a1_ref, a2_ref, w1_ref, w2_ref, x_ref, o_ref):
    o_ref[...] = x_ref[...] + (_dot(a1_ref[...], w1_ref[...]) + _dot(a2_ref[...], w2_ref[...]))


def _outproj(ret2d, sb2d, w_out, x2d, *, tm=1024, tn=512):
    T, K = ret2d.shape
    N = w_out.shape[1]
    tm = min(tm, T)
    return pl.pallas_call(
        _outproj_kernel,
        grid=(T // tm, N // tn),
        in_specs=[
            pl.BlockSpec((tm, K), lambda i, j: (i, 0)),
            pl.BlockSpec((tm, K), lambda i, j: (i, 0)),
            pl.BlockSpec((K, tn), lambda i, j: (0, j)),
            pl.BlockSpec((K, tn), lambda i, j: (1, j)),
            pl.BlockSpec((tm, tn), lambda i, j: (i, j)),
        ],
        out_specs=pl.BlockSpec((tm, tn), lambda i, j: (i, j)),
        out_shape=jax.ShapeDtypeStruct((T, N), F32),
        compiler_params=_cparams("parallel", "arbitrary"),
        name="outproj",
    )(ret2d, sb2d, w_out, w_out, x2d)


def _memkv_kernel(m_ref, gm_ref, w_ref, gk_ref, k_ref, v_ref):
    mn = _rms(m_ref[0], gm_ref[...]).astype(BF16)
    kv = _dot(mn, w_ref[...])
    cw = kv.shape[1] // 2
    gk = gk_ref[...]
    for hh in range(cw // HEAD_DIM):
        sl = slice(hh * HEAD_DIM, (hh + 1) * HEAD_DIM)
        k_ref[0, :, sl] = _rms(kv[:, sl], gk).astype(k_ref.dtype)
    v_ref[0] = kv[:, cw:].astype(v_ref.dtype)


def _memkv(mem, g_mem, w_kv, g_k):
    B, M, D = mem.shape
    cw = w_kv.shape[1] // 2
    return pl.pallas_call(
        _memkv_kernel,
        grid=(B,),
        in_specs=[
            pl.BlockSpec((1, M, D), lambda b: (b, 0, 0)),
            pl.BlockSpec((1, D), lambda b: (0, 0)),
            pl.BlockSpec((D, 2 * cw), lambda b: (0, 0)),
            pl.BlockSpec((1, HEAD_DIM), lambda b: (0, 0)),
        ],
        out_specs=[pl.BlockSpec((1, M, cw), lambda b: (b, 0, 0))] * 2,
        out_shape=[jax.ShapeDtypeStruct((B, M, cw), BF16)] * 2,
        compiler_params=_cparams("parallel"),
        name="memkv",
    )(mem, g_mem.reshape(1, D), w_kv, g_k.reshape(1, HEAD_DIM))


def _cross_kernel(h_ref, gc_ref, wq_ref, gq_ref, k_ref, v_ref, wo_ref, gf_ref, h2_ref, xn_ref):
    h = h_ref[...]
    xn = _rms(h, gc_ref[...]).astype(BF16)
    q = _dot(xn, wq_ref[...])
    gq = gq_ref[...]
    scale = HEAD_DIM**-0.5
    outs = []
    for hh in range(q.shape[1] // HEAD_DIM):
        sl = slice(hh * HEAD_DIM, (hh + 1) * HEAD_DIM)
        qh = _rms(q[:, sl], gq).astype(BF16)
        s = _dot_nt(qh, k_ref[0, :, sl]) * scale
        e = jnp.exp(s - jnp.max(s, axis=-1, keepdims=True))
        p = e / jnp.sum(e, axis=-1, keepdims=True)
        outs.append(_dot(p.astype(BF16), v_ref[0, :, sl]).astype(BF16))
    h2 = h + _dot(jnp.concatenate(outs, axis=-1), wo_ref[...])
    h2_ref[...] = h2
    xn_ref[...] = _rms(h2, gf_ref[...]).astype(xn_ref.dtype)


def _cross_attention(h2d, seq, g_cross, w_q, g_q, kmem, vmem, w_o, g_ffn, *, tm=256):
    T, D = h2d.shape
    _, M, cw = kmem.shape
    tm = min(tm, seq)
    tiles_per_seq = seq // tm
    const = lambda i: (0, 0)
    return pl.pallas_call(
        _cross_kernel,
        grid=(T // tm,),
        in_specs=[
            pl.BlockSpec((tm, D), lambda i: (i, 0)),
            pl.BlockSpec((1, D), const),
            pl.BlockSpec((D, cw), const),
            pl.BlockSpec((1, HEAD_DIM), const),
            pl.BlockSpec((1, M, cw), lambda i: (i // tiles_per_seq, 0, 0)),
            pl.BlockSpec((1, M, cw), lambda i: (i // tiles_per_seq, 0, 0)),
            pl.BlockSpec((cw, D), const),
            pl.BlockSpec((1, D), const),
        ],
        out_specs=[pl.BlockSpec((tm, D), lambda i: (i, 0))] * 2,
        out_shape=[jax.ShapeDtypeStruct((T, D), F32), jax.ShapeDtypeStruct((T, D), BF16)],
        compiler_params=_cparams("parallel"),
        name="cross_attention",
    )(h2d, g_cross.reshape(1, D), w_q, g_q.reshape(1, HEAD_DIM), kmem, vmem, w_o, g_ffn.reshape(1, D))


def _ffn_up_kernel(a_ref, wg_ref, wv_ref, cwg_ref, cwv_ref, cbg_ref, cbv_ref, o_ref, ubuf_ref, carry_ref,
                   *, tm, tiles_per_seq):
    i = pl.program_id(0)
    j = pl.program_id(1)
    a = a_ref[...]
    seq_start = (i % tiles_per_seq) == 0
    halo = SUBLANES

    def conv(w_ref, cw_ref, cb_ref, slot):
        u = _dot(a, w_ref[...])

        @pl.when(seq_start)
        def _():
            ubuf_ref[slot, 0:halo, :] = jnp.zeros((halo, u.shape[1]), F32)

        @pl.when(jnp.logical_not(seq_start))
        def _():
            ubuf_ref[slot, 0:halo, :] = carry_ref[j, slot]

        ubuf_ref[slot, halo : halo + tm, :] = u
        carry_ref[j, slot] = u[tm - halo :, :]
        cw = cw_ref[...]
        c = cb_ref[...] + ubuf_ref[slot, halo - 2 : halo - 2 + tm, :] * cw[0:1, :]
        c = c + ubuf_ref[slot, halo - 1 : halo - 1 + tm, :] * cw[1:2, :]
        return c + u * cw[2:3, :]

    g = conv(wg_ref, cwg_ref, cbg_ref, 0)
    val = conv(wv_ref, cwv_ref, cbv_ref, 1)
    o_ref[...] = (_silu(g) * val).astype(o_ref.dtype)


def _ffn_up(xn, seq, wg, wv, cwg, cwv, cbg, cbv, *, tm=1024, tf=512):
    T, D = xn.shape
    F = wg.shape[1]
    tm = min(tm, seq)
    tf = min(tf, F)
    n_j = F // tf
    kern = functools.partial(_ffn_up_kernel, tm=tm, tiles_per_seq=seq // tm)
    col = lambda i, j: (0, j)
    return pl.pallas_call(
        kern,
        grid=(T // tm, n_j),
        in_specs=[
            pl.BlockSpec((tm, D), lambda i, j: (i, 0)),
            pl.BlockSpec((D, tf), col),
            pl.BlockSpec((D, tf), col),
            pl.BlockSpec((CONV_WIDTH, tf), col),
            pl.BlockSpec((CONV_WIDTH, tf), col),
            pl.BlockSpec((1, tf), col),
            pl.BlockSpec((1, tf), col),
        ],
        out_specs=pl.BlockSpec((tm, tf), lambda i, j: (i, j)),
        out_shape=jax.ShapeDtypeStruct((T, F), BF16),
        scratch_shapes=[
            pltpu.VMEM((2, tm + SUBLANES, tf), F32),
            pltpu.VMEM((n_j, 2, SUBLANES, tf), F32),
        ],
        compiler_params=_cparams("arbitrary", "arbitrary"),
        name="ffn_up",
    )(xn, wg, wv, cwg, cwv, cbg, cbv)


def _ffn_down_kernel(a_ref, w_ref, h_ref, o_ref, acc_ref):
    k = pl.program_id(2)

    @pl.when(k == 0)
    def _():
        acc_ref[...] = jnp.zeros_like(acc_ref)

    acc_ref[...] += _dot(a_ref[...], w_ref[...])

    @pl.when(k == pl.num_programs(2) - 1)
    def _():
        o_ref[...] = h_ref[...] + acc_ref[...]


def _ffn_down(act, w_down, h2d, *, tm=1024, tn=1024, tk=1024):
    T, F = act.shape
    N = w_down.shape[1]
    tm, tn, tk = min(tm, T), min(tn, N), min(tk, F)
    return pl.pallas_call(
        _ffn_down_kernel,
        grid=(T // tm, N // tn, F // tk),
        in_specs=[
            pl.BlockSpec((tm, tk), lambda i, j, k: (i, k)),
            pl.BlockSpec((tk, tn), lambda i, j, k: (k, j)),
            pl.BlockSpec((tm, tn), lambda i, j, k: (i, j)),
        ],
        out_specs=pl.BlockSpec((tm, tn), lambda i, j, k: (i, j)),
        out_shape=jax.ShapeDtypeStruct((T, N), F32),
        scratch_shapes=[pltpu.VMEM((tm, tn), F32)],
        compiler_params=_cparams("parallel", "parallel", "arbitrary"),
        name="ffn_down",
    )(act, w_down, h2d)


def _rope_tables(S):
    inv_freq = ROPE_BASE ** (-jnp.linspace(0.0, 1.0, HEAD_DIM // 2, dtype=F32))
    ang = jnp.arange(S, dtype=F32)[:, None] * inv_freq[None, :]
    cos, sin = jnp.cos(ang), jnp.sin(ang)
    return jnp.concatenate([cos, cos], axis=-1), jnp.concatenate([-sin, sin], axis=-1)


def _retention_tables(H):
    C = RET_CHUNK
    log_g = jnp.log1p(-jnp.exp2(-5.0 - jnp.arange(H, dtype=F32)))
    idx = jnp.arange(C, dtype=F32)
    diff = idx[:, None] - idx[None, :]
    dint = jnp.where(diff >= 0, jnp.exp(jnp.maximum(diff, 0.0)[None] * log_g[:, None, None]), 0.0)
    kdec = jnp.exp((C - 1 - idx)[None, :] * log_g[:, None])
    qdec = jnp.exp((idx + 1.0)[None, :] * log_g[:, None])
    cdec = jnp.exp(C * log_g)
    lanes = lambda t: jnp.broadcast_to(t[..., None], t.shape + (HEAD_DIM,))
    return dint, lanes(kdec), lanes(qdec), lanes(cdec[:, None])


def _pad_cols(a, n):
    return jnp.pad(a, ((0, 0), (0, n - a.shape[1])))


def _layer(h, mem, attn_norm, w_in, ret_norm, sb_q_norm, sb_k_norm, sb_out_norm, w_out, cross_norm, mem_norm,
           cross_w_q, cross_w_kv, cross_q_norm, cross_k_norm, cross_w_o, ffn_norm, ffn_w_up, ffn_conv_w,
           ffn_conv_b, ffn_w_down):
    B, S, D = h.shape
    T = B * S
    W = D // 2
    H = W // HEAD_DIM
    d_ff = ffn_w_down.shape[0]
    tn = min(512, W)
    seg = W // tn
    x2d = h.reshape(T, D)

    xn = _rmsnorm_rows(x2d, attn_norm, BF16)
    w_in_b = w_in.astype(BF16)
    cos2, sin2 = _rope_tables(S)
    tm_in = min(1024, S)
    tiles_per_seq = S // tm_in
    rope_spec = pl.BlockSpec((tm_in, HEAD_DIM), lambda i, j: (i % tiles_per_seq, 0))
    ret_qk = _inproj(xn, w_in_b, B, S, n_col_tiles=2 * seg, col_map=lambda j: j, mode="rope", out_dtype=F32,
                     extra=(cos2, sin2), extra_specs=(rope_spec, rope_spec), n_scaled_tiles=seg, tn=tn,
                     name="inproj_ret_qk")
    v_all = _inproj(xn, w_in_b, B, S, n_col_tiles=2 * seg,
                    col_map=lambda j: jnp.where(j < seg, 2 * seg + j, 5 * seg + j), mode="plain",
                    out_dtype=BF16, tn=tn, name="inproj_v")
    gate = _inproj(xn, w_in_b, B, S, n_col_tiles=seg, col_map=lambda j: 3 * seg + j, mode="plain",
                   out_dtype=F32, tn=tn, name="inproj_gate")
    sb_gains = jnp.stack([sb_q_norm, sb_k_norm]).reshape(2, 1, HEAD_DIM)
    gain_spec = pl.BlockSpec((1, 1, HEAD_DIM), lambda i, j: (j // seg, 0, 0))
    sb_qk = _inproj(xn, w_in_b, B, S, n_col_tiles=2 * seg, col_map=lambda j: 4 * seg + j, mode="norm",
                    out_dtype=BF16, extra=(sb_gains,), extra_specs=(gain_spec,), tn=tn, name="inproj_sb_qk")

    ret = _retention(ret_qk, v_all, gate, _retention_tables(H), ret_norm)
    sb = _stick_breaking(sb_qk, v_all, sb_out_norm)
    h1 = _outproj(ret.reshape(T, W), sb.reshape(T, W), w_out.astype(BF16), x2d)

    kmem, vmem = _memkv(mem, mem_norm, cross_w_kv.astype(BF16), cross_k_norm)
    h2, xn_ffn = _cross_attention(h1, S, cross_norm, cross_w_q.astype(BF16), cross_q_norm, kmem, vmem,
                                  cross_w_o.astype(BF16), ffn_norm)

    f_pad = -(-d_ff // 1024) * 1024
    w_up_b = ffn_w_up.astype(BF16)
    wg, wv = _pad_cols(w_up_b[:, :d_ff], f_pad), _pad_cols(w_up_b[:, d_ff:], f_pad)
    cwg, cwv = _pad_cols(ffn_conv_w[:, :d_ff], f_pad), _pad_cols(ffn_conv_w[:, d_ff:], f_pad)
    cb = ffn_conv_b.reshape(1, 2 * d_ff)
    cbg, cbv = _pad_cols(cb[:, :d_ff], f_pad), _pad_cols(cb[:, d_ff:], f_pad)
    act = _ffn_up(xn_ffn, S, wg, wv, cwg, cwv, cbg, cbv)
    w_down_b = jnp.pad(ffn_w_down.astype(BF16), ((0, f_pad - d_ff), (0, 0)))
    out = _ffn_down(act, w_down_b, h2)
    return out.reshape(B, S, D)


def kernel(x, mem, attn_norm, w_in, ret_norm, sb_q_norm, sb_k_norm, sb_out_norm, w_out, cross_norm, mem_norm,
           cross_w_q, cross_w_kv, cross_q_norm, cross_k_norm, cross_w_o, ffn_norm, ffn_w_up, ffn_conv_w,
           ffn_conv_b, ffn_w_down):
    h = x
    for l in range(attn_norm.shape[0]):
        h = _layer(h, mem, attn_norm[l], w_in[l], ret_norm[l], sb_q_norm[l], sb_k_norm[l], sb_out_norm[l],
                   w_out[l], cross_norm[l], mem_norm[l], cross_w_q[l], cross_w_kv[l], cross_q_norm[l],
                   cross_k_norm[l], cross_w_o[l], ffn_norm[l], ffn_w_up[l], ffn_conv_w[l], ffn_conv_b[l],
                   ffn_w_down[l])
    return h
```

```python
import functools

import jax
import jax.numpy as jnp
from jax import lax
from jax.experimental import pallas as pl
from jax.experimental.pallas import tpu as pltpu

HEAD_DIM = 128
EPS = 1e-6
ROPE_BASE = 10000.0
RET_CHUNK = 128
N_CROSS_HEADS = 4
CONV_WIDTH = 3
SUBLANES = 8

VMEM_LIMIT_BYTES = 56 * 1024 * 1024

ROW_CHUNK = 256

SB_LOG_UNDERFLOW = -104.0

F32 = jnp.float32
BF16 = jnp.bfloat16


def _cparams(*sem):
    return pltpu.CompilerParams(dimension_semantics=sem, vmem_limit_bytes=VMEM_LIMIT_BYTES)


def _rms(x, g):
    return x * lax.rsqrt(jnp.mean(x * x, axis=-1, keepdims=True) + EPS) * g


def _silu(x):
    return x * (1.0 / (1.0 + jnp.exp(-x)))


def _dot(a, b):
    return jnp.dot(a, b, preferred_element_type=F32)


def _dot_nt(a, b):
    return lax.dot_general(a, b, (((1,), (1,)), ((), ())), preferred_element_type=F32)


def _row_chunks(n_rows):
    rc = min(ROW_CHUNK, n_rows)
    return [slice(r0, r0 + rc) for r0 in range(0, n_rows, rc)]


def _rmsnorm_kernel(x_ref, g_ref, o_ref):
    o_ref[...] = _rms(x_ref[...], g_ref[...]).astype(o_ref.dtype)


def _rmsnorm_rows(x2d, g, out_dtype, tm=256):
    T, D = x2d.shape
    tm = min(tm, T)
    return pl.pallas_call(
        _rmsnorm_kernel,
        grid=(T // tm,),
        in_specs=[pl.BlockSpec((tm, D), lambda i: (i, 0)), pl.BlockSpec((1, D), lambda i: (0, 0))],
        out_specs=pl.BlockSpec((tm, D), lambda i: (i, 0)),
        out_shape=jax.ShapeDtypeStruct((T, D), out_dtype),
        compiler_params=_cparams("parallel"),
        name="rmsnorm_rows",
    )(x2d, g.reshape(1, D))


def _inproj_kernel(a_ref, w_ref, *rest, mode, heads_per_tile, n_scaled_tiles):
    j = pl.program_id(1)
    o_ref = rest[-1]
    if mode == "rope":
        scale = jnp.where(j < n_scaled_tiles, HEAD_DIM**-0.5, 1.0).astype(F32)
    elif mode == "norm":
        gain = rest[0][0]
    for rows in _row_chunks(a_ref.shape[0]):
        acc = _dot(a_ref[rows, :], w_ref[...])
        for hh in range(heads_per_tile):
            xh = acc[:, hh * HEAD_DIM : (hh + 1) * HEAD_DIM]
            if mode == "rope":
                xh = (xh * rest[0][rows, :] + pltpu.roll(xh, HEAD_DIM // 2, axis=1) * rest[1][rows, :]) * scale
            elif mode == "norm":
                xh = _rms(xh, gain)
            o_ref[0, hh, rows, :] = xh.astype(o_ref.dtype)


def _inproj(xn, w, batch, seq, *, n_col_tiles, col_map, mode, out_dtype, extra=(), extra_specs=(),
            n_scaled_tiles=0, tm=1024, tn=512, name):
    T, D = xn.shape
    tm = min(tm, seq)
    hpt = tn // HEAD_DIM
    tiles_per_seq = seq // tm
    kern = functools.partial(_inproj_kernel, mode=mode, heads_per_tile=hpt, n_scaled_tiles=n_scaled_tiles)
    return pl.pallas_call(
        kern,
        grid=(T // tm, n_col_tiles),
        in_specs=[
            pl.BlockSpec((tm, D), lambda i, j: (i, 0)),
            pl.BlockSpec((D, tn), lambda i, j: (0, col_map(j))),
            *extra_specs,
        ],
        out_specs=pl.BlockSpec(
            (1, hpt, tm, HEAD_DIM), lambda i, j: (i // tiles_per_seq, j, i % tiles_per_seq, 0)
        ),
        out_shape=jax.ShapeDtypeStruct((batch, n_col_tiles * hpt, seq, HEAD_DIM), out_dtype),
        compiler_params=_cparams("parallel", "arbitrary"),
        name=name,
    )(xn, w, *extra)


def _retention_kernel(q_ref, k_ref, v_ref, g_ref, dint_ref, kdec_ref, qdec_ref, cdec_ref, gn_ref, o_ref,
                      state_ref, *, n_chunks):
    @pl.when(pl.program_id(2) == 0)
    def _():
        state_ref[...] = jnp.zeros_like(state_ref)

    dint = dint_ref[0]
    kdec = kdec_ref[0]
    qdec = qdec_ref[0]
    cdec = cdec_ref[0]
    gn = gn_ref[...]
    C = RET_CHUNK
    chunks = range(n_chunks)
    rows = [pl.ds(n * C, C) for n in chunks]
    q = [q_ref[0, 0, r, :] for r in rows]
    k = [k_ref[0, 0, r, :] for r in rows]
    scores = [_dot_nt(q[n].astype(BF16), k[n].astype(BF16)) for n in chunks]
    kv = [_dot((k[n] * kdec).T.astype(BF16), v_ref[0, 0, rows[n], :]) for n in chunks]
    states = [state_ref[...]]
    for n in chunks:
        states.append(states[n] * cdec + kv[n])
    state_ref[...] = states[n_chunks]
    intra = [_dot((scores[n] * dint).astype(BF16), v_ref[0, 0, rows[n], :]) for n in chunks]
    cross = [_dot((q[n] * qdec).astype(BF16), states[n].astype(BF16)) for n in chunks]
    for n in chunks:
        y = _rms(intra[n] + cross[n], gn) * _silu(g_ref[0, 0, rows[n], :])
        o_ref[0, rows[n], :] = y.astype(o_ref.dtype)


def _retention(qk, v_all, gate, tables, gn, *, rows_per_step=1024):
    B, H2, S, d = qk.shape
    H = H2 // 2
    tr = min(rows_per_step, S)
    dint, kdec, qdec, cdec = tables
    C = RET_CHUNK
    kern = functools.partial(_retention_kernel, n_chunks=tr // C)
    head_blk = (1, 1, tr, d)
    return pl.pallas_call(
        kern,
        grid=(B, H, S // tr),
        in_specs=[
            pl.BlockSpec(head_blk, lambda b, h, r: (b, h, r, 0)),
            pl.BlockSpec(head_blk, lambda b, h, r: (b, H + h, r, 0)),
            pl.BlockSpec(head_blk, lambda b, h, r: (b, h, r, 0)),
            pl.BlockSpec(head_blk, lambda b, h, r: (b, h, r, 0)),
            pl.BlockSpec((1, C, C), lambda b, h, r: (h, 0, 0)),
            pl.BlockSpec((1, C, d), lambda b, h, r: (h, 0, 0)),
            pl.BlockSpec((1, C, d), lambda b, h, r: (h, 0, 0)),
            pl.BlockSpec((1, 1, d), lambda b, h, r: (h, 0, 0)),
            pl.BlockSpec((1, d), lambda b, h, r: (0, 0)),
        ],
        out_specs=pl.BlockSpec((1, tr, d), lambda b, h, r: (b, r, h)),
        out_shape=jax.ShapeDtypeStruct((B, S, H * d), BF16),
        scratch_shapes=[pltpu.VMEM((d, d), F32)],
        compiler_params=_cparams("parallel", "parallel", "arbitrary"),
        name="retention",
    )(qk, qk, v_all, gate, dint, kdec, qdec, cdec, gn.reshape(1, d))


def _sb_kernel(q_ref, k_ref, v_ref, w2_ref, gn_ref, o_ref, *, tq, tk, group):
    scale = HEAD_DIM**-0.5
    w2 = w2_ref[...]
    gn = gn_ref[...]
    tk2 = 2 * tk
    qi = pl.program_id(2)
    heads = range(group)
    qbs = [q_ref[0, g] for g in heads]

    def fold(kstart, runs, accs, mask):
        keys = pl.ds(kstart, tk2)
        z = [_dot_nt(qbs[g], k_ref[0, g, keys, :]) * scale for g in heads]
        hi, lo = [], []
        for g in heads:
            nz = -z[g]
            ls = jnp.minimum(nz, 0.0) - jnp.log(1.0 + jnp.exp(jnp.minimum(z[g], nz)))
            if mask is not None:
                ls = jnp.where(mask, ls, 0.0)
            h = ls.astype(BF16)
            hi.append(h)
            lo.append((ls - h.astype(F32)).astype(BF16))
        cr_new = [_dot(jnp.concatenate([hi[g][:, tk:], lo[g][:, tk:]], axis=1), w2) for g in heads]
        cr_old = [_dot(jnp.concatenate([hi[g][:, :tk], lo[g][:, :tk]], axis=1), w2) for g in heads]
        a, new_runs = [], []
        for g in heads:
            run_mid = runs[g] + cr_new[g][:, tk:]
            w = jnp.exp(z[g] + jnp.concatenate([cr_old[g][:, :tk] + run_mid, cr_new[g][:, :tk] + runs[g]], axis=1))
            if mask is not None:
                w = jnp.where(mask, w, 0.0)
            a.append(w.astype(BF16))
            new_runs.append(run_mid + cr_old[g][:, tk:])
        new_accs = [accs[g] + _dot(a[g], v_ref[0, g, keys, :]) for g in heads]
        return tuple(new_runs), tuple(new_accs)

    runs = tuple(jnp.zeros((tq, tk), F32) for _ in heads)
    accs = tuple(jnp.zeros((tq, HEAD_DIM), F32) for _ in heads)
    mask = lax.broadcasted_iota(jnp.int32, (tq, tk2), 1) < lax.broadcasted_iota(jnp.int32, (tq, tk2), 0)
    runs, accs = fold(pl.multiple_of(qi * tq, tq), runs, accs, mask)

    def cond(c):
        return jnp.logical_and(c[0] >= 0, jnp.max(functools.reduce(jnp.maximum, c[1])) > SB_LOG_UNDERFLOW)

    def body(c):
        kblk, runs, accs = c
        runs, accs = fold(pl.multiple_of(kblk * tk2, tk2), runs, accs, None)
        return kblk - 1, runs, accs

    _, _, accs = lax.while_loop(cond, body, (qi - 1, runs, accs))
    for g in heads:
        o_ref[0, :, g * HEAD_DIM : (g + 1) * HEAD_DIM] = _rms(accs[g], gn).astype(o_ref.dtype)


def _stick_breaking(sqk, v_all, gn, *, tk=128, group=4):
    B, H2, S, d = sqk.shape
    H = H2 // 2
    tq = 2 * tk
    group = min(group, H)
    n_groups = H // group
    idx = jnp.arange(tk)
    w2 = jnp.concatenate([(idx[:, None] >= idx[None, :]).astype(BF16), jnp.ones((tk, tk), BF16)], axis=1)
    w2 = jnp.concatenate([w2, w2], axis=0)
    kern = functools.partial(_sb_kernel, tq=tq, tk=tk, group=group)
    kv_blk = (1, group, S, d)
    return pl.pallas_call(
        kern,
        grid=(B, n_groups, S // tq),
        in_specs=[
            pl.BlockSpec((1, group, tq, d), lambda b, h, qi: (b, h, qi, 0)),
            pl.BlockSpec(kv_blk, lambda b, h, qi: (b, n_groups + h, 0, 0)),
            pl.BlockSpec(kv_blk, lambda b, h, qi: (b, n_groups + h, 0, 0)),
            pl.BlockSpec((2 * tk, 2 * tk), lambda b, h, qi: (0, 0)),
            pl.BlockSpec((1, d), lambda b, h, qi: (0, 0)),
        ],
        out_specs=pl.BlockSpec((1, tq, group * d), lambda b, h, qi: (b, qi, h)),
        out_shape=jax.ShapeDtypeStruct((B, S, H * d), BF16),
        compiler_params=_cparams("parallel", "parallel", "arbitrary"),
        name="stick_breaking",
    )(sqk, sqk, v_all, w2, gn.reshape(1, d))


def _outproj_kernel(a1_ref, a2_ref, w1_ref, w2_ref, x_ref, o_ref):
    for rows in _row_chunks(o_ref.shape[0]):
        acc = _dot(a1_ref[rows, :], w1_ref[...]) + _dot(a2_ref[rows, :], w2_ref[...])
        o_ref[rows, :] = x_ref[rows, :] + acc


def _outproj(ret2d, sb2d, w_out, x2d, *, tm=1024, tn=512):
    T, K = ret2d.shape
    N = w_out.shape[1]
    tm = min(tm, T)
    return pl.pallas_call(
        _outproj_kernel,
        grid=(T // tm, N // tn),
        in_specs=[
            pl.BlockSpec((tm, K), lambda i, j: (i, 0)),
            pl.BlockSpec((tm, K), lambda i, j: (i, 0)),
            pl.BlockSpec((K, tn), lambda i, j: (0, j)),
            pl.BlockSpec((K, tn), lambda i, j: (1, j)),
            pl.BlockSpec((tm, tn), lambda i, j: (i, j)),
        ],
        out_specs=pl.BlockSpec((tm, tn), lambda i, j: (i, j)),
        out_shape=jax.ShapeDtypeStruct((T, N), F32),
        compiler_params=_cparams("parallel", "arbitrary"),
        name="outproj",
    )(ret2d, sb2d, w_out, w_out, x2d)


def _memkv_kernel(m_ref, gm_ref, w_ref, gk_ref, k_ref, v_ref):
    mn = _rms(m_ref[0], gm_ref[...]).astype(BF16)
    kv = _dot(mn, w_ref[...])
    cw = kv.shape[1] // 2
    gk = gk_ref[...]
    for hh in range(cw // HEAD_DIM):
        sl = slice(hh * HEAD_DIM, (hh + 1) * HEAD_DIM)
        k_ref[0, :, sl] = _rms(kv[:, sl], gk).astype(k_ref.dtype)
    v_ref[0] = kv[:, cw:].astype(v_ref.dtype)


def _memkv(mem, g_mem, w_kv, g_k):
    B, M, D = mem.shape
    cw = w_kv.shape[1] // 2
    return pl.pallas_call(
        _memkv_kernel,
        grid=(B,),
        in_specs=[
            pl.BlockSpec((1, M, D), lambda b: (b, 0, 0)),
            pl.BlockSpec((1, D), lambda b: (0, 0)),
            pl.BlockSpec((D, 2 * cw), lambda b: (0, 0)),
            pl.BlockSpec((1, HEAD_DIM), lambda b: (0, 0)),
        ],
        out_specs=[pl.BlockSpec((1, M, cw), lambda b: (b, 0, 0))] * 2,
        out_shape=[jax.ShapeDtypeStruct((B, M, cw), BF16)] * 2,
        compiler_params=_cparams("parallel"),
        name="memkv",
    )(mem, g_mem.reshape(1, D), w_kv, g_k.reshape(1, HEAD_DIM))


def _cross_kernel(h_ref, gc_ref, wq_ref, gq_ref, k_ref, v_ref, wo_ref, gf_ref, h2_ref, xn_ref):
    h = h_ref[...]
    xn = _rms(h, gc_ref[...]).astype(BF16)
    q = _dot(xn, wq_ref[...])
    gq = gq_ref[...]
    scale = HEAD_DIM**-0.5
    outs = []
    for hh in range(q.shape[1] // HEAD_DIM):
        sl = slice(hh * HEAD_DIM, (hh + 1) * HEAD_DIM)
        qh = _rms(q[:, sl], gq).astype(BF16)
        s = _dot_nt(qh, k_ref[0, :, sl]) * scale
        e = jnp.exp(s - jnp.max(s, axis=-1, keepdims=True))
        p = e / jnp.sum(e, axis=-1, keepdims=True)
        outs.append(_dot(p.astype(BF16), v_ref[0, :, sl]).astype(BF16))
    h2 = h + _dot(jnp.concatenate(outs, axis=-1), wo_ref[...])
    h2_ref[...] = h2
    xn_ref[...] = _rms(h2, gf_ref[...]).astype(xn_ref.dtype)


def _cross_attention(h2d, seq, g_cross, w_q, g_q, kmem, vmem, w_o, g_ffn, *, tm=256):
    T, D = h2d.shape
    _, M, cw = kmem.shape
    tm = min(tm, seq)
    tiles_per_seq = seq // tm
    const = lambda i: (0, 0)
    return pl.pallas_call(
        _cross_kernel,
        grid=(T // tm,),
        in_specs=[
            pl.BlockSpec((tm, D), lambda i: (i, 0)),
            pl.BlockSpec((1, D), const),
            pl.BlockSpec((D, cw), const),
            pl.BlockSpec((1, HEAD_DIM), const),
            pl.BlockSpec((1, M, cw), lambda i: (i // tiles_per_seq, 0, 0)),
            pl.BlockSpec((1, M, cw), lambda i: (i // tiles_per_seq, 0, 0)),
            pl.BlockSpec((cw, D), const),
            pl.BlockSpec((1, D), const),
        ],
        out_specs=[pl.BlockSpec((tm, D), lambda i: (i, 0))] * 2,
        out_shape=[jax.ShapeDtypeStruct((T, D), F32), jax.ShapeDtypeStruct((T, D), BF16)],
        compiler_params=_cparams("parallel"),
        name="cross_attention",
    )(h2d, g_cross.reshape(1, D), w_q, g_q.reshape(1, HEAD_DIM), kmem, vmem, w_o, g_ffn.reshape(1, D))


CONV_PIECE_ROWS = 64
LANES = 128


def _ffn_up_kernel(a_ref, wg_ref, wv_ref, cwg_ref, cwv_ref, cbg_ref, cbv_ref, o_ref, ubuf_ref, carry_ref,
                   *, tiles_per_seq):
    i = pl.program_id(0)
    j = pl.program_id(1)

    @pl.when(jnp.logical_and(i == 0, j == 0))
    def _():
        carry_ref[...] = jnp.zeros_like(carry_ref)

    seq_start = (i % tiles_per_seq) == 0
    halo = SUBLANES
    chunks = _row_chunks(a_ref.shape[0])
    rc = chunks[0].stop
    piece = min(CONV_PIECE_ROWS, rc)
    lane_tiles = range(o_ref.shape[1] // LANES)
    projections = ((wg_ref, cwg_ref, cbg_ref), (wv_ref, cwv_ref, cbv_ref))

    def project(c):
        par = c % 2
        a = a_ref[chunks[c], :]
        for slot, (w_ref, _, _) in enumerate(projections):
            u = _dot(a, w_ref[...])
            for lt in lane_tiles:
                if c == 0:
                    ubuf_ref[par, slot, lt, 0:halo, :] = jnp.where(seq_start, 0.0, carry_ref[j, slot, lt])
                else:
                    ubuf_ref[par, slot, lt, 0:halo, :] = ubuf_ref[1 - par, slot, lt, rc : rc + halo, :]
                ubuf_ref[par, slot, lt, halo : halo + rc, :] = u[:, lt * LANES : (lt + 1) * LANES]

    def activate(c):
        par = c % 2
        for lt in lane_tiles:
            cols = slice(lt * LANES, (lt + 1) * LANES)
            for p0 in range(0, rc, piece):

                def conv(slot):
                    _, cw_ref, cb_ref = projections[slot]
                    cw = cw_ref[:, cols]
                    c2 = cb_ref[:, cols] + ubuf_ref[par, slot, lt, pl.ds(p0 + halo - 2, piece), :] * cw[0:1, :]
                    c2 = c2 + ubuf_ref[par, slot, lt, pl.ds(p0 + halo - 1, piece), :] * cw[1:2, :]
                    return c2 + ubuf_ref[par, slot, lt, pl.ds(p0 + halo, piece), :] * cw[2:3, :]

                out_rows = slice(chunks[c].start + p0, chunks[c].start + p0 + piece)
                o_ref[out_rows, cols] = (_silu(conv(0)) * conv(1)).astype(o_ref.dtype)

    project(0)
    for c in range(1, len(chunks)):
        project(c)
        activate(c - 1)
    activate(len(chunks) - 1)
    last = (len(chunks) - 1) % 2
    for slot in range(2):
        for lt in lane_tiles:
            carry_ref[j, slot, lt] = ubuf_ref[last, slot, lt, rc : rc + halo, :]


def _ffn_up(xn, seq, w_up, conv_w, conv_b, *, tm=2048, tf=256):
    T, D = xn.shape
    F = w_up.shape[1] // 2
    tm = min(tm, seq)
    assert F % tf == 0 and seq % tm == 0
    n_j = F // tf
    kern = functools.partial(_ffn_up_kernel, tiles_per_seq=seq // tm)
    gate_col = lambda i, j: (0, j)
    value_col = lambda i, j: (0, n_j + j)
    return pl.pallas_call(
        kern,
        grid=(T // tm, n_j),
        in_specs=[
            pl.BlockSpec((tm, D), lambda i, j: (i, 0)),
            pl.BlockSpec((D, tf), gate_col),
            pl.BlockSpec((D, tf), value_col),
            pl.BlockSpec((CONV_WIDTH, tf), gate_col),
            pl.BlockSpec((CONV_WIDTH, tf), value_col),
            pl.BlockSpec((1, tf), gate_col),
            pl.BlockSpec((1, tf), value_col),
        ],
        out_specs=pl.BlockSpec((tm, tf), lambda i, j: (i, j)),
        out_shape=jax.ShapeDtypeStruct((T, F), BF16),
        scratch_shapes=[
            pltpu.VMEM((2, 2, tf // LANES, SUBLANES + min(ROW_CHUNK, tm), LANES), F32),
            pltpu.VMEM((n_j, 2, tf // LANES, SUBLANES, LANES), F32),
        ],
        compiler_params=_cparams("arbitrary", "arbitrary"),
        name="ffn_up",
    )(xn, w_up, w_up, conv_w, conv_w, conv_b, conv_b)


def _ffn_down_kernel(a_ref, w_ref, h_ref, o_ref):
    for rows in _row_chunks(o_ref.shape[0]):
        o_ref[rows, :] = h_ref[rows, :] + _dot(a_ref[rows, :], w_ref[...])


def _ffn_down(act, w_down, h2d, *, tm=512, tn=512):
    T, F = act.shape
    N = w_down.shape[1]
    tm, tn = min(tm, T), min(tn, N)
    return pl.pallas_call(
        _ffn_down_kernel,
        grid=(T // tm, N // tn),
        in_specs=[
            pl.BlockSpec((tm, F), lambda i, j: (i, 0)),
            pl.BlockSpec((F, tn), lambda i, j: (0, j)),
            pl.BlockSpec((tm, tn), lambda i, j: (i, j)),
        ],
        out_specs=pl.BlockSpec((tm, tn), lambda i, j: (i, j)),
        out_shape=jax.ShapeDtypeStruct((T, N), F32),
        compiler_params=_cparams("parallel", "arbitrary"),
        name="ffn_down",
    )(act, w_down, h2d)


def _rope_tables(S):
    inv_freq = ROPE_BASE ** (-jnp.linspace(0.0, 1.0, HEAD_DIM // 2, dtype=F32))
    ang = jnp.arange(S, dtype=F32)[:, None] * inv_freq[None, :]
    cos, sin = jnp.cos(ang), jnp.sin(ang)
    return jnp.concatenate([cos, cos], axis=-1), jnp.concatenate([-sin, sin], axis=-1)


def _retention_tables(H):
    C = RET_CHUNK
    log_g = jnp.log1p(-jnp.exp2(-5.0 - jnp.arange(H, dtype=F32)))
    idx = jnp.arange(C, dtype=F32)
    diff = idx[:, None] - idx[None, :]
    dint = jnp.where(diff >= 0, jnp.exp(jnp.maximum(diff, 0.0)[None] * log_g[:, None, None]), 0.0)
    kdec = jnp.exp((C - 1 - idx)[None, :] * log_g[:, None])
    qdec = jnp.exp((idx + 1.0)[None, :] * log_g[:, None])
    cdec = jnp.exp(C * log_g)
    lanes = lambda t: jnp.broadcast_to(t[..., None], t.shape + (HEAD_DIM,))
    return dint, lanes(kdec), lanes(qdec), lanes(cdec[:, None])


def _layer(h, mem, attn_norm, w_in, ret_norm, sb_q_norm, sb_k_norm, sb_out_norm, w_out, cross_norm, mem_norm,
           cross_w_q, cross_w_kv, cross_q_norm, cross_k_norm, cross_w_o, ffn_norm, ffn_w_up, ffn_conv_w,
           ffn_conv_b, ffn_w_down):
    B, S, D = h.shape
    T = B * S
    W = D // 2
    H = W // HEAD_DIM
    tn = min(512, W)
    seg = W // tn
    x2d = h.reshape(T, D)

    xn = _rmsnorm_rows(x2d, attn_norm, BF16)
    w_in_b = w_in.astype(BF16)
    cos2, sin2 = _rope_tables(S)
    tm_in = min(1024, S)
    tiles_per_seq = S // tm_in
    rope_spec = pl.BlockSpec((tm_in, HEAD_DIM), lambda i, j: (i % tiles_per_seq, 0))
    ret_qk = _inproj(xn, w_in_b, B, S, n_col_tiles=2 * seg, col_map=lambda j: j, mode="rope", out_dtype=F32,
                     extra=(cos2, sin2), extra_specs=(rope_spec, rope_spec), n_scaled_tiles=seg, tn=tn,
                     name="inproj_ret_qk")
    v_all = _inproj(xn, w_in_b, B, S, n_col_tiles=2 * seg,
                    col_map=lambda j: jnp.where(j < seg, 2 * seg + j, 5 * seg + j), mode="plain",
                    out_dtype=BF16, tn=tn, name="inproj_v")
    gate = _inproj(xn, w_in_b, B, S, n_col_tiles=seg, col_map=lambda j: 3 * seg + j, mode="plain",
                   out_dtype=F32, tn=tn, name="inproj_gate")
    sb_gains = jnp.stack([sb_q_norm, sb_k_norm]).reshape(2, 1, HEAD_DIM)
    gain_spec = pl.BlockSpec((1, 1, HEAD_DIM), lambda i, j: (j // seg, 0, 0))
    sb_qk = _inproj(xn, w_in_b, B, S, n_col_tiles=2 * seg, col_map=lambda j: 4 * seg + j, mode="norm",
                    out_dtype=BF16, extra=(sb_gains,), extra_specs=(gain_spec,), tn=tn, name="inproj_sb_qk")

    ret = _retention(ret_qk, v_all, gate, _retention_tables(H), ret_norm)
    sb = _stick_breaking(sb_qk, v_all, sb_out_norm)
    h1 = _outproj(ret.reshape(T, W), sb.reshape(T, W), w_out.astype(BF16), x2d)

    kmem, vmem = _memkv(mem, mem_norm, cross_w_kv.astype(BF16), cross_k_norm)
    h2, xn_ffn = _cross_attention(h1, S, cross_norm, cross_w_q.astype(BF16), cross_q_norm, kmem, vmem,
                                  cross_w_o.astype(BF16), ffn_norm)

    act = _ffn_up(xn_ffn, S, ffn_w_up.astype(BF16), ffn_conv_w, ffn_conv_b.reshape(1, -1))
    out = _ffn_down(act, ffn_w_down.astype(BF16), h2)
    return out.reshape(B, S, D)


def kernel(x, mem, attn_norm, w_in, ret_norm, sb_q_norm, sb_k_norm, sb_out_norm, w_out, cross_norm, mem_norm,
           cross_w_q, cross_w_kv, cross_q_norm, cross_k_norm, cross_w_o, ffn_norm, ffn_w_up, ffn_conv_w,
           ffn_conv_b, ffn_w_down):
    h = x
    for l in range(attn_norm.shape[0]):
        h = _layer(h, mem, attn_norm[l], w_in[l], ret_norm[l], sb_q_norm[l], sb_k_norm[l], sb_out_norm[l],
                   w_out[l], cross_norm[l], mem_norm[l], cross_w_q[l], cross_w_kv[l], cross_q_norm[l],
                   cross_k_norm[l], cross_w_o[l], ffn_norm[l], ffn_w_up[l], ffn_conv_w[l], ffn_conv_b[l],
                   ffn_w_down[l])
    return h
```

```python
import functools

import jax
import jax.numpy as jnp
from jax import lax
from jax.experimental import pallas as pl
from jax.experimental.pallas import tpu as pltpu

HEAD_DIM = 128
EPS = 1e-6
ROPE_BASE = 10000.0
RET_CHUNK = 128
N_CROSS_HEADS = 4
CONV_WIDTH = 3
SUBLANES = 8

VMEM_LIMIT_BYTES = 56 * 1024 * 1024

ROW_CHUNK = 256

SB_EXP2_UNDERFLOW = 150.0
LOG2_E = 1.4426950408889634

F32 = jnp.float32
BF16 = jnp.bfloat16


def _cparams(*sem):
    return pltpu.CompilerParams(dimension_semantics=sem, vmem_limit_bytes=VMEM_LIMIT_BYTES)


def _rms(x, g):
    return x * lax.rsqrt(jnp.mean(x * x, axis=-1, keepdims=True) + EPS) * g


def _silu(x):
    return x * (1.0 / (1.0 + jnp.exp(-x)))


def _dot(a, b):
    return jnp.dot(a, b, preferred_element_type=F32)


def _dot_nt(a, b):
    return lax.dot_general(a, b, (((1,), (1,)), ((), ())), preferred_element_type=F32)


def _row_chunks(n_rows):
    rc = min(ROW_CHUNK, n_rows)
    return [slice(r0, r0 + rc) for r0 in range(0, n_rows, rc)]


def _rmsnorm_kernel(x_ref, g_ref, o_ref):
    o_ref[...] = _rms(x_ref[...], g_ref[...]).astype(o_ref.dtype)


def _rmsnorm_rows(x2d, g, out_dtype, tm=256):
    T, D = x2d.shape
    tm = min(tm, T)
    return pl.pallas_call(
        _rmsnorm_kernel,
        grid=(T // tm,),
        in_specs=[pl.BlockSpec((tm, D), lambda i: (i, 0)), pl.BlockSpec((1, D), lambda i: (0, 0))],
        out_specs=pl.BlockSpec((tm, D), lambda i: (i, 0)),
        out_shape=jax.ShapeDtypeStruct((T, D), out_dtype),
        compiler_params=_cparams("parallel"),
        name="rmsnorm_rows",
    )(x2d, g.reshape(1, D))


def _cast_block(src_ref, dst_ref, step, n_blocks, n_steps):
    if n_blocks == n_steps:
        dst_ref[...] = src_ref[...].astype(dst_ref.dtype)
    else:

        @pl.when(step < n_blocks)
        def _():
            dst_ref[...] = src_ref[...].astype(dst_ref.dtype)


def _cast_plan(w, n_steps, step_of):
    n_rows = w.shape[0]
    packed_rows = 16
    n_blocks = max(n for n in range(1, n_steps + 1) if n_rows % n == 0 and (n_rows // n) % packed_rows == 0)
    spec = lambda: pl.BlockSpec((n_rows // n_blocks, w.shape[1]),
                                lambda *g: (jnp.minimum(step_of(*g), n_blocks - 1), 0))
    return n_blocks, spec(), spec(), jax.ShapeDtypeStruct(w.shape, BF16)


def _inproj_kernel(a_ref, w_ref, *rest, mode, heads_per_tile, n_scaled_tiles, n_extra, cast_blocks):
    j = pl.program_id(1)
    extra = rest[:n_extra]
    if cast_blocks:
        src_ref, o_ref, dst_ref = rest[n_extra:]
        _cast_block(src_ref, dst_ref, pl.program_id(0) * pl.num_programs(1) + j, cast_blocks[0], cast_blocks[1])
    else:
        (o_ref,) = rest[n_extra:]
    if mode == "rope":
        scale = jnp.where(j < n_scaled_tiles, HEAD_DIM**-0.5, 1.0).astype(F32)
    elif mode == "norm":
        gain = extra[0][0]
    for rows in _row_chunks(a_ref.shape[0]):
        acc = _dot(a_ref[rows, :], w_ref[...])
        for hh in range(heads_per_tile):
            xh = acc[:, hh * HEAD_DIM : (hh + 1) * HEAD_DIM]
            if mode == "rope":
                xh = (xh * extra[0][rows, :] + pltpu.roll(xh, HEAD_DIM // 2, axis=1) * extra[1][rows, :]) * scale
            elif mode == "norm":
                xh = _rms(xh, gain)
            o_ref[0, hh, rows, :] = xh.astype(o_ref.dtype)


def _inproj(xn, w, batch, seq, *, n_col_tiles, col_map, mode, out_dtype, extra=(), extra_specs=(),
            n_scaled_tiles=0, tm=1024, tn=512, name, cast=None):
    T, D = xn.shape
    tm = min(tm, seq)
    hpt = tn // HEAD_DIM
    tiles_per_seq = seq // tm
    n_steps = (T // tm) * n_col_tiles
    in_specs = [
        pl.BlockSpec((tm, D), lambda i, j: (i, 0)),
        pl.BlockSpec((D, tn), lambda i, j: (0, col_map(j))),
        *extra_specs,
    ]
    out_specs = pl.BlockSpec((1, hpt, tm, HEAD_DIM), lambda i, j: (i // tiles_per_seq, j, i % tiles_per_seq, 0))
    out_shape = jax.ShapeDtypeStruct((batch, n_col_tiles * hpt, seq, HEAD_DIM), out_dtype)
    operands = [xn, w, *extra]
    cast_blocks = None
    if cast is not None:
        n_blocks, src_spec, dst_spec, dst_shape = _cast_plan(cast, n_steps, lambda i, j: i * n_col_tiles + j)
        in_specs.append(src_spec)
        out_specs, out_shape = [out_specs, dst_spec], [out_shape, dst_shape]
        operands.append(cast)
        cast_blocks = (n_blocks, n_steps)
    kern = functools.partial(_inproj_kernel, mode=mode, heads_per_tile=hpt, n_scaled_tiles=n_scaled_tiles,
                             n_extra=len(extra), cast_blocks=cast_blocks)
    return pl.pallas_call(
        kern,
        grid=(T // tm, n_col_tiles),
        in_specs=in_specs,
        out_specs=out_specs,
        out_shape=out_shape,
        compiler_params=_cparams("arbitrary", "arbitrary"),
        name=name,
    )(*operands)


def _retention_kernel(q_ref, k_ref, v_ref, g_ref, dint_ref, kdec_ref, qdec_ref, cdec_ref, gn_ref, o_ref,
                      state_ref, *, n_chunks):
    @pl.when(pl.program_id(2) == 0)
    def _():
        state_ref[...] = jnp.zeros_like(state_ref)

    dint = dint_ref[0]
    kdec = kdec_ref[0]
    qdec = qdec_ref[0]
    cdec = cdec_ref[0]
    gn = gn_ref[...]
    C = RET_CHUNK
    chunks = range(n_chunks)
    rows = [pl.ds(n * C, C) for n in chunks]
    q = [q_ref[0, 0, r, :] for r in rows]
    k = [k_ref[0, 0, r, :] for r in rows]
    scores = [_dot_nt(q[n].astype(BF16), k[n].astype(BF16)) for n in chunks]
    kv = [_dot((k[n] * kdec).T.astype(BF16), v_ref[0, 0, rows[n], :]) for n in chunks]
    states = [state_ref[...]]
    for n in chunks:
        states.append(states[n] * cdec + kv[n])
    state_ref[...] = states[n_chunks]
    intra = [_dot((scores[n] * dint).astype(BF16), v_ref[0, 0, rows[n], :]) for n in chunks]
    cross = [_dot((q[n] * qdec).astype(BF16), states[n].astype(BF16)) for n in chunks]
    for n in chunks:
        y = _rms(intra[n] + cross[n], gn) * _silu(g_ref[0, 0, rows[n], :])
        o_ref[0, rows[n], :] = y.astype(o_ref.dtype)


def _retention(qk, v_all, gate, tables, gn, *, rows_per_step=1024):
    B, H2, S, d = qk.shape
    H = H2 // 2
    tr = min(rows_per_step, S)
    dint, kdec, qdec, cdec = tables
    C = RET_CHUNK
    kern = functools.partial(_retention_kernel, n_chunks=tr // C)
    head_blk = (1, 1, tr, d)
    return pl.pallas_call(
        kern,
        grid=(B, H, S // tr),
        in_specs=[
            pl.BlockSpec(head_blk, lambda b, h, r: (b, h, r, 0)),
            pl.BlockSpec(head_blk, lambda b, h, r: (b, H + h, r, 0)),
            pl.BlockSpec(head_blk, lambda b, h, r: (b, h, r, 0)),
            pl.BlockSpec(head_blk, lambda b, h, r: (b, h, r, 0)),
            pl.BlockSpec((1, C, C), lambda b, h, r: (h, 0, 0)),
            pl.BlockSpec((1, C, d), lambda b, h, r: (h, 0, 0)),
            pl.BlockSpec((1, C, d), lambda b, h, r: (h, 0, 0)),
            pl.BlockSpec((1, 1, d), lambda b, h, r: (h, 0, 0)),
            pl.BlockSpec((1, d), lambda b, h, r: (0, 0)),
        ],
        out_specs=pl.BlockSpec((1, tr, d), lambda b, h, r: (b, r, h)),
        out_shape=jax.ShapeDtypeStruct((B, S, H * d), BF16),
        scratch_shapes=[pltpu.VMEM((d, d), F32)],
        compiler_params=_cparams("parallel", "parallel", "arbitrary"),
        name="retention",
    )(qk, qk, v_all, gate, dint, kdec, qdec, cdec, gn.reshape(1, d))


def _sb_kernel(q_ref, k_ref, v_ref, w2_ref, gn_ref, o_ref, *, tq, tk, group):
    scale = HEAD_DIM**-0.5
    w2 = w2_ref[...]
    gn = gn_ref[...]
    tk2 = 2 * tk
    qi = pl.program_id(2)
    heads = range(group)
    qbs = [q_ref[0, g] for g in heads]

    def fold(kstart, runs, accs, mask):
        keys = pl.ds(kstart, tk2)
        z2 = [_dot_nt(qbs[g], k_ref[0, g, keys, :]) * (scale * LOG2_E) for g in heads]
        hi, lo = [], []
        for g in heads:
            sp = jnp.maximum(z2[g], 0.0) + jnp.log2(1.0 + jnp.exp2(-jnp.abs(z2[g])))
            if mask is not None:
                sp = jnp.where(mask, sp, 0.0)
            h = sp.astype(BF16)
            hi.append(h)
            lo.append((sp - h.astype(F32)).astype(BF16))
        cr_new = [_dot(jnp.concatenate([hi[g][:, tk:], lo[g][:, tk:]], axis=1), w2) for g in heads]
        cr_old = [_dot(jnp.concatenate([hi[g][:, :tk], lo[g][:, :tk]], axis=1), w2) for g in heads]
        a, new_runs = [], []
        for g in heads:
            run_mid = runs[g] + cr_new[g][:, tk:]
            w = jnp.exp2(z2[g] - jnp.concatenate([cr_old[g][:, :tk] + run_mid, cr_new[g][:, :tk] + runs[g]], axis=1))
            if mask is not None:
                w = jnp.where(mask, w, 0.0)
            a.append(w.astype(BF16))
            new_runs.append(run_mid + cr_old[g][:, tk:])
        new_accs = [accs[g] + _dot(a[g], v_ref[0, g, keys, :]) for g in heads]
        return tuple(new_runs), tuple(new_accs)

    runs = tuple(jnp.zeros((tq, tk), F32) for _ in heads)
    accs = tuple(jnp.zeros((tq, HEAD_DIM), F32) for _ in heads)
    mask = lax.broadcasted_iota(jnp.int32, (tq, tk2), 1) < lax.broadcasted_iota(jnp.int32, (tq, tk2), 0)
    runs, accs = fold(pl.multiple_of(qi * tq, tq), runs, accs, mask)

    def cond(c):
        return jnp.logical_and(c[0] >= 0, jnp.min(functools.reduce(jnp.minimum, c[1])) < SB_EXP2_UNDERFLOW)

    def body(c):
        kblk, runs, accs = c
        runs, accs = fold(pl.multiple_of(kblk * tk2, tk2), runs, accs, None)
        return kblk - 1, runs, accs

    _, _, accs = lax.while_loop(cond, body, (qi - 1, runs, accs))
    for g in heads:
        o_ref[0, :, g * HEAD_DIM : (g + 1) * HEAD_DIM] = _rms(accs[g], gn).astype(o_ref.dtype)


def _stick_breaking(sqk, v_all, gn, *, tk=128, group=4):
    B, H2, S, d = sqk.shape
    H = H2 // 2
    tq = 2 * tk
    group = min(group, H)
    n_groups = H // group
    idx = jnp.arange(tk)
    w2 = jnp.concatenate([(idx[:, None] >= idx[None, :]).astype(BF16), jnp.ones((tk, tk), BF16)], axis=1)
    w2 = jnp.concatenate([w2, w2], axis=0)
    kern = functools.partial(_sb_kernel, tq=tq, tk=tk, group=group)
    kv_blk = (1, group, S, d)
    return pl.pallas_call(
        kern,
        grid=(B, n_groups, S // tq),
        in_specs=[
            pl.BlockSpec((1, group, tq, d), lambda b, h, qi: (b, h, qi, 0)),
            pl.BlockSpec(kv_blk, lambda b, h, qi: (b, n_groups + h, 0, 0)),
            pl.BlockSpec(kv_blk, lambda b, h, qi: (b, n_groups + h, 0, 0)),
            pl.BlockSpec((2 * tk, 2 * tk), lambda b, h, qi: (0, 0)),
            pl.BlockSpec((1, d), lambda b, h, qi: (0, 0)),
        ],
        out_specs=pl.BlockSpec((1, tq, group * d), lambda b, h, qi: (b, qi, h)),
        out_shape=jax.ShapeDtypeStruct((B, S, H * d), BF16),
        compiler_params=_cparams("parallel", "parallel", "arbitrary"),
        name="stick_breaking",
    )(sqk, sqk, v_all, w2, gn.reshape(1, d))


def _outproj_kernel(a1_ref, a2_ref, w1_ref, w2_ref, x_ref, src_ref, o_ref, dst_ref, *, cast_blocks):
    step = pl.program_id(0) * pl.num_programs(1) + pl.program_id(1)
    _cast_block(src_ref, dst_ref, step, cast_blocks[0], cast_blocks[1])
    for rows in _row_chunks(o_ref.shape[0]):
        acc = _dot(a1_ref[rows, :], w1_ref[...]) + _dot(a2_ref[rows, :], w2_ref[...])
        o_ref[rows, :] = x_ref[rows, :] + acc


def _outproj(ret2d, sb2d, w_out, x2d, cast, *, tm=1024, tn=512):
    T, K = ret2d.shape
    N = w_out.shape[1]
    tm = min(tm, T)
    n_steps = (T // tm) * (N // tn)
    n_blocks, src_spec, dst_spec, dst_shape = _cast_plan(cast, n_steps, lambda i, j: i * (N // tn) + j)
    return pl.pallas_call(
        functools.partial(_outproj_kernel, cast_blocks=(n_blocks, n_steps)),
        grid=(T // tm, N // tn),
        in_specs=[
            pl.BlockSpec((tm, K), lambda i, j: (i, 0)),
            pl.BlockSpec((tm, K), lambda i, j: (i, 0)),
            pl.BlockSpec((K, tn), lambda i, j: (0, j)),
            pl.BlockSpec((K, tn), lambda i, j: (1, j)),
            pl.BlockSpec((tm, tn), lambda i, j: (i, j)),
            src_spec,
        ],
        out_specs=[pl.BlockSpec((tm, tn), lambda i, j: (i, j)), dst_spec],
        out_shape=[jax.ShapeDtypeStruct((T, N), F32), dst_shape],
        compiler_params=_cparams("arbitrary", "arbitrary"),
        name="outproj",
    )(ret2d, sb2d, w_out, w_out, x2d, cast)


def _memkv_kernel(m_ref, gm_ref, w_ref, gk_ref, k_ref, v_ref):
    mn = _rms(m_ref[0], gm_ref[...]).astype(BF16)
    kv = _dot(mn, w_ref[...])
    cw = kv.shape[1] // 2
    gk = gk_ref[...]
    for hh in range(cw // HEAD_DIM):
        sl = slice(hh * HEAD_DIM, (hh + 1) * HEAD_DIM)
        k_ref[0, :, sl] = _rms(kv[:, sl], gk).astype(k_ref.dtype)
    v_ref[0] = kv[:, cw:].astype(v_ref.dtype)


def _memkv(mem, g_mem, w_kv, g_k):
    B, M, D = mem.shape
    cw = w_kv.shape[1] // 2
    return pl.pallas_call(
        _memkv_kernel,
        grid=(B,),
        in_specs=[
            pl.BlockSpec((1, M, D), lambda b: (b, 0, 0)),
            pl.BlockSpec((1, D), lambda b: (0, 0)),
            pl.BlockSpec((D, 2 * cw), lambda b: (0, 0)),
            pl.BlockSpec((1, HEAD_DIM), lambda b: (0, 0)),
        ],
        out_specs=[pl.BlockSpec((1, M, cw), lambda b: (b, 0, 0))] * 2,
        out_shape=[jax.ShapeDtypeStruct((B, M, cw), BF16)] * 2,
        compiler_params=_cparams("parallel"),
        name="memkv",
    )(mem, g_mem.reshape(1, D), w_kv, g_k.reshape(1, HEAD_DIM))


def _cross_kernel(h_ref, gc_ref, wq_ref, gq_ref, k_ref, v_ref, wo_ref, gf_ref, h2_ref, xn_ref):
    h = h_ref[...]
    xn = _rms(h, gc_ref[...]).astype(BF16)
    q = _dot(xn, wq_ref[...])
    gq = gq_ref[...]
    scale = HEAD_DIM**-0.5
    outs = []
    for hh in range(q.shape[1] // HEAD_DIM):
        sl = slice(hh * HEAD_DIM, (hh + 1) * HEAD_DIM)
        qh = _rms(q[:, sl], gq).astype(BF16)
        s = _dot_nt(qh, k_ref[0, :, sl]) * scale
        e = jnp.exp(s - jnp.max(s, axis=-1, keepdims=True))
        p = e / jnp.sum(e, axis=-1, keepdims=True)
        outs.append(_dot(p.astype(BF16), v_ref[0, :, sl]).astype(BF16))
    h2 = h + _dot(jnp.concatenate(outs, axis=-1), wo_ref[...])
    h2_ref[...] = h2
    xn_ref[...] = _rms(h2, gf_ref[...]).astype(xn_ref.dtype)


def _cross_attention(h2d, seq, g_cross, w_q, g_q, kmem, vmem, w_o, g_ffn, *, tm=256):
    T, D = h2d.shape
    _, M, cw = kmem.shape
    tm = min(tm, seq)
    tiles_per_seq = seq // tm
    const = lambda i: (0, 0)
    return pl.pallas_call(
        _cross_kernel,
        grid=(T // tm,),
        in_specs=[
            pl.BlockSpec((tm, D), lambda i: (i, 0)),
            pl.BlockSpec((1, D), const),
            pl.BlockSpec((D, cw), const),
            pl.BlockSpec((1, HEAD_DIM), const),
            pl.BlockSpec((1, M, cw), lambda i: (i // tiles_per_seq, 0, 0)),
            pl.BlockSpec((1, M, cw), lambda i: (i // tiles_per_seq, 0, 0)),
            pl.BlockSpec((cw, D), const),
            pl.BlockSpec((1, D), const),
        ],
        out_specs=[pl.BlockSpec((tm, D), lambda i: (i, 0))] * 2,
        out_shape=[jax.ShapeDtypeStruct((T, D), F32), jax.ShapeDtypeStruct((T, D), BF16)],
        compiler_params=_cparams("parallel"),
        name="cross_attention",
    )(h2d, g_cross.reshape(1, D), w_q, g_q.reshape(1, HEAD_DIM), kmem, vmem, w_o, g_ffn.reshape(1, D))


CONV_PIECE_ROWS = 64
LANES = 128


def _ffn_up_kernel(a_ref, wg_ref, wv_ref, cwg_ref, cwv_ref, cbg_ref, cbv_ref, o_ref, ubuf_ref, carry_ref,
                   *, tiles_per_seq):
    i = pl.program_id(0)
    j = pl.program_id(1)

    @pl.when(jnp.logical_and(i == 0, j == 0))
    def _():
        carry_ref[...] = jnp.zeros_like(carry_ref)

    seq_start = (i % tiles_per_seq) == 0
    halo = SUBLANES
    chunks = _row_chunks(a_ref.shape[0])
    rc = chunks[0].stop
    piece = min(CONV_PIECE_ROWS, rc)
    lane_tiles = range(o_ref.shape[1] // LANES)
    projections = ((wg_ref, cwg_ref, cbg_ref), (wv_ref, cwv_ref, cbv_ref))

    def project(c):
        par = c % 2
        a = a_ref[chunks[c], :]
        for slot, (w_ref, _, _) in enumerate(projections):
            u = _dot(a, w_ref[...])
            for lt in lane_tiles:
                if c == 0:
                    ubuf_ref[par, slot, lt, 0:halo, :] = jnp.where(seq_start, 0.0, carry_ref[j, slot, lt])
                else:
                    ubuf_ref[par, slot, lt, 0:halo, :] = ubuf_ref[1 - par, slot, lt, rc : rc + halo, :]
                ubuf_ref[par, slot, lt, halo : halo + rc, :] = u[:, lt * LANES : (lt + 1) * LANES]

    def activate(c):
        par = c % 2
        for lt in lane_tiles:
            cols = slice(lt * LANES, (lt + 1) * LANES)
            for p0 in range(0, rc, piece):

                def conv(slot):
                    _, cw_ref, cb_ref = projections[slot]
                    cw = cw_ref[:, cols]
                    c2 = cb_ref[:, cols] + ubuf_ref[par, slot, lt, pl.ds(p0 + halo - 2, piece), :] * cw[0:1, :]
                    c2 = c2 + ubuf_ref[par, slot, lt, pl.ds(p0 + halo - 1, piece), :] * cw[1:2, :]
                    return c2 + ubuf_ref[par, slot, lt, pl.ds(p0 + halo, piece), :] * cw[2:3, :]

                out_rows = slice(chunks[c].start + p0, chunks[c].start + p0 + piece)
                o_ref[out_rows, cols] = (_silu(conv(0)) * conv(1)).astype(o_ref.dtype)

    project(0)
    for c in range(1, len(chunks)):
        project(c)
        activate(c - 1)
    activate(len(chunks) - 1)
    last = (len(chunks) - 1) % 2
    for slot in range(2):
        for lt in lane_tiles:
            carry_ref[j, slot, lt] = ubuf_ref[last, slot, lt, rc : rc + halo, :]


def _ffn_up(xn, seq, w_up, conv_w, conv_b, *, tm=2048, tf=256):
    T, D = xn.shape
    F = w_up.shape[1] // 2
    tm = min(tm, seq)
    assert F % tf == 0 and seq % tm == 0
    n_j = F // tf
    kern = functools.partial(_ffn_up_kernel, tiles_per_seq=seq // tm)
    gate_col = lambda i, j: (0, j)
    value_col = lambda i, j: (0, n_j + j)
    return pl.pallas_call(
        kern,
        grid=(T // tm, n_j),
        in_specs=[
            pl.BlockSpec((tm, D), lambda i, j: (i, 0)),
            pl.BlockSpec((D, tf), gate_col),
            pl.BlockSpec((D, tf), value_col),
            pl.BlockSpec((CONV_WIDTH, tf), gate_col),
            pl.BlockSpec((CONV_WIDTH, tf), value_col),
            pl.BlockSpec((1, tf), gate_col),
            pl.BlockSpec((1, tf), value_col),
        ],
        out_specs=pl.BlockSpec((tm, tf), lambda i, j: (i, j)),
        out_shape=jax.ShapeDtypeStruct((T, F), BF16),
        scratch_shapes=[
            pltpu.VMEM((2, 2, tf // LANES, SUBLANES + min(ROW_CHUNK, tm), LANES), F32),
            pltpu.VMEM((n_j, 2, tf // LANES, SUBLANES, LANES), F32),
        ],
        compiler_params=_cparams("arbitrary", "arbitrary"),
        name="ffn_up",
    )(xn, w_up, w_up, conv_w, conv_w, conv_b, conv_b)


def _ffn_down_kernel(a_ref, w_ref, h_ref, o_ref):
    for rows in _row_chunks(o_ref.shape[0]):
        o_ref[rows, :] = h_ref[rows, :] + _dot(a_ref[rows, :], w_ref[...])


def _ffn_down(act, w_down, h2d, *, tm=512, tn=512):
    T, F = act.shape
    N = w_down.shape[1]
    tm, tn = min(tm, T), min(tn, N)
    return pl.pallas_call(
        _ffn_down_kernel,
        grid=(T // tm, N // tn),
        in_specs=[
            pl.BlockSpec((tm, F), lambda i, j: (i, 0)),
            pl.BlockSpec((F, tn), lambda i, j: (0, j)),
            pl.BlockSpec((tm, tn), lambda i, j: (i, j)),
        ],
        out_specs=pl.BlockSpec((tm, tn), lambda i, j: (i, j)),
        out_shape=jax.ShapeDtypeStruct((T, N), F32),
        compiler_params=_cparams("parallel", "arbitrary"),
        name="ffn_down",
    )(act, w_down, h2d)


def _rope_tables(S):
    inv_freq = ROPE_BASE ** (-jnp.linspace(0.0, 1.0, HEAD_DIM // 2, dtype=F32))
    ang = jnp.arange(S, dtype=F32)[:, None] * inv_freq[None, :]
    cos, sin = jnp.cos(ang), jnp.sin(ang)
    return jnp.concatenate([cos, cos], axis=-1), jnp.concatenate([-sin, sin], axis=-1)


def _retention_tables(H):
    C = RET_CHUNK
    log_g = jnp.log1p(-jnp.exp2(-5.0 - jnp.arange(H, dtype=F32)))
    idx = jnp.arange(C, dtype=F32)
    diff = idx[:, None] - idx[None, :]
    dint = jnp.where(diff >= 0, jnp.exp(jnp.maximum(diff, 0.0)[None] * log_g[:, None, None]), 0.0)
    kdec = jnp.exp((C - 1 - idx)[None, :] * log_g[:, None])
    qdec = jnp.exp((idx + 1.0)[None, :] * log_g[:, None])
    cdec = jnp.exp(C * log_g)
    lanes = lambda t: jnp.broadcast_to(t[..., None], t.shape + (HEAD_DIM,))
    return dint, lanes(kdec), lanes(qdec), lanes(cdec[:, None])


def _layer(h, mem, attn_norm, w_in, ret_norm, sb_q_norm, sb_k_norm, sb_out_norm, w_out, cross_norm, mem_norm,
           cross_w_q, cross_w_kv, cross_q_norm, cross_k_norm, cross_w_o, ffn_norm, ffn_w_up, ffn_conv_w,
           ffn_conv_b, ffn_w_down):
    B, S, D = h.shape
    T = B * S
    W = D // 2
    H = W // HEAD_DIM
    tn = min(512, W)
    seg = W // tn
    x2d = h.reshape(T, D)

    xn = _rmsnorm_rows(x2d, attn_norm, BF16)
    w_in_b = w_in.astype(BF16)
    cos2, sin2 = _rope_tables(S)
    tm_in = min(1024, S)
    tiles_per_seq = S // tm_in
    rope_spec = pl.BlockSpec((tm_in, HEAD_DIM), lambda i, j: (i % tiles_per_seq, 0))
    ret_qk = _inproj(xn, w_in_b, B, S, n_col_tiles=2 * seg, col_map=lambda j: j, mode="rope", out_dtype=F32,
                     extra=(cos2, sin2), extra_specs=(rope_spec, rope_spec), n_scaled_tiles=seg, tn=tn,
                     name="inproj_ret_qk")
    v_all, w_up_b = _inproj(xn, w_in_b, B, S, n_col_tiles=2 * seg,
                            col_map=lambda j: jnp.where(j < seg, 2 * seg + j, 5 * seg + j), mode="plain",
                            out_dtype=BF16, tn=tn, name="inproj_v", cast=ffn_w_up)
    gate, w_out_b = _inproj(xn, w_in_b, B, S, n_col_tiles=seg, col_map=lambda j: 3 * seg + j, mode="plain",
                            out_dtype=F32, tn=tn, name="inproj_gate", cast=w_out)
    sb_gains = jnp.stack([sb_q_norm, sb_k_norm]).reshape(2, 1, HEAD_DIM)
    gain_spec = pl.BlockSpec((1, 1, HEAD_DIM), lambda i, j: (j // seg, 0, 0))
    sb_qk = _inproj(xn, w_in_b, B, S, n_col_tiles=2 * seg, col_map=lambda j: 4 * seg + j, mode="norm",
                    out_dtype=BF16, extra=(sb_gains,), extra_specs=(gain_spec,), tn=tn, name="inproj_sb_qk")

    ret = _retention(ret_qk, v_all, gate, _retention_tables(H), ret_norm)
    sb = _stick_breaking(sb_qk, v_all, sb_out_norm)
    h1, w_down_b = _outproj(ret.reshape(T, W), sb.reshape(T, W), w_out_b, x2d, ffn_w_down)

    kmem, vmem = _memkv(mem, mem_norm, cross_w_kv.astype(BF16), cross_k_norm)
    h2, xn_ffn = _cross_attention(h1, S, cross_norm, cross_w_q.astype(BF16), cross_q_norm, kmem, vmem,
                                  cross_w_o.astype(BF16), ffn_norm)

    act = _ffn_up(xn_ffn, S, w_up_b, ffn_conv_w, ffn_conv_b.reshape(1, -1))
    out = _ffn_down(act, w_down_b, h2)
    return out.reshape(B, S, D)


def kernel(x, mem, attn_norm, w_in, ret_norm, sb_q_norm, sb_k_norm, sb_out_norm, w_out, cross_norm, mem_norm,
           cross_w_q, cross_w_kv, cross_q_norm, cross_k_norm, cross_w_o, ffn_norm, ffn_w_up, ffn_conv_w,
           ffn_conv_b, ffn_w_down):
    h = x
    for l in range(attn_norm.shape[0]):
        h = _layer(h, mem, attn_norm[l], w_in[l], ret_norm[l], sb_q_norm[l], sb_k_norm[l], sb_out_norm[l],
                   w_out[l], cross_norm[l], mem_norm[l], cross_w_q[l], cross_w_kv[l], cross_q_norm[l],
                   cross_k_norm[l], cross_w_o[l], ffn_norm[l], ffn_w_up[l], ffn_conv_w[l], ffn_conv_b[l],
                   ffn_w_down[l])
    return h
```

```python
import functools

import jax
import jax.numpy as jnp
from jax import lax
from jax.experimental import pallas as pl
from jax.experimental.pallas import tpu as pltpu

HEAD_DIM = 128
EPS = 1e-6
ROPE_BASE = 10000.0
RET_CHUNK = 128
N_CROSS_HEADS = 4
CONV_WIDTH = 3
SUBLANES = 8

VMEM_LIMIT_BYTES = 56 * 1024 * 1024

ROW_CHUNK = 256

SB_EXP2_UNDERFLOW = 150.0
LOG2_E = 1.4426950408889634

F32 = jnp.float32
BF16 = jnp.bfloat16


def _cparams(*sem):
    return pltpu.CompilerParams(dimension_semantics=sem, vmem_limit_bytes=VMEM_LIMIT_BYTES)


def _rms(x, g):
    return x * lax.rsqrt(jnp.mean(x * x, axis=-1, keepdims=True) + EPS) * g


def _silu(x):
    return x * (1.0 / (1.0 + jnp.exp(-x)))


def _dot(a, b):
    return jnp.dot(a, b, preferred_element_type=F32)


def _dot_nt(a, b):
    return lax.dot_general(a, b, (((1,), (1,)), ((), ())), preferred_element_type=F32)


def _row_chunks(n_rows):
    rc = min(ROW_CHUNK, n_rows)
    return [slice(r0, r0 + rc) for r0 in range(0, n_rows, rc)]


def _rmsnorm_kernel(x_ref, g_ref, o_ref):
    o_ref[...] = _rms(x_ref[...], g_ref[...]).astype(o_ref.dtype)


def _rmsnorm_rows(x2d, g, out_dtype, tm=256):
    T, D = x2d.shape
    tm = min(tm, T)
    return pl.pallas_call(
        _rmsnorm_kernel,
        grid=(T // tm,),
        in_specs=[pl.BlockSpec((tm, D), lambda i: (i, 0)), pl.BlockSpec((1, D), lambda i: (0, 0))],
        out_specs=pl.BlockSpec((tm, D), lambda i: (i, 0)),
        out_shape=jax.ShapeDtypeStruct((T, D), out_dtype),
        compiler_params=_cparams("parallel"),
        name="rmsnorm_rows",
    )(x2d, g.reshape(1, D))


def _cast_block(src_ref, dst_ref, step, n_blocks, n_steps):
    if n_blocks == n_steps:
        dst_ref[...] = src_ref[...].astype(dst_ref.dtype)
    else:

        @pl.when(step < n_blocks)
        def _():
            dst_ref[...] = src_ref[...].astype(dst_ref.dtype)


def _cast_plan(w, n_steps, step_of):
    n_rows = w.shape[0]
    packed_rows = 16
    n_blocks = max(n for n in range(1, n_steps + 1) if n_rows % n == 0 and (n_rows // n) % packed_rows == 0)
    spec = lambda: pl.BlockSpec((n_rows // n_blocks, w.shape[1]),
                                lambda *g: (jnp.minimum(step_of(*g), n_blocks - 1), 0))
    return n_blocks, spec(), spec(), jax.ShapeDtypeStruct(w.shape, BF16)


def _inproj_kernel(a_ref, w_ref, *rest, mode, heads_per_tile, n_scaled_tiles, n_extra, cast_blocks):
    j = pl.program_id(1)
    extra = rest[:n_extra]
    if cast_blocks:
        src_ref, o_ref, dst_ref = rest[n_extra:]
        _cast_block(src_ref, dst_ref, pl.program_id(0) * pl.num_programs(1) + j, cast_blocks[0], cast_blocks[1])
    else:
        (o_ref,) = rest[n_extra:]
    if mode == "rope":
        scale = jnp.where(j < n_scaled_tiles, HEAD_DIM**-0.5, 1.0).astype(F32)
    elif mode == "norm":
        gain = extra[0][0]
    for rows in _row_chunks(a_ref.shape[0]):
        acc = _dot(a_ref[rows, :], w_ref[...])
        for hh in range(heads_per_tile):
            xh = acc[:, hh * HEAD_DIM : (hh + 1) * HEAD_DIM]
            if mode == "rope":
                xh = (xh * extra[0][rows, :] + pltpu.roll(xh, HEAD_DIM // 2, axis=1) * extra[1][rows, :]) * scale
            elif mode == "norm":
                xh = _rms(xh, gain)
            o_ref[0, hh, rows, :] = xh.astype(o_ref.dtype)


def _inproj(xn, w, batch, seq, *, n_col_tiles, col_map, mode, out_dtype, extra=(), extra_specs=(),
            n_scaled_tiles=0, tm=1024, tn=1024, name, cast=None):
    T, D = xn.shape
    tm = min(tm, seq)
    hpt = tn // HEAD_DIM
    tiles_per_seq = seq // tm
    n_steps = (T // tm) * n_col_tiles
    in_specs = [
        pl.BlockSpec((tm, D), lambda i, j: (i, 0)),
        pl.BlockSpec((D, tn), lambda i, j: (0, col_map(j))),
        *extra_specs,
    ]
    out_specs = pl.BlockSpec((1, hpt, tm, HEAD_DIM), lambda i, j: (i // tiles_per_seq, j, i % tiles_per_seq, 0))
    out_shape = jax.ShapeDtypeStruct((batch, n_col_tiles * hpt, seq, HEAD_DIM), out_dtype)
    operands = [xn, w, *extra]
    cast_blocks = None
    if cast is not None:
        n_blocks, src_spec, dst_spec, dst_shape = _cast_plan(cast, n_steps, lambda i, j: i * n_col_tiles + j)
        in_specs.append(src_spec)
        out_specs, out_shape = [out_specs, dst_spec], [out_shape, dst_shape]
        operands.append(cast)
        cast_blocks = (n_blocks, n_steps)
    kern = functools.partial(_inproj_kernel, mode=mode, heads_per_tile=hpt, n_scaled_tiles=n_scaled_tiles,
                             n_extra=len(extra), cast_blocks=cast_blocks)
    return pl.pallas_call(
        kern,
        grid=(T // tm, n_col_tiles),
        in_specs=in_specs,
        out_specs=out_specs,
        out_shape=out_shape,
        compiler_params=_cparams("arbitrary", "arbitrary"),
        name=name,
    )(*operands)


def _retention_kernel(q_ref, k_ref, v_ref, g_ref, dint_ref, kdec_ref, qdec_ref, cdec_ref, gn_ref, o_ref,
                      state_ref, *, n_chunks):
    @pl.when(pl.program_id(2) == 0)
    def _():
        state_ref[...] = jnp.zeros_like(state_ref)

    dint = dint_ref[0]
    kdec = kdec_ref[0]
    qdec = qdec_ref[0]
    cdec = cdec_ref[0]
    gn = gn_ref[...]
    C = RET_CHUNK
    chunks = range(n_chunks)
    rows = [pl.ds(n * C, C) for n in chunks]
    q = [q_ref[0, 0, r, :] for r in rows]
    k = [k_ref[0, 0, r, :] for r in rows]
    scores = [_dot_nt(q[n].astype(BF16), k[n].astype(BF16)) for n in chunks]
    kv = [_dot((k[n] * kdec).T.astype(BF16), v_ref[0, 0, rows[n], :]) for n in chunks]
    states = [state_ref[...]]
    for n in chunks:
        states.append(states[n] * cdec + kv[n])
    state_ref[...] = states[n_chunks]
    intra = [_dot((scores[n] * dint).astype(BF16), v_ref[0, 0, rows[n], :]) for n in chunks]
    cross = [_dot((q[n] * qdec).astype(BF16), states[n].astype(BF16)) for n in chunks]
    for n in chunks:
        y = _rms(intra[n] + cross[n], gn) * _silu(g_ref[0, 0, rows[n], :])
        o_ref[0, rows[n], :] = y.astype(o_ref.dtype)


def _retention(qk, v_all, gate, tables, gn, *, rows_per_step=1024):
    B, H2, S, d = qk.shape
    H = H2 // 2
    tr = min(rows_per_step, S)
    dint, kdec, qdec, cdec = tables
    C = RET_CHUNK
    kern = functools.partial(_retention_kernel, n_chunks=tr // C)
    head_blk = (1, 1, tr, d)
    return pl.pallas_call(
        kern,
        grid=(B, H, S // tr),
        in_specs=[
            pl.BlockSpec(head_blk, lambda b, h, r: (b, h, r, 0)),
            pl.BlockSpec(head_blk, lambda b, h, r: (b, H + h, r, 0)),
            pl.BlockSpec(head_blk, lambda b, h, r: (b, h, r, 0)),
            pl.BlockSpec(head_blk, lambda b, h, r: (b, h, r, 0)),
            pl.BlockSpec((1, C, C), lambda b, h, r: (h, 0, 0)),
            pl.BlockSpec((1, C, d), lambda b, h, r: (h, 0, 0)),
            pl.BlockSpec((1, C, d), lambda b, h, r: (h, 0, 0)),
            pl.BlockSpec((1, 1, d), lambda b, h, r: (h, 0, 0)),
            pl.BlockSpec((1, d), lambda b, h, r: (0, 0)),
        ],
        out_specs=pl.BlockSpec((1, tr, d), lambda b, h, r: (b, r, h)),
        out_shape=jax.ShapeDtypeStruct((B, S, H * d), BF16),
        scratch_shapes=[pltpu.VMEM((d, d), F32)],
        compiler_params=_cparams("parallel", "parallel", "arbitrary"),
        name="retention",
    )(qk, qk, v_all, gate, dint, kdec, qdec, cdec, gn.reshape(1, d))


def _sb_kernel(q_ref, k_ref, v_ref, w2_ref, gn_ref, o_ref, *, tq, tk, group):
    scale = HEAD_DIM**-0.5
    w2 = w2_ref[...]
    gn = gn_ref[...]
    tk2 = 2 * tk
    qi = pl.program_id(2)
    heads = range(group)
    qbs = [q_ref[0, g] for g in heads]

    def fold(kstart, runs, accs, mask):
        keys = pl.ds(kstart, tk2)
        z2 = [_dot_nt(qbs[g], k_ref[0, g, keys, :]) * (scale * LOG2_E) for g in heads]
        hi, lo = [], []
        for g in heads:
            sp = jnp.maximum(z2[g], 0.0) + jnp.log2(1.0 + jnp.exp2(-jnp.abs(z2[g])))
            if mask is not None:
                sp = jnp.where(mask, sp, 0.0)
            h = sp.astype(BF16)
            hi.append(h)
            lo.append((sp - h.astype(F32)).astype(BF16))
        cr_new = [_dot(jnp.concatenate([hi[g][:, tk:], lo[g][:, tk:]], axis=1), w2) for g in heads]
        cr_old = [_dot(jnp.concatenate([hi[g][:, :tk], lo[g][:, :tk]], axis=1), w2) for g in heads]
        a, new_runs = [], []
        for g in heads:
            run_mid = runs[g] + cr_new[g][:, tk:]
            w = jnp.exp2(z2[g] - jnp.concatenate([cr_old[g][:, :tk] + run_mid, cr_new[g][:, :tk] + runs[g]], axis=1))
            if mask is not None:
                w = jnp.where(mask, w, 0.0)
            a.append(w.astype(BF16))
            new_runs.append(run_mid + cr_old[g][:, tk:])
        new_accs = [accs[g] + _dot(a[g], v_ref[0, g, keys, :]) for g in heads]
        return tuple(new_runs), tuple(new_accs)

    runs = tuple(jnp.zeros((tq, tk), F32) for _ in heads)
    accs = tuple(jnp.zeros((tq, HEAD_DIM), F32) for _ in heads)
    mask = lax.broadcasted_iota(jnp.int32, (tq, tk2), 1) < lax.broadcasted_iota(jnp.int32, (tq, tk2), 0)
    runs, accs = fold(pl.multiple_of(qi * tq, tq), runs, accs, mask)

    def cond(c):
        return jnp.logical_and(c[0] >= 0, jnp.min(functools.reduce(jnp.minimum, c[1])) < SB_EXP2_UNDERFLOW)

    def body(c):
        kblk, runs, accs = c
        runs, accs = fold(pl.multiple_of(kblk * tk2, tk2), runs, accs, None)
        return kblk - 1, runs, accs

    _, _, accs = lax.while_loop(cond, body, (qi - 1, runs, accs))
    for g in heads:
        o_ref[0, :, g * HEAD_DIM : (g + 1) * HEAD_DIM] = _rms(accs[g], gn).astype(o_ref.dtype)


def _stick_breaking(sqk, v_all, gn, *, tk=128, group=4):
    B, H2, S, d = sqk.shape
    H = H2 // 2
    tq = 2 * tk
    group = min(group, H)
    n_groups = H // group
    idx = jnp.arange(tk)
    w2 = jnp.concatenate([(idx[:, None] >= idx[None, :]).astype(BF16), jnp.ones((tk, tk), BF16)], axis=1)
    w2 = jnp.concatenate([w2, w2], axis=0)
    kern = functools.partial(_sb_kernel, tq=tq, tk=tk, group=group)
    kv_blk = (1, group, S, d)
    return pl.pallas_call(
        kern,
        grid=(B, n_groups, S // tq),
        in_specs=[
            pl.BlockSpec((1, group, tq, d), lambda b, h, qi: (b, h, qi, 0)),
            pl.BlockSpec(kv_blk, lambda b, h, qi: (b, n_groups + h, 0, 0)),
            pl.BlockSpec(kv_blk, lambda b, h, qi: (b, n_groups + h, 0, 0)),
            pl.BlockSpec((2 * tk, 2 * tk), lambda b, h, qi: (0, 0)),
            pl.BlockSpec((1, d), lambda b, h, qi: (0, 0)),
        ],
        out_specs=pl.BlockSpec((1, tq, group * d), lambda b, h, qi: (b, qi, h)),
        out_shape=jax.ShapeDtypeStruct((B, S, H * d), BF16),
        compiler_params=_cparams("parallel", "parallel", "arbitrary"),
        name="stick_breaking",
    )(sqk, sqk, v_all, w2, gn.reshape(1, d))


def _outproj_kernel(a1_ref, a2_ref, w1_ref, w2_ref, x_ref, o_ref):
    for rows in _row_chunks(o_ref.shape[0]):
        acc = _dot(a1_ref[rows, :], w1_ref[...]) + _dot(a2_ref[rows, :], w2_ref[...])
        o_ref[rows, :] = x_ref[rows, :] + acc


def _outproj(ret2d, sb2d, w_out, x2d, *, tm=1024, tn=1024):
    T, K = ret2d.shape
    N = w_out.shape[1]
    tm, tn = min(tm, T), min(tn, N)
    return pl.pallas_call(
        _outproj_kernel,
        grid=(T // tm, N // tn),
        in_specs=[
            pl.BlockSpec((tm, K), lambda i, j: (i, 0)),
            pl.BlockSpec((tm, K), lambda i, j: (i, 0)),
            pl.BlockSpec((K, tn), lambda i, j: (0, j)),
            pl.BlockSpec((K, tn), lambda i, j: (1, j)),
            pl.BlockSpec((tm, tn), lambda i, j: (i, j)),
        ],
        out_specs=pl.BlockSpec((tm, tn), lambda i, j: (i, j)),
        out_shape=jax.ShapeDtypeStruct((T, N), F32),
        compiler_params=_cparams("parallel", "arbitrary"),
        name="outproj",
    )(ret2d, sb2d, w_out, w_out, x2d)


def _memkv_kernel(m_ref, gm_ref, w_ref, gk_ref, k_ref, v_ref):
    mn = _rms(m_ref[0], gm_ref[...]).astype(BF16)
    kv = _dot(mn, w_ref[...])
    cw = kv.shape[1] // 2
    gk = gk_ref[...]
    for hh in range(cw // HEAD_DIM):
        sl = slice(hh * HEAD_DIM, (hh + 1) * HEAD_DIM)
        k_ref[0, :, sl] = _rms(kv[:, sl], gk).astype(k_ref.dtype)
    v_ref[0] = kv[:, cw:].astype(v_ref.dtype)


def _memkv(mem, g_mem, w_kv, g_k):
    B, M, D = mem.shape
    cw = w_kv.shape[1] // 2
    return pl.pallas_call(
        _memkv_kernel,
        grid=(B,),
        in_specs=[
            pl.BlockSpec((1, M, D), lambda b: (b, 0, 0)),
            pl.BlockSpec((1, D), lambda b: (0, 0)),
            pl.BlockSpec((D, 2 * cw), lambda b: (0, 0)),
            pl.BlockSpec((1, HEAD_DIM), lambda b: (0, 0)),
        ],
        out_specs=[pl.BlockSpec((1, M, cw), lambda b: (b, 0, 0))] * 2,
        out_shape=[jax.ShapeDtypeStruct((B, M, cw), BF16)] * 2,
        compiler_params=_cparams("parallel"),
        name="memkv",
    )(mem, g_mem.reshape(1, D), w_kv, g_k.reshape(1, HEAD_DIM))


def _cross_kernel(h_ref, gc_ref, wq_ref, gq_ref, k_ref, v_ref, wo_ref, gf_ref, h2_ref, xn_ref):
    h = h_ref[...]
    xn = _rms(h, gc_ref[...]).astype(BF16)
    q = _dot(xn, wq_ref[...])
    gq = gq_ref[...]
    scale = HEAD_DIM**-0.5
    outs = []
    for hh in range(q.shape[1] // HEAD_DIM):
        sl = slice(hh * HEAD_DIM, (hh + 1) * HEAD_DIM)
        qh = _rms(q[:, sl], gq).astype(BF16)
        s = _dot_nt(qh, k_ref[0, :, sl]) * scale
        e = jnp.exp(s - jnp.max(s, axis=-1, keepdims=True))
        p = e / jnp.sum(e, axis=-1, keepdims=True)
        outs.append(_dot(p.astype(BF16), v_ref[0, :, sl]).astype(BF16))
    h2 = h + _dot(jnp.concatenate(outs, axis=-1), wo_ref[...])
    h2_ref[...] = h2
    xn_ref[...] = _rms(h2, gf_ref[...]).astype(xn_ref.dtype)


def _cross_attention(h2d, seq, g_cross, w_q, g_q, kmem, vmem, w_o, g_ffn, *, tm=256):
    T, D = h2d.shape
    _, M, cw = kmem.shape
    tm = min(tm, seq)
    tiles_per_seq = seq // tm
    const = lambda i: (0, 0)
    return pl.pallas_call(
        _cross_kernel,
        grid=(T // tm,),
        in_specs=[
            pl.BlockSpec((tm, D), lambda i: (i, 0)),
            pl.BlockSpec((1, D), const),
            pl.BlockSpec((D, cw), const),
            pl.BlockSpec((1, HEAD_DIM), const),
            pl.BlockSpec((1, M, cw), lambda i: (i // tiles_per_seq, 0, 0)),
            pl.BlockSpec((1, M, cw), lambda i: (i // tiles_per_seq, 0, 0)),
            pl.BlockSpec((cw, D), const),
            pl.BlockSpec((1, D), const),
        ],
        out_specs=[pl.BlockSpec((tm, D), lambda i: (i, 0))] * 2,
        out_shape=[jax.ShapeDtypeStruct((T, D), F32), jax.ShapeDtypeStruct((T, D), BF16)],
        compiler_params=_cparams("parallel"),
        name="cross_attention",
    )(h2d, g_cross.reshape(1, D), w_q, g_q.reshape(1, HEAD_DIM), kmem, vmem, w_o, g_ffn.reshape(1, D))


CONV_PIECE_ROWS = 64
LANES = 128


def _ffn_up_kernel(a_ref, wg_ref, wv_ref, cwg_ref, cwv_ref, cbg_ref, cbv_ref, src_ref, o_ref, dst_ref, ubuf_ref,
                   carry_ref, *, tiles_per_seq, cast_blocks):
    i = pl.program_id(0)
    j = pl.program_id(1)

    @pl.when(jnp.logical_and(i == 0, j == 0))
    def _():
        carry_ref[...] = jnp.zeros_like(carry_ref)

    _cast_block(src_ref, dst_ref, i * pl.num_programs(1) + j, cast_blocks[0], cast_blocks[1])
    seq_start = (i % tiles_per_seq) == 0
    halo = SUBLANES
    chunks = _row_chunks(a_ref.shape[0])
    rc = chunks[0].stop
    piece = min(CONV_PIECE_ROWS, rc)
    lane_tiles = range(o_ref.shape[1] // LANES)
    projections = ((wg_ref, cwg_ref, cbg_ref), (wv_ref, cwv_ref, cbv_ref))

    def project(c):
        par = c % 2
        a = a_ref[chunks[c], :]
        for slot, (w_ref, _, _) in enumerate(projections):
            u = _dot(a, w_ref[...])
            for lt in lane_tiles:
                if c == 0:
                    ubuf_ref[par, slot, lt, 0:halo, :] = jnp.where(seq_start, 0.0, carry_ref[j, slot, lt])
                else:
                    ubuf_ref[par, slot, lt, 0:halo, :] = ubuf_ref[1 - par, slot, lt, rc : rc + halo, :]
                ubuf_ref[par, slot, lt, halo : halo + rc, :] = u[:, lt * LANES : (lt + 1) * LANES]

    def activate(c):
        par = c % 2
        for lt in lane_tiles:
            cols = slice(lt * LANES, (lt + 1) * LANES)
            for p0 in range(0, rc, piece):

                def conv(slot):
                    _, cw_ref, cb_ref = projections[slot]
                    cw = cw_ref[:, cols]
                    c2 = cb_ref[:, cols] + ubuf_ref[par, slot, lt, pl.ds(p0 + halo - 2, piece), :] * cw[0:1, :]
                    c2 = c2 + ubuf_ref[par, slot, lt, pl.ds(p0 + halo - 1, piece), :] * cw[1:2, :]
                    return c2 + ubuf_ref[par, slot, lt, pl.ds(p0 + halo, piece), :] * cw[2:3, :]

                out_rows = slice(chunks[c].start + p0, chunks[c].start + p0 + piece)
                o_ref[out_rows, cols] = (_silu(conv(0)) * conv(1)).astype(o_ref.dtype)

    project(0)
    for c in range(1, len(chunks)):
        project(c)
        activate(c - 1)
    activate(len(chunks) - 1)
    last = (len(chunks) - 1) % 2
    for slot in range(2):
        for lt in lane_tiles:
            carry_ref[j, slot, lt] = ubuf_ref[last, slot, lt, rc : rc + halo, :]


def _ffn_up(xn, seq, w_up, conv_w, conv_b, cast, *, tm=2048, tf=256):
    T, D = xn.shape
    F = w_up.shape[1] // 2
    tm = min(tm, seq)
    assert F % tf == 0 and seq % tm == 0
    n_j = F // tf
    n_steps = (T // tm) * n_j
    n_blocks, src_spec, dst_spec, dst_shape = _cast_plan(cast, n_steps, lambda i, j: i * n_j + j)
    kern = functools.partial(_ffn_up_kernel, tiles_per_seq=seq // tm, cast_blocks=(n_blocks, n_steps))
    gate_col = lambda i, j: (0, j)
    value_col = lambda i, j: (0, n_j + j)
    return pl.pallas_call(
        kern,
        grid=(T // tm, n_j),
        in_specs=[
            pl.BlockSpec((tm, D), lambda i, j: (i, 0)),
            pl.BlockSpec((D, tf), gate_col),
            pl.BlockSpec((D, tf), value_col),
            pl.BlockSpec((CONV_WIDTH, tf), gate_col),
            pl.BlockSpec((CONV_WIDTH, tf), value_col),
            pl.BlockSpec((1, tf), gate_col),
            pl.BlockSpec((1, tf), value_col),
            src_spec,
        ],
        out_specs=[pl.BlockSpec((tm, tf), lambda i, j: (i, j)), dst_spec],
        out_shape=[jax.ShapeDtypeStruct((T, F), BF16), dst_shape],
        scratch_shapes=[
            pltpu.VMEM((2, 2, tf // LANES, SUBLANES + min(ROW_CHUNK, tm), LANES), F32),
            pltpu.VMEM((n_j, 2, tf // LANES, SUBLANES, LANES), F32),
        ],
        compiler_params=_cparams("arbitrary", "arbitrary"),
        name="ffn_up",
    )(xn, w_up, w_up, conv_w, conv_w, conv_b, conv_b, cast)


def _ffn_down_kernel(a_ref, w_ref, h_ref, o_ref):
    for rows in _row_chunks(o_ref.shape[0]):
        o_ref[rows, :] = h_ref[rows, :] + _dot(a_ref[rows, :], w_ref[...])


def _ffn_down(act, w_down, h2d, *, tm=512, tn=512):
    T, F = act.shape
    N = w_down.shape[1]
    tm, tn = min(tm, T), min(tn, N)
    return pl.pallas_call(
        _ffn_down_kernel,
        grid=(T // tm, N // tn),
        in_specs=[
            pl.BlockSpec((tm, F), lambda i, j: (i, 0)),
            pl.BlockSpec((F, tn), lambda i, j: (0, j)),
            pl.BlockSpec((tm, tn), lambda i, j: (i, j)),
        ],
        out_specs=pl.BlockSpec((tm, tn), lambda i, j: (i, j)),
        out_shape=jax.ShapeDtypeStruct((T, N), F32),
        compiler_params=_cparams("parallel", "arbitrary"),
        name="ffn_down",
    )(act, w_down, h2d)


def _rope_tables(S):
    inv_freq = ROPE_BASE ** (-jnp.linspace(0.0, 1.0, HEAD_DIM // 2, dtype=F32))
    ang = jnp.arange(S, dtype=F32)[:, None] * inv_freq[None, :]
    cos, sin = jnp.cos(ang), jnp.sin(ang)
    return jnp.concatenate([cos, cos], axis=-1), jnp.concatenate([-sin, sin], axis=-1)


def _retention_tables(H):
    C = RET_CHUNK
    log_g = jnp.log1p(-jnp.exp2(-5.0 - jnp.arange(H, dtype=F32)))
    idx = jnp.arange(C, dtype=F32)
    diff = idx[:, None] - idx[None, :]
    dint = jnp.where(diff >= 0, jnp.exp(jnp.maximum(diff, 0.0)[None] * log_g[:, None, None]), 0.0)
    kdec = jnp.exp((C - 1 - idx)[None, :] * log_g[:, None])
    qdec = jnp.exp((idx + 1.0)[None, :] * log_g[:, None])
    cdec = jnp.exp(C * log_g)
    lanes = lambda t: jnp.broadcast_to(t[..., None], t.shape + (HEAD_DIM,))
    return dint, lanes(kdec), lanes(qdec), lanes(cdec[:, None])


def _layer(h, mem, attn_norm, w_in, ret_norm, sb_q_norm, sb_k_norm, sb_out_norm, w_out, cross_norm, mem_norm,
           cross_w_q, cross_w_kv, cross_q_norm, cross_k_norm, cross_w_o, ffn_norm, ffn_w_up, ffn_conv_w,
           ffn_conv_b, ffn_w_down):
    B, S, D = h.shape
    T = B * S
    W = D // 2
    H = W // HEAD_DIM
    tn = min(1024, W)
    seg = W // tn
    x2d = h.reshape(T, D)

    xn = _rmsnorm_rows(x2d, attn_norm, BF16)
    w_in_b = w_in.astype(BF16)
    cos2, sin2 = _rope_tables(S)
    tm_in = min(1024, S)
    tiles_per_seq = S // tm_in
    rope_spec = pl.BlockSpec((tm_in, HEAD_DIM), lambda i, j: (i % tiles_per_seq, 0))
    ret_qk = _inproj(xn, w_in_b, B, S, n_col_tiles=2 * seg, col_map=lambda j: j, mode="rope", out_dtype=F32,
                     extra=(cos2, sin2), extra_specs=(rope_spec, rope_spec), n_scaled_tiles=seg, tn=tn,
                     name="inproj_ret_qk")
    v_all, w_up_b = _inproj(xn, w_in_b, B, S, n_col_tiles=2 * seg,
                            col_map=lambda j: jnp.where(j < seg, 2 * seg + j, 5 * seg + j), mode="plain",
                            out_dtype=BF16, tn=tn, name="inproj_v", cast=ffn_w_up)
    gate, w_out_b = _inproj(xn, w_in_b, B, S, n_col_tiles=seg, col_map=lambda j: 3 * seg + j, mode="plain",
                            out_dtype=F32, tn=tn, name="inproj_gate", cast=w_out)
    sb_gains = jnp.stack([sb_q_norm, sb_k_norm]).reshape(2, 1, HEAD_DIM)
    gain_spec = pl.BlockSpec((1, 1, HEAD_DIM), lambda i, j: (j // seg, 0, 0))
    sb_qk = _inproj(xn, w_in_b, B, S, n_col_tiles=2 * seg, col_map=lambda j: 4 * seg + j, mode="norm",
                    out_dtype=BF16, extra=(sb_gains,), extra_specs=(gain_spec,), tn=tn, name="inproj_sb_qk")

    ret = _retention(ret_qk, v_all, gate, _retention_tables(H), ret_norm)
    sb = _stick_breaking(sb_qk, v_all, sb_out_norm)
    h1 = _outproj(ret.reshape(T, W), sb.reshape(T, W), w_out_b, x2d)

    kmem, vmem = _memkv(mem, mem_norm, cross_w_kv.astype(BF16), cross_k_norm)
    h2, xn_ffn = _cross_attention(h1, S, cross_norm, cross_w_q.astype(BF16), cross_q_norm, kmem, vmem,
                                  cross_w_o.astype(BF16), ffn_norm)

    act, w_down_b = _ffn_up(xn_ffn, S, w_up_b, ffn_conv_w, ffn_conv_b.reshape(1, -1), ffn_w_down)
    out = _ffn_down(act, w_down_b, h2)
    return out.reshape(B, S, D)


def kernel(x, mem, attn_norm, w_in, ret_norm, sb_q_norm, sb_k_norm, sb_out_norm, w_out, cross_norm, mem_norm,
           cross_w_q, cross_w_kv, cross_q_norm, cross_k_norm, cross_w_o, ffn_norm, ffn_w_up, ffn_conv_w,
           ffn_conv_b, ffn_w_down):
    h = x
    for l in range(attn_norm.shape[0]):
        h = _layer(h, mem, attn_norm[l], w_in[l], ret_norm[l], sb_q_norm[l], sb_k_norm[l], sb_out_norm[l],
                   w_out[l], cross_norm[l], mem_norm[l], cross_w_q[l], cross_w_kv[l], cross_q_norm[l],
                   cross_k_norm[l], cross_w_o[l], ffn_norm[l], ffn_w_up[l], ffn_conv_w[l], ffn_conv_b[l],
                   ffn_w_down[l])
    return h
```

```python
import functools

import jax
import jax.numpy as jnp
from jax import lax
from jax.experimental import pallas as pl
from jax.experimental.pallas import tpu as pltpu

HEAD_DIM = 128
EPS = 1e-6
ROPE_BASE = 10000.0
RET_CHUNK = 128
N_CROSS_HEADS = 4
CONV_WIDTH = 3
SUBLANES = 8
LANES = 128

VMEM_LIMIT_BYTES = 56 * 1024 * 1024

ROW_CHUNK = 256

SB_EXP2_UNDERFLOW = 150.0
LOG2_E = 1.4426950408889634

F32 = jnp.float32
BF16 = jnp.bfloat16


def _cparams(*sem):
    return pltpu.CompilerParams(dimension_semantics=sem, vmem_limit_bytes=VMEM_LIMIT_BYTES)


def _rms(x, g):
    return x * lax.rsqrt(jnp.mean(x * x, axis=-1, keepdims=True) + EPS) * g


def _silu(x):
    return x * (1.0 / (1.0 + jnp.exp(-x)))


def _dot(a, b):
    return jnp.dot(a, b, preferred_element_type=F32)


def _dot_nt(a, b):
    return lax.dot_general(a, b, (((1,), (1,)), ((), ())), preferred_element_type=F32)


def _row_chunks(n_rows):
    rc = min(ROW_CHUNK, n_rows)
    return [slice(r0, r0 + rc) for r0 in range(0, n_rows, rc)]


def _rmsnorm_kernel(x_ref, g_ref, o_ref):
    o_ref[...] = _rms(x_ref[...], g_ref[...]).astype(o_ref.dtype)


def _rmsnorm_rows(x2d, g, out_dtype, tm=256):
    T, D = x2d.shape
    tm = min(tm, T)
    return pl.pallas_call(
        _rmsnorm_kernel,
        grid=(T // tm,),
        in_specs=[pl.BlockSpec((tm, D), lambda i: (i, 0)), pl.BlockSpec((1, D), lambda i: (0, 0))],
        out_specs=pl.BlockSpec((tm, D), lambda i: (i, 0)),
        out_shape=jax.ShapeDtypeStruct((T, D), out_dtype),
        compiler_params=_cparams("parallel"),
        name="rmsnorm_rows",
    )(x2d, g.reshape(1, D))


def _cast_block(src_ref, dst_ref, step, n_blocks, n_steps):
    if n_blocks == n_steps:
        dst_ref[...] = src_ref[...].astype(dst_ref.dtype)
    else:

        @pl.when(step < n_blocks)
        def _():
            dst_ref[...] = src_ref[...].astype(dst_ref.dtype)


def _cast_plan(w, n_steps, step_of):
    n_rows = w.shape[0]
    packed_rows = 16
    n_blocks = max(n for n in range(1, n_steps + 1) if n_rows % n == 0 and (n_rows // n) % packed_rows == 0)
    spec = lambda: pl.BlockSpec((n_rows // n_blocks, w.shape[1]),
                                lambda *g: (jnp.minimum(step_of(*g), n_blocks - 1), 0))
    return n_blocks, spec(), spec(), jax.ShapeDtypeStruct(w.shape, BF16)


def _inproj_kernel(a_ref, w_ref, *rest, mode, heads_per_tile, n_scaled_tiles, n_extra, cast_blocks):
    j = pl.program_id(1)
    extra = rest[:n_extra]
    if cast_blocks:
        src_ref, o_ref, dst_ref = rest[n_extra:]
        _cast_block(src_ref, dst_ref, pl.program_id(0) * pl.num_programs(1) + j, cast_blocks[0], cast_blocks[1])
    else:
        (o_ref,) = rest[n_extra:]
    if mode == "rope":
        scale = jnp.where(j < n_scaled_tiles, HEAD_DIM**-0.5, 1.0).astype(F32)
    elif mode == "norm":
        gain = extra[0][0]
    for rows in _row_chunks(a_ref.shape[0]):
        acc = _dot(a_ref[rows, :], w_ref[...])
        for hh in range(heads_per_tile):
            xh = acc[:, hh * HEAD_DIM : (hh + 1) * HEAD_DIM]
            if mode == "rope":
                xh = (xh * extra[0][rows, :] + pltpu.roll(xh, HEAD_DIM // 2, axis=1) * extra[1][rows, :]) * scale
            elif mode == "norm":
                xh = _rms(xh, gain)
            o_ref[0, hh, rows, :] = xh.astype(o_ref.dtype)


def _inproj(xn, w, batch, seq, *, n_col_tiles, col_map, mode, out_dtype, extra=(), extra_specs=(),
            n_scaled_tiles=0, tm=1024, tn=1024, name, cast=None):
    T, D = xn.shape
    tm = min(tm, seq)
    hpt = tn // HEAD_DIM
    tiles_per_seq = seq // tm
    n_steps = (T // tm) * n_col_tiles
    in_specs = [
        pl.BlockSpec((tm, D), lambda i, j: (i, 0)),
        pl.BlockSpec((D, tn), lambda i, j: (0, col_map(j))),
        *extra_specs,
    ]
    out_specs = pl.BlockSpec((1, hpt, tm, HEAD_DIM), lambda i, j: (i // tiles_per_seq, j, i % tiles_per_seq, 0))
    out_shape = jax.ShapeDtypeStruct((batch, n_col_tiles * hpt, seq, HEAD_DIM), out_dtype)
    operands = [xn, w, *extra]
    cast_blocks = None
    if cast is not None:
        n_blocks, src_spec, dst_spec, dst_shape = _cast_plan(cast, n_steps, lambda i, j: i * n_col_tiles + j)
        in_specs.append(src_spec)
        out_specs, out_shape = [out_specs, dst_spec], [out_shape, dst_shape]
        operands.append(cast)
        cast_blocks = (n_blocks, n_steps)
    kern = functools.partial(_inproj_kernel, mode=mode, heads_per_tile=hpt, n_scaled_tiles=n_scaled_tiles,
                             n_extra=len(extra), cast_blocks=cast_blocks)
    return pl.pallas_call(
        kern,
        grid=(T // tm, n_col_tiles),
        in_specs=in_specs,
        out_specs=out_specs,
        out_shape=out_shape,
        compiler_params=_cparams("arbitrary", "arbitrary"),
        name=name,
    )(*operands)


def _retention_kernel(q_ref, k_ref, v_ref, g_ref, dint_ref, kdec_ref, qdec_ref, cdec_ref, gn_ref, o_ref,
                      state_ref, *, n_chunks):
    @pl.when(pl.program_id(2) == 0)
    def _():
        state_ref[...] = jnp.zeros_like(state_ref)

    dint = dint_ref[0]
    kdec = kdec_ref[0]
    qdec = qdec_ref[0]
    cdec = cdec_ref[0]
    gn = gn_ref[...]
    C = RET_CHUNK
    chunks = range(n_chunks)
    rows = [pl.ds(n * C, C) for n in chunks]
    q = [q_ref[0, 0, r, :] for r in rows]
    k = [k_ref[0, 0, r, :] for r in rows]
    scores = [_dot_nt(q[n].astype(BF16), k[n].astype(BF16)) for n in chunks]
    kv = [_dot((k[n] * kdec).T.astype(BF16), v_ref[0, 0, rows[n], :]) for n in chunks]
    states = [state_ref[...]]
    for n in chunks:
        states.append(states[n] * cdec + kv[n])
    state_ref[...] = states[n_chunks]
    out = [
        _dot(
            jnp.concatenate([(scores[n] * dint).astype(BF16), (q[n] * qdec).astype(BF16)], axis=1),
            jnp.concatenate([v_ref[0, 0, rows[n], :], states[n].astype(BF16)], axis=0),
        )
        for n in chunks
    ]
    for n in chunks:
        y = _rms(out[n], gn) * _silu(g_ref[0, 0, rows[n], :])
        o_ref[0, rows[n], :] = y.astype(o_ref.dtype)


def _retention(qk, v_all, gate, tables, gn, *, rows_per_step=1024):
    B, H2, S, d = qk.shape
    H = H2 // 2
    tr = min(rows_per_step, S)
    dint, kdec, qdec, cdec = tables
    C = RET_CHUNK
    kern = functools.partial(_retention_kernel, n_chunks=tr // C)
    head_blk = (1, 1, tr, d)
    return pl.pallas_call(
        kern,
        grid=(B, H, S // tr),
        in_specs=[
            pl.BlockSpec(head_blk, lambda b, h, r: (b, h, r, 0)),
            pl.BlockSpec(head_blk, lambda b, h, r: (b, H + h, r, 0)),
            pl.BlockSpec(head_blk, lambda b, h, r: (b, h, r, 0)),
            pl.BlockSpec(head_blk, lambda b, h, r: (b, h, r, 0)),
            pl.BlockSpec((1, C, C), lambda b, h, r: (h, 0, 0)),
            pl.BlockSpec((1, C, d), lambda b, h, r: (h, 0, 0)),
            pl.BlockSpec((1, C, d), lambda b, h, r: (h, 0, 0)),
            pl.BlockSpec((1, 1, d), lambda b, h, r: (h, 0, 0)),
            pl.BlockSpec((1, d), lambda b, h, r: (0, 0)),
        ],
        out_specs=pl.BlockSpec((1, tr, d), lambda b, h, r: (b, r, h)),
        out_shape=jax.ShapeDtypeStruct((B, S, H * d), BF16),
        scratch_shapes=[pltpu.VMEM((d, d), F32)],
        compiler_params=_cparams("parallel", "parallel", "arbitrary"),
        name="retention",
    )(qk, qk, v_all, gate, dint, kdec, qdec, cdec, gn.reshape(1, d))


def _sb_kernel(q_ref, k_ref, v_ref, w2_ref, gn_ref, o_ref, *, tq, tk, group):
    scale = HEAD_DIM**-0.5
    w2 = w2_ref[...]
    gn = gn_ref[...]
    qi = pl.program_id(2)
    heads = range(group)

    def guard(unit, x):
        _, _, _, _, mask, valid = unit
        if mask is not None:
            head = [] if x.shape[1] == tk else [x[:, : x.shape[1] - tk]]
            x = jnp.concatenate(head + [jnp.where(mask, x[:, x.shape[1] - tk :], 0.0)], axis=1)
        if valid is not None:
            x = jnp.where(valid, x, 0.0)
        return x

    def key_rows(unit):
        return pl.ds(unit[2], unit[3] * tk)

    def scan_keys(units):
        newest = lambda x: x[:, x.shape[1] - tk :]
        z2 = [_dot_nt(u[1], k_ref[0, u[0], key_rows(u), :]) * (scale * LOG2_E) for u in units]
        hi, lo = [], []
        for u, z in zip(units, z2):
            sp = guard(u, jnp.maximum(z, 0.0) + jnp.log2(1.0 + jnp.exp2(-jnp.abs(z))))
            h = sp.astype(BF16)
            hi.append(h)
            lo.append((sp - h.astype(F32)).astype(BF16))
        cr_new = [_dot(jnp.concatenate([newest(h), newest(l)], axis=1), w2) for h, l in zip(hi, lo)]
        cr_old = [_dot(jnp.concatenate([h[:, :tk], l[:, :tk]], axis=1), w2) if u[3] == 2 else None
                  for u, h, l in zip(units, hi, lo)]
        return list(zip(z2, cr_new, cr_old))

    def weigh(units, scans, runs, accs):
        a, new_runs = [], []
        for u, (z2, cr_new, cr_old), run in zip(units, scans, runs):
            run_mid = run + cr_new[:, tk:]
            if cr_old is not None:
                behind = jnp.concatenate([cr_old[:, :tk] + run_mid, cr_new[:, :tk] + run], axis=1)
                new_runs.append(run_mid + cr_old[:, tk:])
            else:
                behind = cr_new[:, :tk] + run
                new_runs.append(run_mid)
            a.append(guard(u, jnp.exp2(z2 - behind)).astype(BF16))
        new_accs = [acc + _dot(w, v_ref[0, u[0], key_rows(u), :]) for u, w, acc in zip(units, a, accs)]
        return new_runs, new_accs

    q0 = pl.multiple_of(qi * tq, tq)
    mask = lax.broadcasted_iota(jnp.int32, (tk, tk), 1) < lax.broadcasted_iota(jnp.int32, (tk, tk), 0)
    diag = [(g, q_ref[0, g, half * tk : (half + 1) * tk, :], q0, half + 1, mask, None)
            for g in heads for half in range(2)]
    prev_start = pl.multiple_of(jnp.maximum(qi - 1, 0) * tq, tq)
    prev = [(g, q_ref[0, g], prev_start, 2, None, qi > 0) for g in heads]
    scans = scan_keys(diag + prev)
    zero_run, zero_acc = jnp.zeros((tk, tk), F32), jnp.zeros((tk, HEAD_DIM), F32)
    runs, accs = weigh(diag, scans[: len(diag)], [zero_run] * len(diag), [zero_acc] * len(diag))
    runs = [jnp.concatenate([runs[2 * g], runs[2 * g + 1]], axis=0) for g in heads]
    accs = [jnp.concatenate([accs[2 * g], accs[2 * g + 1]], axis=0) for g in heads]
    runs, accs = weigh(prev, scans[len(diag) :], runs, accs)

    def cond(c):
        return jnp.logical_and(c[0] >= 0, jnp.min(functools.reduce(jnp.minimum, c[1])) < SB_EXP2_UNDERFLOW)

    def body(c):
        kblk, runs, accs = c
        kstart = pl.multiple_of(kblk * tq, tq)
        units = [(g, q_ref[0, g], kstart, 2, None, None) for g in heads]
        runs, accs = weigh(units, scan_keys(units), runs, accs)
        return kblk - 1, tuple(runs), tuple(accs)

    _, _, accs = lax.while_loop(cond, body, (qi - 2, tuple(runs), tuple(accs)))
    for g in heads:
        o_ref[0, :, g * HEAD_DIM : (g + 1) * HEAD_DIM] = _rms(accs[g], gn).astype(o_ref.dtype)


def _stick_breaking(sqk, v_all, gn, *, tk=128, group=4):
    B, H2, S, d = sqk.shape
    H = H2 // 2
    tq = 2 * tk
    group = min(group, H)
    n_groups = H // group
    idx = jnp.arange(tk)
    w2 = jnp.concatenate([(idx[:, None] >= idx[None, :]).astype(BF16), jnp.ones((tk, tk), BF16)], axis=1)
    w2 = jnp.concatenate([w2, w2], axis=0)
    kern = functools.partial(_sb_kernel, tq=tq, tk=tk, group=group)
    kv_spec = lambda: pl.BlockSpec((1, group, S, d), lambda b, h, qi: (b, n_groups + h, 0, 0))
    return pl.pallas_call(
        kern,
        grid=(B, n_groups, S // tq),
        in_specs=[
            pl.BlockSpec((1, group, tq, d), lambda b, h, qi: (b, h, qi, 0)),
            kv_spec(),
            kv_spec(),
            pl.BlockSpec((2 * tk, 2 * tk), lambda b, h, qi: (0, 0)),
            pl.BlockSpec((1, d), lambda b, h, qi: (0, 0)),
        ],
        out_specs=pl.BlockSpec((1, tq, group * d), lambda b, h, qi: (b, qi, h)),
        out_shape=jax.ShapeDtypeStruct((B, S, H * d), BF16),
        compiler_params=_cparams("parallel", "parallel", "arbitrary"),
        name="stick_breaking",
    )(sqk, sqk, v_all, w2, gn.reshape(1, d))


def _outproj_kernel(a1_ref, a2_ref, w1_ref, w2_ref, x_ref, o_ref):
    for rows in _row_chunks(o_ref.shape[0]):
        acc = _dot(a1_ref[rows, :], w1_ref[...]) + _dot(a2_ref[rows, :], w2_ref[...])
        o_ref[rows, :] = x_ref[rows, :] + acc


def _outproj(ret2d, sb2d, w_out, x2d, *, tm=1024, tn=1024):
    T, K = ret2d.shape
    N = w_out.shape[1]
    tm, tn = min(tm, T), min(tn, N)
    return pl.pallas_call(
        _outproj_kernel,
        grid=(T // tm, N // tn),
        in_specs=[
            pl.BlockSpec((tm, K), lambda i, j: (i, 0)),
            pl.BlockSpec((tm, K), lambda i, j: (i, 0)),
            pl.BlockSpec((K, tn), lambda i, j: (0, j)),
            pl.BlockSpec((K, tn), lambda i, j: (1, j)),
            pl.BlockSpec((tm, tn), lambda i, j: (i, j)),
        ],
        out_specs=pl.BlockSpec((tm, tn), lambda i, j: (i, j)),
        out_shape=jax.ShapeDtypeStruct((T, N), F32),
        compiler_params=_cparams("parallel", "arbitrary"),
        name="outproj",
    )(ret2d, sb2d, w_out, w_out, x2d)


def _memkv_kernel(m_ref, gm_ref, w_ref, gk_ref, k_ref, v_ref):
    mn = _rms(m_ref[0], gm_ref[...]).astype(BF16)
    kv = _dot(mn, w_ref[...])
    cw = kv.shape[1] // 2
    gk = gk_ref[...]
    for hh in range(cw // HEAD_DIM):
        sl = slice(hh * HEAD_DIM, (hh + 1) * HEAD_DIM)
        k_ref[0, :, sl] = _rms(kv[:, sl], gk).astype(k_ref.dtype)
    v_ref[0] = kv[:, cw:].astype(v_ref.dtype)


def _memkv(mem, g_mem, w_kv, g_k):
    B, M, D = mem.shape
    cw = w_kv.shape[1] // 2
    return pl.pallas_call(
        _memkv_kernel,
        grid=(B,),
        in_specs=[
            pl.BlockSpec((1, M, D), lambda b: (b, 0, 0)),
            pl.BlockSpec((1, D), lambda b: (0, 0)),
            pl.BlockSpec((D, 2 * cw), lambda b: (0, 0)),
            pl.BlockSpec((1, HEAD_DIM), lambda b: (0, 0)),
        ],
        out_specs=[pl.BlockSpec((1, M, cw), lambda b: (b, 0, 0))] * 2,
        out_shape=[jax.ShapeDtypeStruct((B, M, cw), BF16)] * 2,
        compiler_params=_cparams("parallel"),
        name="memkv",
    )(mem, g_mem.reshape(1, D), w_kv, g_k.reshape(1, HEAD_DIM))


def _cross_kernel(h_ref, gc_ref, wq_ref, gq_ref, k_ref, v_ref, wo_ref, gf_ref, h2_ref, xn_ref):
    n_rows = h_ref.shape[0]
    part_rows = min(LANES, n_rows)
    parts = [slice(r0, r0 + part_rows) for r0 in range(0, n_rows, part_rows)]
    gq = gq_ref[...]
    scale = HEAD_DIM**-0.5
    h = [h_ref[rows, :] for rows in parts]
    xn = [_rms(hp, gc_ref[...]).astype(BF16) for hp in h]
    q = [_dot(x, wq_ref[...]) for x in xn]
    head_cols = [slice(c0, c0 + HEAD_DIM) for c0 in range(0, wq_ref.shape[1], HEAD_DIM)]
    s = [[_dot_nt(_rms(qp[:, sl], gq).astype(BF16), k_ref[0, :, sl]) * scale for sl in head_cols] for qp in q]
    o = []
    for sp in s:
        heads_out = []
        for sh, sl in zip(sp, head_cols):
            e = jnp.exp(sh - jnp.max(sh, axis=-1, keepdims=True))
            p = e / jnp.sum(e, axis=-1, keepdims=True)
            heads_out.append(_dot(p.astype(BF16), v_ref[0, :, sl]).astype(BF16))
        o.append(jnp.concatenate(heads_out, axis=-1))
    h2 = [hp + _dot(op, wo_ref[...]) for hp, op in zip(h, o)]
    for rows, h2p in zip(parts, h2):
        h2_ref[rows, :] = h2p
        xn_ref[rows, :] = _rms(h2p, gf_ref[...]).astype(xn_ref.dtype)


def _cross_attention(h2d, seq, g_cross, w_q, g_q, kmem, vmem, w_o, g_ffn, *, tm=256):
    T, D = h2d.shape
    _, M, cw = kmem.shape
    tm = min(tm, seq)
    tiles_per_seq = seq // tm
    const = lambda i: (0, 0)
    return pl.pallas_call(
        _cross_kernel,
        grid=(T // tm,),
        in_specs=[
            pl.BlockSpec((tm, D), lambda i: (i, 0)),
            pl.BlockSpec((1, D), const),
            pl.BlockSpec((D, cw), const),
            pl.BlockSpec((1, HEAD_DIM), const),
            pl.BlockSpec((1, M, cw), lambda i: (i // tiles_per_seq, 0, 0)),
            pl.BlockSpec((1, M, cw), lambda i: (i // tiles_per_seq, 0, 0)),
            pl.BlockSpec((cw, D), const),
            pl.BlockSpec((1, D), const),
        ],
        out_specs=[pl.BlockSpec((tm, D), lambda i: (i, 0))] * 2,
        out_shape=[jax.ShapeDtypeStruct((T, D), F32), jax.ShapeDtypeStruct((T, D), BF16)],
        compiler_params=_cparams("parallel"),
        name="cross_attention",
    )(h2d, g_cross.reshape(1, D), w_q, g_q.reshape(1, HEAD_DIM), kmem, vmem, w_o, g_ffn.reshape(1, D))


CONV_PIECE_ROWS = 64


def _ffn_up_kernel(a_ref, wg_ref, wv_ref, cwg_ref, cwv_ref, cbg_ref, cbv_ref, src_ref, o_ref, dst_ref, ubuf_ref,
                   carry_ref, *, tiles_per_seq, cast_blocks):
    i = pl.program_id(0)
    j = pl.program_id(1)

    @pl.when(jnp.logical_and(i == 0, j == 0))
    def _():
        carry_ref[...] = jnp.zeros_like(carry_ref)

    _cast_block(src_ref, dst_ref, i * pl.num_programs(1) + j, cast_blocks[0], cast_blocks[1])
    seq_start = (i % tiles_per_seq) == 0
    halo = SUBLANES
    chunks = _row_chunks(a_ref.shape[0])
    size = lambda c: chunks[c].stop - chunks[c].start
    lane_tiles = range(o_ref.shape[1] // LANES)
    projections = ((wg_ref, cwg_ref, cbg_ref), (wv_ref, cwv_ref, cbv_ref))

    def project(c):
        par = c % 2
        a = a_ref[chunks[c], :]
        for slot, (w_ref, _, _) in enumerate(projections):
            u = _dot(a, w_ref[...])
            for lt in lane_tiles:
                if c == 0:
                    ubuf_ref[par, slot, lt, 0:halo, :] = jnp.where(seq_start, 0.0, carry_ref[j, slot, lt])
                else:
                    ubuf_ref[par, slot, lt, 0:halo, :] = ubuf_ref[1 - par, slot, lt, size(c - 1) : size(c - 1) + halo, :]
                ubuf_ref[par, slot, lt, halo : halo + size(c), :] = u[:, lt * LANES : (lt + 1) * LANES]

    def activate(c):
        par = c % 2
        piece = min(CONV_PIECE_ROWS, size(c))
        for lt in lane_tiles:
            cols = slice(lt * LANES, (lt + 1) * LANES)
            for p0 in range(0, size(c), piece):

                def conv(slot):
                    _, cw_ref, cb_ref = projections[slot]
                    cw = cw_ref[:, cols]
                    c2 = cb_ref[:, cols] + ubuf_ref[par, slot, lt, pl.ds(p0 + halo - 2, piece), :] * cw[0:1, :]
                    c2 = c2 + ubuf_ref[par, slot, lt, pl.ds(p0 + halo - 1, piece), :] * cw[1:2, :]
                    return c2 + ubuf_ref[par, slot, lt, pl.ds(p0 + halo, piece), :] * cw[2:3, :]

                out_rows = slice(chunks[c].start + p0, chunks[c].start + p0 + piece)
                o_ref[out_rows, cols] = (_silu(conv(0)) * conv(1)).astype(o_ref.dtype)

    project(0)
    for c in range(1, len(chunks)):
        project(c)
        activate(c - 1)
    activate(len(chunks) - 1)
    last = len(chunks) - 1
    for slot in range(2):
        for lt in lane_tiles:
            carry_ref[j, slot, lt] = ubuf_ref[last % 2, slot, lt, size(last) : size(last) + halo, :]


def _ffn_up(xn, seq, w_up, conv_w, conv_b, cast, *, tm=2048, tf=256):
    T, D = xn.shape
    F = w_up.shape[1] // 2
    tm = min(tm, seq)
    assert F % tf == 0 and seq % tm == 0
    n_j = F // tf
    n_steps = (T // tm) * n_j
    n_blocks, src_spec, dst_spec, dst_shape = _cast_plan(cast, n_steps, lambda i, j: i * n_j + j)
    kern = functools.partial(_ffn_up_kernel, tiles_per_seq=seq // tm, cast_blocks=(n_blocks, n_steps))
    gate_col = lambda i, j: (0, j)
    value_col = lambda i, j: (0, n_j + j)
    return pl.pallas_call(
        kern,
        grid=(T // tm, n_j),
        in_specs=[
            pl.BlockSpec((tm, D), lambda i, j: (i, 0)),
            pl.BlockSpec((D, tf), gate_col),
            pl.BlockSpec((D, tf), value_col),
            pl.BlockSpec((CONV_WIDTH, tf), gate_col),
            pl.BlockSpec((CONV_WIDTH, tf), value_col),
            pl.BlockSpec((1, tf), gate_col),
            pl.BlockSpec((1, tf), value_col),
            src_spec,
        ],
        out_specs=[pl.BlockSpec((tm, tf), lambda i, j: (i, j)), dst_spec],
        out_shape=[jax.ShapeDtypeStruct((T, F), BF16), dst_shape],
        scratch_shapes=[
            pltpu.VMEM((2, 2, tf // LANES, SUBLANES + min(ROW_CHUNK, tm), LANES), F32),
            pltpu.VMEM((n_j, 2, tf // LANES, SUBLANES, LANES), F32),
        ],
        compiler_params=_cparams("arbitrary", "arbitrary"),
        name="ffn_up",
    )(xn, w_up, w_up, conv_w, conv_w, conv_b, conv_b, cast)


def _ffn_down_kernel(a_ref, w_ref, h_ref, o_ref):
    for rows in _row_chunks(o_ref.shape[0]):
        o_ref[rows, :] = h_ref[rows, :] + _dot(a_ref[rows, :], w_ref[...])


def _ffn_down(act, w_down, h2d, *, tm=512, tn=512):
    T, F = act.shape
    N = w_down.shape[1]
    tm, tn = min(tm, T), min(tn, N)
    return pl.pallas_call(
        _ffn_down_kernel,
        grid=(T // tm, N // tn),
        in_specs=[
            pl.BlockSpec((tm, F), lambda i, j: (i, 0)),
            pl.BlockSpec((F, tn), lambda i, j: (0, j)),
            pl.BlockSpec((tm, tn), lambda i, j: (i, j)),
        ],
        out_specs=pl.BlockSpec((tm, tn), lambda i, j: (i, j)),
        out_shape=jax.ShapeDtypeStruct((T, N), F32),
        compiler_params=_cparams("parallel", "arbitrary"),
        name="ffn_down",
    )(act, w_down, h2d)


def _rope_tables(S):
    inv_freq = ROPE_BASE ** (-jnp.linspace(0.0, 1.0, HEAD_DIM // 2, dtype=F32))
    ang = jnp.arange(S, dtype=F32)[:, None] * inv_freq[None, :]
    cos, sin = jnp.cos(ang), jnp.sin(ang)
    return jnp.concatenate([cos, cos], axis=-1), jnp.concatenate([-sin, sin], axis=-1)


def _retention_tables(H):
    C = RET_CHUNK
    log_g = jnp.log1p(-jnp.exp2(-5.0 - jnp.arange(H, dtype=F32)))
    idx = jnp.arange(C, dtype=F32)
    diff = idx[:, None] - idx[None, :]
    dint = jnp.where(diff >= 0, jnp.exp(jnp.maximum(diff, 0.0)[None] * log_g[:, None, None]), 0.0)
    kdec = jnp.exp((C - 1 - idx)[None, :] * log_g[:, None])
    qdec = jnp.exp((idx + 1.0)[None, :] * log_g[:, None])
    cdec = jnp.exp(C * log_g)
    lanes = lambda t: jnp.broadcast_to(t[..., None], t.shape + (HEAD_DIM,))
    return dint, lanes(kdec), lanes(qdec), lanes(cdec[:, None])


def _layer(h, mem, attn_norm, w_in, ret_norm, sb_q_norm, sb_k_norm, sb_out_norm, w_out, cross_norm, mem_norm,
           cross_w_q, cross_w_kv, cross_q_norm, cross_k_norm, cross_w_o, ffn_norm, ffn_w_up, ffn_conv_w,
           ffn_conv_b, ffn_w_down):
    B, S, D = h.shape
    T = B * S
    W = D // 2
    H = W // HEAD_DIM
    tn = min(1024, W)
    seg = W // tn
    x2d = h.reshape(T, D)

    xn = _rmsnorm_rows(x2d, attn_norm, BF16)
    w_in_b = w_in.astype(BF16)
    cos2, sin2 = _rope_tables(S)
    tm_in = min(1024, S)
    tiles_per_seq = S // tm_in
    rope_spec = pl.BlockSpec((tm_in, HEAD_DIM), lambda i, j: (i % tiles_per_seq, 0))
    ret_qk = _inproj(xn, w_in_b, B, S, n_col_tiles=2 * seg, col_map=lambda j: j, mode="rope", out_dtype=F32,
                     extra=(cos2, sin2), extra_specs=(rope_spec, rope_spec), n_scaled_tiles=seg, tn=tn,
                     name="inproj_ret_qk")
    v_all, w_up_b = _inproj(xn, w_in_b, B, S, n_col_tiles=2 * seg,
                            col_map=lambda j: jnp.where(j < seg, 2 * seg + j, 5 * seg + j), mode="plain",
                            out_dtype=BF16, tn=tn, name="inproj_v", cast=ffn_w_up)
    gate, w_out_b = _inproj(xn, w_in_b, B, S, n_col_tiles=seg, col_map=lambda j: 3 * seg + j, mode="plain",
                            out_dtype=F32, tn=tn, name="inproj_gate", cast=w_out)
    sb_gains = jnp.stack([sb_q_norm, sb_k_norm]).reshape(2, 1, HEAD_DIM)
    gain_spec = pl.BlockSpec((1, 1, HEAD_DIM), lambda i, j: (j // seg, 0, 0))
    sb_qk = _inproj(xn, w_in_b, B, S, n_col_tiles=2 * seg, col_map=lambda j: 4 * seg + j, mode="norm",
                    out_dtype=BF16, extra=(sb_gains,), extra_specs=(gain_spec,), tn=tn, name="inproj_sb_qk")

    ret = _retention(ret_qk, v_all, gate, _retention_tables(H), ret_norm)
    sb = _stick_breaking(sb_qk, v_all, sb_out_norm)
    h1 = _outproj(ret.reshape(T, W), sb.reshape(T, W), w_out_b, x2d)

    kmem, vmem = _memkv(mem, mem_norm, cross_w_kv.astype(BF16), cross_k_norm)
    h2, xn_ffn = _cross_attention(h1, S, cross_norm, cross_w_q.astype(BF16), cross_q_norm, kmem, vmem,
                                  cross_w_o.astype(BF16), ffn_norm)

    act, w_down_b = _ffn_up(xn_ffn, S, w_up_b, ffn_conv_w, ffn_conv_b.reshape(1, -1), ffn_w_down)
    out = _ffn_down(act, w_down_b, h2)
    return out.reshape(B, S, D)


def kernel(x, mem, attn_norm, w_in, ret_norm, sb_q_norm, sb_k_norm, sb_out_norm, w_out, cross_norm, mem_norm,
           cross_w_q, cross_w_kv, cross_q_norm, cross_k_norm, cross_w_o, ffn_norm, ffn_w_up, ffn_conv_w,
           ffn_conv_b, ffn_w_down):
    h = x
    for l in range(attn_norm.shape[0]):
        h = _layer(h, mem, attn_norm[l], w_in[l], ret_norm[l], sb_q_norm[l], sb_k_norm[l], sb_out_norm[l],
                   w_out[l], cross_norm[l], mem_norm[l], cross_w_q[l], cross_w_kv[l], cross_q_norm[l],
                   cross_k_norm[l], cross_w_o[l], ffn_norm[l], ffn_w_up[l], ffn_conv_w[l], ffn_conv_b[l],
                   ffn_w_down[l])
    return h
```

```python
import functools

import jax
import jax.numpy as jnp
from jax import lax
from jax.experimental import pallas as pl
from jax.experimental.pallas import tpu as pltpu

HEAD_DIM = 128
EPS = 1e-6
ROPE_BASE = 10000.0
RET_CHUNK = 128
N_CROSS_HEADS = 4
CONV_WIDTH = 3
SUBLANES = 8
LANES = 128

VMEM_LIMIT_BYTES = 56 * 1024 * 1024

ROW_CHUNK = 256

SB_EXP2_UNDERFLOW = 150.0
LOG2_E = 1.4426950408889634

F32 = jnp.float32
BF16 = jnp.bfloat16


def _cparams(*sem):
    return pltpu.CompilerParams(dimension_semantics=sem, vmem_limit_bytes=VMEM_LIMIT_BYTES)


def _rms(x, g):
    return x * lax.rsqrt(jnp.mean(x * x, axis=-1, keepdims=True) + EPS) * g


def _silu(x):
    return x * (1.0 / (1.0 + jnp.exp(-x)))


def _dot(a, b):
    return jnp.dot(a, b, preferred_element_type=F32)


def _dot_nt(a, b):
    return lax.dot_general(a, b, (((1,), (1,)), ((), ())), preferred_element_type=F32)


def _row_chunks(n_rows):
    rc = min(ROW_CHUNK, n_rows)
    return [slice(r0, r0 + rc) for r0 in range(0, n_rows, rc)]


def _rmsnorm_kernel(x_ref, g_ref, o_ref):
    o_ref[...] = _rms(x_ref[...], g_ref[...]).astype(o_ref.dtype)


def _rmsnorm_rows(x2d, g, out_dtype, tm=256):
    T, D = x2d.shape
    tm = min(tm, T)
    return pl.pallas_call(
        _rmsnorm_kernel,
        grid=(T // tm,),
        in_specs=[pl.BlockSpec((tm, D), lambda i: (i, 0)), pl.BlockSpec((1, D), lambda i: (0, 0))],
        out_specs=pl.BlockSpec((tm, D), lambda i: (i, 0)),
        out_shape=jax.ShapeDtypeStruct((T, D), out_dtype),
        compiler_params=_cparams("parallel"),
        name="rmsnorm_rows",
    )(x2d, g.reshape(1, D))


def _cast_block(src_ref, dst_ref, step, n_blocks, n_steps):
    if n_blocks == n_steps:
        dst_ref[...] = src_ref[...].astype(dst_ref.dtype)
    else:

        @pl.when(step < n_blocks)
        def _():
            dst_ref[...] = src_ref[...].astype(dst_ref.dtype)


def _cast_plan(w, n_steps, step_of):
    n_rows = w.shape[0]
    packed_rows = 16
    n_blocks = max(n for n in range(1, n_steps + 1) if n_rows % n == 0 and (n_rows // n) % packed_rows == 0)
    spec = lambda: pl.BlockSpec((n_rows // n_blocks, w.shape[1]),
                                lambda *g: (jnp.minimum(step_of(*g), n_blocks - 1), 0))
    return n_blocks, spec(), spec(), jax.ShapeDtypeStruct(w.shape, BF16)


def _inproj_kernel(a_ref, w_ref, *rest, mode, heads_per_tile, n_scaled_tiles, n_extra, cast_blocks):
    j = pl.program_id(1)
    extra = rest[:n_extra]
    if cast_blocks:
        src_ref, o_ref, dst_ref = rest[n_extra:]
        _cast_block(src_ref, dst_ref, pl.program_id(0) * pl.num_programs(1) + j, cast_blocks[0], cast_blocks[1])
    else:
        (o_ref,) = rest[n_extra:]
    if mode == "rope":
        scale = jnp.where(j < n_scaled_tiles, HEAD_DIM**-0.5, 1.0).astype(F32)
    elif mode == "norm":
        gain = extra[0][0]
    for rows in _row_chunks(a_ref.shape[0]):
        acc = _dot(a_ref[rows, :], w_ref[...])
        for hh in range(heads_per_tile):
            xh = acc[:, hh * HEAD_DIM : (hh + 1) * HEAD_DIM]
            if mode == "rope":
                xh = (xh * extra[0][rows, :] + pltpu.roll(xh, HEAD_DIM // 2, axis=1) * extra[1][rows, :]) * scale
            elif mode == "norm":
                xh = _rms(xh, gain)
            o_ref[0, hh, rows, :] = xh.astype(o_ref.dtype)


def _inproj(xn, w, batch, seq, *, n_col_tiles, col_map, mode, out_dtype, extra=(), extra_specs=(),
            n_scaled_tiles=0, tm=1024, tn=1024, name, cast=None):
    T, D = xn.shape
    tm = min(tm, seq)
    hpt = tn // HEAD_DIM
    tiles_per_seq = seq // tm
    n_steps = (T // tm) * n_col_tiles
    in_specs = [
        pl.BlockSpec((tm, D), lambda i, j: (i, 0)),
        pl.BlockSpec((D, tn), lambda i, j: (0, col_map(j))),
        *extra_specs,
    ]
    out_specs = pl.BlockSpec((1, hpt, tm, HEAD_DIM), lambda i, j: (i // tiles_per_seq, j, i % tiles_per_seq, 0))
    out_shape = jax.ShapeDtypeStruct((batch, n_col_tiles * hpt, seq, HEAD_DIM), out_dtype)
    operands = [xn, w, *extra]
    cast_blocks = None
    if cast is not None:
        n_blocks, src_spec, dst_spec, dst_shape = _cast_plan(cast, n_steps, lambda i, j: i * n_col_tiles + j)
        in_specs.append(src_spec)
        out_specs, out_shape = [out_specs, dst_spec], [out_shape, dst_shape]
        operands.append(cast)
        cast_blocks = (n_blocks, n_steps)
    kern = functools.partial(_inproj_kernel, mode=mode, heads_per_tile=hpt, n_scaled_tiles=n_scaled_tiles,
                             n_extra=len(extra), cast_blocks=cast_blocks)
    return pl.pallas_call(
        kern,
        grid=(T // tm, n_col_tiles),
        in_specs=in_specs,
        out_specs=out_specs,
        out_shape=out_shape,
        compiler_params=_cparams("arbitrary", "arbitrary"),
        name=name,
    )(*operands)


def _retention_kernel(q_ref, k_ref, v_ref, g_ref, dint_ref, kdec_ref, qdec_ref, cdec_ref, gn_ref, o_ref,
                      state_ref, *, n_chunks):
    @pl.when(pl.program_id(2) == 0)
    def _():
        state_ref[...] = jnp.zeros_like(state_ref)

    dint = dint_ref[0]
    kdec = kdec_ref[0]
    qdec = qdec_ref[0]
    cdec = cdec_ref[0]
    gn = gn_ref[...]
    C = RET_CHUNK
    chunks = range(n_chunks)
    rows = [pl.ds(n * C, C) for n in chunks]
    q = [q_ref[0, 0, r, :] for r in rows]
    k = [k_ref[0, 0, r, :] for r in rows]
    scores = [_dot_nt(q[n].astype(BF16), k[n].astype(BF16)) for n in chunks]
    kv = [_dot((k[n] * kdec).T.astype(BF16), v_ref[0, 0, rows[n], :]) for n in chunks]
    states = [state_ref[...]]
    for n in chunks:
        states.append(states[n] * cdec + kv[n])
    state_ref[...] = states[n_chunks]
    out = [
        _dot(
            jnp.concatenate([(scores[n] * dint).astype(BF16), (q[n] * qdec).astype(BF16)], axis=1),
            jnp.concatenate([v_ref[0, 0, rows[n], :], states[n].astype(BF16)], axis=0),
        )
        for n in chunks
    ]
    for n in chunks:
        y = _rms(out[n], gn) * _silu(g_ref[0, 0, rows[n], :])
        o_ref[0, rows[n], :] = y.astype(o_ref.dtype)


def _retention(qk, v_all, gate, tables, gn, *, rows_per_step=2048):
    B, H2, S, d = qk.shape
    H = H2 // 2
    tr = min(rows_per_step, S)
    dint, kdec, qdec, cdec = tables
    C = RET_CHUNK
    kern = functools.partial(_retention_kernel, n_chunks=tr // C)
    head_blk = (1, 1, tr, d)
    return pl.pallas_call(
        kern,
        grid=(B, H, S // tr),
        in_specs=[
            pl.BlockSpec(head_blk, lambda b, h, r: (b, h, r, 0)),
            pl.BlockSpec(head_blk, lambda b, h, r: (b, H + h, r, 0)),
            pl.BlockSpec(head_blk, lambda b, h, r: (b, h, r, 0)),
            pl.BlockSpec(head_blk, lambda b, h, r: (b, h, r, 0)),
            pl.BlockSpec((1, C, C), lambda b, h, r: (h, 0, 0)),
            pl.BlockSpec((1, C, d), lambda b, h, r: (h, 0, 0)),
            pl.BlockSpec((1, C, d), lambda b, h, r: (h, 0, 0)),
            pl.BlockSpec((1, 1, d), lambda b, h, r: (h, 0, 0)),
            pl.BlockSpec((1, d), lambda b, h, r: (0, 0)),
        ],
        out_specs=pl.BlockSpec((1, tr, d), lambda b, h, r: (b, r, h)),
        out_shape=jax.ShapeDtypeStruct((B, S, H * d), BF16),
        scratch_shapes=[pltpu.VMEM((d, d), F32)],
        compiler_params=_cparams("parallel", "parallel", "arbitrary"),
        name="retention",
    )(qk, qk, v_all, gate, dint, kdec, qdec, cdec, gn.reshape(1, d))


def _sb_kernel(q_ref, k_ref, v_ref, w2_ref, gn_ref, o_ref, *, tq, tk, group):
    scale = HEAD_DIM**-0.5
    w2 = w2_ref[...]
    gn = gn_ref[...]
    qi = pl.program_id(2)
    heads = range(group)

    def guard(unit, x):
        _, _, _, _, mask, valid = unit
        if mask is not None:
            head = [] if x.shape[1] == tk else [x[:, : x.shape[1] - tk]]
            x = jnp.concatenate(head + [jnp.where(mask, x[:, x.shape[1] - tk :], 0.0)], axis=1)
        if valid is not None:
            x = jnp.where(valid, x, 0.0)
        return x

    def key_rows(unit):
        return pl.ds(unit[2], unit[3] * tk)

    def scan_keys(units):
        newest = lambda x: x[:, x.shape[1] - tk :]
        z2 = [_dot_nt(u[1], k_ref[0, u[0], key_rows(u), :]) * (scale * LOG2_E) for u in units]
        hi, lo = [], []
        for u, z in zip(units, z2):
            sp = guard(u, jnp.maximum(z, 0.0) + jnp.log2(1.0 + jnp.exp2(-jnp.abs(z))))
            h = sp.astype(BF16)
            hi.append(h)
            lo.append((sp - h.astype(F32)).astype(BF16))
        cr_new = [_dot(jnp.concatenate([newest(h), newest(l)], axis=1), w2) for h, l in zip(hi, lo)]
        cr_old = [_dot(jnp.concatenate([h[:, :tk], l[:, :tk]], axis=1), w2) if u[3] == 2 else None
                  for u, h, l in zip(units, hi, lo)]
        return list(zip(z2, cr_new, cr_old))

    def weigh(units, scans, runs, accs):
        a, new_runs = [], []
        for u, (z2, cr_new, cr_old), run in zip(units, scans, runs):
            run_mid = run + cr_new[:, tk:]
            if cr_old is not None:
                behind = jnp.concatenate([cr_old[:, :tk] + run_mid, cr_new[:, :tk] + run], axis=1)
                new_runs.append(run_mid + cr_old[:, tk:])
            else:
                behind = cr_new[:, :tk] + run
                new_runs.append(run_mid)
            a.append(guard(u, jnp.exp2(z2 - behind)).astype(BF16))
        new_accs = [acc + _dot(w, v_ref[0, u[0], key_rows(u), :]) for u, w, acc in zip(units, a, accs)]
        return new_runs, new_accs

    q0 = pl.multiple_of(qi * tq, tq)
    mask = lax.broadcasted_iota(jnp.int32, (tk, tk), 1) < lax.broadcasted_iota(jnp.int32, (tk, tk), 0)
    diag = [(g, q_ref[0, g, half * tk : (half + 1) * tk, :], q0, half + 1, mask, None)
            for g in heads for half in range(2)]
    prev_start = pl.multiple_of(jnp.maximum(qi - 1, 0) * tq, tq)
    prev = [(g, q_ref[0, g], prev_start, 2, None, qi > 0) for g in heads]
    scans = scan_keys(diag + prev)
    zero_run, zero_acc = jnp.zeros((tk, tk), F32), jnp.zeros((tk, HEAD_DIM), F32)
    runs, accs = weigh(diag, scans[: len(diag)], [zero_run] * len(diag), [zero_acc] * len(diag))
    runs = [jnp.concatenate([runs[2 * g], runs[2 * g + 1]], axis=0) for g in heads]
    accs = [jnp.concatenate([accs[2 * g], accs[2 * g + 1]], axis=0) for g in heads]
    runs, accs = weigh(prev, scans[len(diag) :], runs, accs)

    def cond(c):
        return jnp.logical_and(c[0] >= 0, jnp.min(functools.reduce(jnp.minimum, c[1])) < SB_EXP2_UNDERFLOW)

    def body(c):
        kblk, runs, accs = c
        kstart = pl.multiple_of(kblk * tq, tq)
        units = [(g, q_ref[0, g], kstart, 2, None, None) for g in heads]
        runs, accs = weigh(units, scan_keys(units), runs, accs)
        return kblk - 1, tuple(runs), tuple(accs)

    _, _, accs = lax.while_loop(cond, body, (qi - 2, tuple(runs), tuple(accs)))
    for g in heads:
        o_ref[0, :, g * HEAD_DIM : (g + 1) * HEAD_DIM] = _rms(accs[g], gn).astype(o_ref.dtype)


def _stick_breaking(sqk, v_all, gn, *, tk=128, group=4):
    B, H2, S, d = sqk.shape
    H = H2 // 2
    tq = 2 * tk
    group = min(group, H)
    n_groups = H // group
    idx = jnp.arange(tk)
    w2 = jnp.concatenate([(idx[:, None] >= idx[None, :]).astype(BF16), jnp.ones((tk, tk), BF16)], axis=1)
    w2 = jnp.concatenate([w2, w2], axis=0)
    kern = functools.partial(_sb_kernel, tq=tq, tk=tk, group=group)
    kv_spec = lambda: pl.BlockSpec((1, group, S, d), lambda b, h, qi: (b, n_groups + h, 0, 0))
    return pl.pallas_call(
        kern,
        grid=(B, n_groups, S // tq),
        in_specs=[
            pl.BlockSpec((1, group, tq, d), lambda b, h, qi: (b, h, qi, 0)),
            kv_spec(),
            kv_spec(),
            pl.BlockSpec((2 * tk, 2 * tk), lambda b, h, qi: (0, 0)),
            pl.BlockSpec((1, d), lambda b, h, qi: (0, 0)),
        ],
        out_specs=pl.BlockSpec((1, tq, group * d), lambda b, h, qi: (b, qi, h)),
        out_shape=jax.ShapeDtypeStruct((B, S, H * d), BF16),
        compiler_params=_cparams("parallel", "parallel", "arbitrary"),
        name="stick_breaking",
    )(sqk, sqk, v_all, w2, gn.reshape(1, d))


def _outproj_kernel(a1_ref, a2_ref, w1_ref, w2_ref, x_ref, o_ref):
    for rows in _row_chunks(o_ref.shape[0]):
        acc = _dot(a1_ref[rows, :], w1_ref[...]) + _dot(a2_ref[rows, :], w2_ref[...])
        o_ref[rows, :] = x_ref[rows, :] + acc


def _outproj(ret2d, sb2d, w_out, x2d, *, tm=1024, tn=1024):
    T, K = ret2d.shape
    N = w_out.shape[1]
    tm, tn = min(tm, T), min(tn, N)
    return pl.pallas_call(
        _outproj_kernel,
        grid=(T // tm, N // tn),
        in_specs=[
            pl.BlockSpec((tm, K), lambda i, j: (i, 0)),
            pl.BlockSpec((tm, K), lambda i, j: (i, 0)),
            pl.BlockSpec((K, tn), lambda i, j: (0, j)),
            pl.BlockSpec((K, tn), lambda i, j: (1, j)),
            pl.BlockSpec((tm, tn), lambda i, j: (i, j)),
        ],
        out_specs=pl.BlockSpec((tm, tn), lambda i, j: (i, j)),
        out_shape=jax.ShapeDtypeStruct((T, N), F32),
        compiler_params=_cparams("parallel", "arbitrary"),
        name="outproj",
    )(ret2d, sb2d, w_out, w_out, x2d)


def _memkv_kernel(m_ref, gm_ref, w_ref, gk_ref, k_ref, v_ref):
    mn = _rms(m_ref[0], gm_ref[...]).astype(BF16)
    kv = _dot(mn, w_ref[...])
    cw = kv.shape[1] // 2
    gk = gk_ref[...]
    for hh in range(cw // HEAD_DIM):
        sl = slice(hh * HEAD_DIM, (hh + 1) * HEAD_DIM)
        k_ref[0, :, sl] = _rms(kv[:, sl], gk).astype(k_ref.dtype)
    v_ref[0] = kv[:, cw:].astype(v_ref.dtype)


def _memkv(mem, g_mem, w_kv, g_k):
    B, M, D = mem.shape
    cw = w_kv.shape[1] // 2
    return pl.pallas_call(
        _memkv_kernel,
        grid=(B,),
        in_specs=[
            pl.BlockSpec((1, M, D), lambda b: (b, 0, 0)),
            pl.BlockSpec((1, D), lambda b: (0, 0)),
            pl.BlockSpec((D, 2 * cw), lambda b: (0, 0)),
            pl.BlockSpec((1, HEAD_DIM), lambda b: (0, 0)),
        ],
        out_specs=[pl.BlockSpec((1, M, cw), lambda b: (b, 0, 0))] * 2,
        out_shape=[jax.ShapeDtypeStruct((B, M, cw), BF16)] * 2,
        compiler_params=_cparams("parallel"),
        name="memkv",
    )(mem, g_mem.reshape(1, D), w_kv, g_k.reshape(1, HEAD_DIM))


def _cross_kernel(h_ref, gc_ref, wq_ref, gq_ref, k_ref, v_ref, wo_ref, gf_ref, h2_ref, xn_ref):
    n_rows = h_ref.shape[0]
    part_rows = min(LANES, n_rows)
    parts = [slice(r0, r0 + part_rows) for r0 in range(0, n_rows, part_rows)]
    gq = gq_ref[...]
    scale = HEAD_DIM**-0.5
    h = [h_ref[rows, :] for rows in parts]
    xn = [_rms(hp, gc_ref[...]).astype(BF16) for hp in h]
    q = [_dot(x, wq_ref[...]) for x in xn]
    head_cols = [slice(c0, c0 + HEAD_DIM) for c0 in range(0, wq_ref.shape[1], HEAD_DIM)]
    s = [[_dot_nt(_rms(qp[:, sl], gq).astype(BF16), k_ref[0, :, sl]) * scale for sl in head_cols] for qp in q]
    o = []
    for sp in s:
        heads_out = []
        for sh, sl in zip(sp, head_cols):
            e = jnp.exp(sh - jnp.max(sh, axis=-1, keepdims=True))
            p = e / jnp.sum(e, axis=-1, keepdims=True)
            heads_out.append(_dot(p.astype(BF16), v_ref[0, :, sl]).astype(BF16))
        o.append(jnp.concatenate(heads_out, axis=-1))
    h2 = [hp + _dot(op, wo_ref[...]) for hp, op in zip(h, o)]
    for rows, h2p in zip(parts, h2):
        h2_ref[rows, :] = h2p
        xn_ref[rows, :] = _rms(h2p, gf_ref[...]).astype(xn_ref.dtype)


def _cross_attention(h2d, seq, g_cross, w_q, g_q, kmem, vmem, w_o, g_ffn, *, tm=256):
    T, D = h2d.shape
    _, M, cw = kmem.shape
    tm = min(tm, seq)
    tiles_per_seq = seq // tm
    const = lambda i: (0, 0)
    return pl.pallas_call(
        _cross_kernel,
        grid=(T // tm,),
        in_specs=[
            pl.BlockSpec((tm, D), lambda i: (i, 0)),
            pl.BlockSpec((1, D), const),
            pl.BlockSpec((D, cw), const),
            pl.BlockSpec((1, HEAD_DIM), const),
            pl.BlockSpec((1, M, cw), lambda i: (i // tiles_per_seq, 0, 0)),
            pl.BlockSpec((1, M, cw), lambda i: (i // tiles_per_seq, 0, 0)),
            pl.BlockSpec((cw, D), const),
            pl.BlockSpec((1, D), const),
        ],
        out_specs=[pl.BlockSpec((tm, D), lambda i: (i, 0))] * 2,
        out_shape=[jax.ShapeDtypeStruct((T, D), F32), jax.ShapeDtypeStruct((T, D), BF16)],
        compiler_params=_cparams("parallel"),
        name="cross_attention",
    )(h2d, g_cross.reshape(1, D), w_q, g_q.reshape(1, HEAD_DIM), kmem, vmem, w_o, g_ffn.reshape(1, D))


CONV_PIECE_ROWS = 64


def _ffn_up_kernel(a_ref, wg_ref, wv_ref, cwg_ref, cwv_ref, cbg_ref, cbv_ref, src_ref, o_ref, dst_ref, ubuf_ref,
                   carry_ref, *, tiles_per_seq, cast_blocks):
    i = pl.program_id(0)
    j = pl.program_id(1)

    @pl.when(jnp.logical_and(i == 0, j == 0))
    def _():
        carry_ref[...] = jnp.zeros_like(carry_ref)

    _cast_block(src_ref, dst_ref, i * pl.num_programs(1) + j, cast_blocks[0], cast_blocks[1])
    seq_start = (i % tiles_per_seq) == 0
    halo = SUBLANES
    chunks = _row_chunks(a_ref.shape[0])
    size = lambda c: chunks[c].stop - chunks[c].start
    lane_tiles = range(o_ref.shape[1] // LANES)
    projections = ((wg_ref, cwg_ref, cbg_ref), (wv_ref, cwv_ref, cbv_ref))

    def project(c):
        par = c % 2
        a = a_ref[chunks[c], :]
        for slot, (w_ref, _, _) in enumerate(projections):
            u = _dot(a, w_ref[...])
            for lt in lane_tiles:
                if c == 0:
                    ubuf_ref[par, slot, lt, 0:halo, :] = jnp.where(seq_start, 0.0, carry_ref[j, slot, lt])
                else:
                    ubuf_ref[par, slot, lt, 0:halo, :] = ubuf_ref[1 - par, slot, lt, size(c - 1) : size(c - 1) + halo, :]
                ubuf_ref[par, slot, lt, halo : halo + size(c), :] = u[:, lt * LANES : (lt + 1) * LANES]

    def activate(c):
        par = c % 2
        piece = min(CONV_PIECE_ROWS, size(c))
        for lt in lane_tiles:
            cols = slice(lt * LANES, (lt + 1) * LANES)
            for p0 in range(0, size(c), piece):

                def conv(slot):
                    _, cw_ref, cb_ref = projections[slot]
                    cw = cw_ref[:, cols]
                    c2 = cb_ref[:, cols] + ubuf_ref[par, slot, lt, pl.ds(p0 + halo - 2, piece), :] * cw[0:1, :]
                    c2 = c2 + ubuf_ref[par, slot, lt, pl.ds(p0 + halo - 1, piece), :] * cw[1:2, :]
                    return c2 + ubuf_ref[par, slot, lt, pl.ds(p0 + halo, piece), :] * cw[2:3, :]

                out_rows = slice(chunks[c].start + p0, chunks[c].start + p0 + piece)
                o_ref[out_rows, cols] = (_silu(conv(0)) * conv(1)).astype(o_ref.dtype)

    project(0)
    for c in range(1, len(chunks)):
        project(c)
        activate(c - 1)
    activate(len(chunks) - 1)
    last = len(chunks) - 1
    for slot in range(2):
        for lt in lane_tiles:
            carry_ref[j, slot, lt] = ubuf_ref[last % 2, slot, lt, size(last) : size(last) + halo, :]


def _ffn_up(xn, seq, w_up, conv_w, conv_b, cast, *, tm=2048, tf=256):
    T, D = xn.shape
    F = w_up.shape[1] // 2
    tm = min(tm, seq)
    assert F % tf == 0 and seq % tm == 0
    n_j = F // tf
    n_steps = (T // tm) * n_j
    n_blocks, src_spec, dst_spec, dst_shape = _cast_plan(cast, n_steps, lambda i, j: i * n_j + j)
    kern = functools.partial(_ffn_up_kernel, tiles_per_seq=seq // tm, cast_blocks=(n_blocks, n_steps))
    gate_col = lambda i, j: (0, j)
    value_col = lambda i, j: (0, n_j + j)
    return pl.pallas_call(
        kern,
        grid=(T // tm, n_j),
        in_specs=[
            pl.BlockSpec((tm, D), lambda i, j: (i, 0)),
            pl.BlockSpec((D, tf), gate_col),
            pl.BlockSpec((D, tf), value_col),
            pl.BlockSpec((CONV_WIDTH, tf), gate_col),
            pl.BlockSpec((CONV_WIDTH, tf), value_col),
            pl.BlockSpec((1, tf), gate_col),
            pl.BlockSpec((1, tf), value_col),
            src_spec,
        ],
        out_specs=[pl.BlockSpec((tm, tf), lambda i, j: (i, j)), dst_spec],
        out_shape=[jax.ShapeDtypeStruct((T, F), BF16), dst_shape],
        scratch_shapes=[
            pltpu.VMEM((2, 2, tf // LANES, SUBLANES + min(ROW_CHUNK, tm), LANES), F32),
            pltpu.VMEM((n_j, 2, tf // LANES, SUBLANES, LANES), F32),
        ],
        compiler_params=_cparams("arbitrary", "arbitrary"),
        name="ffn_up",
    )(xn, w_up, w_up, conv_w, conv_w, conv_b, conv_b, cast)


def _ffn_down_kernel(a_ref, w_ref, h_ref, o_ref):
    for rows in _row_chunks(o_ref.shape[0]):
        o_ref[rows, :] = h_ref[rows, :] + _dot(a_ref[rows, :], w_ref[...])


def _ffn_down(act, w_down, h2d, *, tm=512, tn=512):
    T, F = act.shape
    N = w_down.shape[1]
    tm, tn = min(tm, T), min(tn, N)
    return pl.pallas_call(
        _ffn_down_kernel,
        grid=(N // tn, T // tm),
        in_specs=[
            pl.BlockSpec((tm, F), lambda j, i: (i, 0)),
            pl.BlockSpec((F, tn), lambda j, i: (0, j)),
            pl.BlockSpec((tm, tn), lambda j, i: (i, j)),
        ],
        out_specs=pl.BlockSpec((tm, tn), lambda j, i: (i, j)),
        out_shape=jax.ShapeDtypeStruct((T, N), F32),
        compiler_params=_cparams("parallel", "arbitrary"),
        name="ffn_down",
    )(act, w_down, h2d)


def _rope_tables(S):
    inv_freq = ROPE_BASE ** (-jnp.linspace(0.0, 1.0, HEAD_DIM // 2, dtype=F32))
    ang = jnp.arange(S, dtype=F32)[:, None] * inv_freq[None, :]
    cos, sin = jnp.cos(ang), jnp.sin(ang)
    return jnp.concatenate([cos, cos], axis=-1), jnp.concatenate([-sin, sin], axis=-1)


def _retention_tables(H):
    C = RET_CHUNK
    log_g = jnp.log1p(-jnp.exp2(-5.0 - jnp.arange(H, dtype=F32)))
    idx = jnp.arange(C, dtype=F32)
    diff = idx[:, None] - idx[None, :]
    dint = jnp.where(diff >= 0, jnp.exp(jnp.maximum(diff, 0.0)[None] * log_g[:, None, None]), 0.0)
    kdec = jnp.exp((C - 1 - idx)[None, :] * log_g[:, None])
    qdec = jnp.exp((idx + 1.0)[None, :] * log_g[:, None])
    cdec = jnp.exp(C * log_g)
    lanes = lambda t: jnp.broadcast_to(t[..., None], t.shape + (HEAD_DIM,))
    return dint, lanes(kdec), lanes(qdec), lanes(cdec[:, None])


def _layer(h, mem, attn_norm, w_in, ret_norm, sb_q_norm, sb_k_norm, sb_out_norm, w_out, cross_norm, mem_norm,
           cross_w_q, cross_w_kv, cross_q_norm, cross_k_norm, cross_w_o, ffn_norm, ffn_w_up, ffn_conv_w,
           ffn_conv_b, ffn_w_down):
    B, S, D = h.shape
    T = B * S
    W = D // 2
    H = W // HEAD_DIM
    tn = min(1024, W)
    seg = W // tn
    x2d = h.reshape(T, D)

    xn = _rmsnorm_rows(x2d, attn_norm, BF16)
    w_in_b = w_in.astype(BF16)
    cos2, sin2 = _rope_tables(S)
    tm_in = min(1024, S)
    tiles_per_seq = S // tm_in
    rope_spec = pl.BlockSpec((tm_in, HEAD_DIM), lambda i, j: (i % tiles_per_seq, 0))
    ret_qk = _inproj(xn, w_in_b, B, S, n_col_tiles=2 * seg, col_map=lambda j: j, mode="rope", out_dtype=F32,
                     extra=(cos2, sin2), extra_specs=(rope_spec, rope_spec), n_scaled_tiles=seg, tn=tn,
                     name="inproj_ret_qk")
    v_all, w_up_b = _inproj(xn, w_in_b, B, S, n_col_tiles=2 * seg,
                            col_map=lambda j: jnp.where(j < seg, 2 * seg + j, 5 * seg + j), mode="plain",
                            out_dtype=BF16, tn=tn, name="inproj_v", cast=ffn_w_up)
    gate, w_out_b = _inproj(xn, w_in_b, B, S, n_col_tiles=seg, col_map=lambda j: 3 * seg + j, mode="plain",
                            out_dtype=F32, tn=tn, name="inproj_gate", cast=w_out)
    sb_gains = jnp.stack([sb_q_norm, sb_k_norm]).reshape(2, 1, HEAD_DIM)
    gain_spec = pl.BlockSpec((1, 1, HEAD_DIM), lambda i, j: (j // seg, 0, 0))
    sb_qk = _inproj(xn, w_in_b, B, S, n_col_tiles=2 * seg, col_map=lambda j: 4 * seg + j, mode="norm",
                    out_dtype=BF16, extra=(sb_gains,), extra_specs=(gain_spec,), tn=tn, name="inproj_sb_qk")

    ret = _retention(ret_qk, v_all, gate, _retention_tables(H), ret_norm)
    sb = _stick_breaking(sb_qk, v_all, sb_out_norm)
    h1 = _outproj(ret.reshape(T, W), sb.reshape(T, W), w_out_b, x2d)

    kmem, vmem = _memkv(mem, mem_norm, cross_w_kv.astype(BF16), cross_k_norm)
    h2, xn_ffn = _cross_attention(h1, S, cross_norm, cross_w_q.astype(BF16), cross_q_norm, kmem, vmem,
                                  cross_w_o.astype(BF16), ffn_norm)

    act, w_down_b = _ffn_up(xn_ffn, S, w_up_b, ffn_conv_w, ffn_conv_b.reshape(1, -1), ffn_w_down)
    out = _ffn_down(act, w_down_b, h2)
    return out.reshape(B, S, D)


def kernel(x, mem, attn_norm, w_in, ret_norm, sb_q_norm, sb_k_norm, sb_out_norm, w_out, cross_norm, mem_norm,
           cross_w_q, cross_w_kv, cross_q_norm, cross_k_norm, cross_w_o, ffn_norm, ffn_w_up, ffn_conv_w,
           ffn_conv_b, ffn_w_down):
    h = x
    for l in range(attn_norm.shape[0]):
        h = _layer(h, mem, attn_norm[l], w_in[l], ret_norm[l], sb_q_norm[l], sb_k_norm[l], sb_out_norm[l],
                   w_out[l], cross_norm[l], mem_norm[l], cross_w_q[l], cross_w_kv[l], cross_q_norm[l],
                   cross_k_norm[l], cross_w_o[l], ffn_norm[l], ffn_w_up[l], ffn_conv_w[l], ffn_conv_b[l],
                   ffn_w_down[l])
    return h
```

```python
import functools

import jax
import jax.numpy as jnp
from jax import lax
from jax.experimental import pallas as pl
from jax.experimental.pallas import tpu as pltpu

HEAD_DIM = 128
EPS = 1e-6
ROPE_BASE = 10000.0
RET_CHUNK = 128
N_CROSS_HEADS = 4
CONV_WIDTH = 3
SUBLANES = 8
LANES = 128

VMEM_LIMIT_BYTES = 56 * 1024 * 1024

ROW_CHUNK = 256

SB_EXP2_UNDERFLOW = 150.0
LOG2_E = 1.4426950408889634

F32 = jnp.float32
BF16 = jnp.bfloat16


def _cparams(*sem):
    return pltpu.CompilerParams(dimension_semantics=sem, vmem_limit_bytes=VMEM_LIMIT_BYTES)


def _rms(x, g):
    return x * lax.rsqrt(jnp.mean(x * x, axis=-1, keepdims=True) + EPS) * g


def _silu(x):
    return x * (1.0 / (1.0 + jnp.exp(-x)))


def _dot(a, b):
    return jnp.dot(a, b, preferred_element_type=F32)


def _dot_nt(a, b):
    return lax.dot_general(a, b, (((1,), (1,)), ((), ())), preferred_element_type=F32)


def _row_chunks(n_rows):
    rc = min(ROW_CHUNK, n_rows)
    return [slice(r0, r0 + rc) for r0 in range(0, n_rows, rc)]


def _rmsnorm_kernel(x_ref, g_ref, o_ref):
    o_ref[...] = _rms(x_ref[...], g_ref[...]).astype(o_ref.dtype)


def _rmsnorm_rows(x2d, g, out_dtype, tm=256):
    T, D = x2d.shape
    tm = min(tm, T)
    return pl.pallas_call(
        _rmsnorm_kernel,
        grid=(T // tm,),
        in_specs=[pl.BlockSpec((tm, D), lambda i: (i, 0)), pl.BlockSpec((1, D), lambda i: (0, 0))],
        out_specs=pl.BlockSpec((tm, D), lambda i: (i, 0)),
        out_shape=jax.ShapeDtypeStruct((T, D), out_dtype),
        compiler_params=_cparams("parallel"),
        name="rmsnorm_rows",
    )(x2d, g.reshape(1, D))


def _cast_block(src_ref, dst_ref, step, n_blocks, n_steps):
    if n_blocks == n_steps:
        dst_ref[...] = src_ref[...].astype(dst_ref.dtype)
    else:

        @pl.when(step < n_blocks)
        def _():
            dst_ref[...] = src_ref[...].astype(dst_ref.dtype)


def _cast_plan(w, n_steps, step_of):
    n_rows = w.shape[0]
    packed_rows = 16
    n_blocks = max(n for n in range(1, n_steps + 1) if n_rows % n == 0 and (n_rows // n) % packed_rows == 0)
    spec = lambda: pl.BlockSpec((n_rows // n_blocks, w.shape[1]),
                                lambda *g: (jnp.minimum(step_of(*g), n_blocks - 1), 0))
    return n_blocks, spec(), spec(), jax.ShapeDtypeStruct(w.shape, BF16)


def _inproj_kernel(a_ref, w_ref, *rest, mode, heads_per_tile, n_scaled_tiles, n_extra, cast_blocks):
    j = pl.program_id(1)
    extra = rest[:n_extra]
    if cast_blocks:
        src_ref, o_ref, dst_ref = rest[n_extra:]
        _cast_block(src_ref, dst_ref, pl.program_id(0) * pl.num_programs(1) + j, cast_blocks[0], cast_blocks[1])
    else:
        (o_ref,) = rest[n_extra:]
    if mode == "rope":
        scale = jnp.where(j < n_scaled_tiles, HEAD_DIM**-0.5, 1.0).astype(F32)
    elif mode == "norm":
        gain = extra[0][0]
    for rows in _row_chunks(a_ref.shape[0]):
        acc = _dot(a_ref[rows, :], w_ref[...])
        for hh in range(heads_per_tile):
            xh = acc[:, hh * HEAD_DIM : (hh + 1) * HEAD_DIM]
            if mode == "rope":
                xh = (xh * extra[0][rows, :] + pltpu.roll(xh, HEAD_DIM // 2, axis=1) * extra[1][rows, :]) * scale
            elif mode == "norm":
                xh = _rms(xh, gain)
            o_ref[0, hh, rows, :] = xh.astype(o_ref.dtype)


def _inproj(xn, w, batch, seq, *, n_col_tiles, col_map, mode, out_dtype, extra=(), extra_specs=(),
            n_scaled_tiles=0, tm=1024, tn=1024, name, cast=None):
    T, D = xn.shape
    tm = min(tm, seq)
    hpt = tn // HEAD_DIM
    tiles_per_seq = seq // tm
    n_steps = (T // tm) * n_col_tiles
    in_specs = [
        pl.BlockSpec((tm, D), lambda i, j: (i, 0)),
        pl.BlockSpec((D, tn), lambda i, j: (0, col_map(j))),
        *extra_specs,
    ]
    out_specs = pl.BlockSpec((1, hpt, tm, HEAD_DIM), lambda i, j: (i // tiles_per_seq, j, i % tiles_per_seq, 0))
    out_shape = jax.ShapeDtypeStruct((batch, n_col_tiles * hpt, seq, HEAD_DIM), out_dtype)
    operands = [xn, w, *extra]
    cast_blocks = None
    if cast is not None:
        n_blocks, src_spec, dst_spec, dst_shape = _cast_plan(cast, n_steps, lambda i, j: i * n_col_tiles + j)
        in_specs.append(src_spec)
        out_specs, out_shape = [out_specs, dst_spec], [out_shape, dst_shape]
        operands.append(cast)
        cast_blocks = (n_blocks, n_steps)
    kern = functools.partial(_inproj_kernel, mode=mode, heads_per_tile=hpt, n_scaled_tiles=n_scaled_tiles,
                             n_extra=len(extra), cast_blocks=cast_blocks)
    return pl.pallas_call(
        kern,
        grid=(T // tm, n_col_tiles),
        in_specs=in_specs,
        out_specs=out_specs,
        out_shape=out_shape,
        compiler_params=_cparams("arbitrary", "arbitrary"),
        name=name,
    )(*operands)


def _retention_kernel(q_ref, k_ref, v_ref, g_ref, dint_ref, kdec_ref, qdec_ref, cdec_ref, gn_ref, o_ref,
                      state_ref, *, n_chunks):
    @pl.when(pl.program_id(2) == 0)
    def _():
        state_ref[...] = jnp.zeros_like(state_ref)

    dint = dint_ref[0]
    kdec = kdec_ref[0]
    qdec = qdec_ref[0]
    cdec = cdec_ref[0]
    gn = gn_ref[...]
    C = RET_CHUNK
    chunks = range(n_chunks)
    rows = [pl.ds(n * C, C) for n in chunks]
    q = [q_ref[0, 0, r, :] for r in rows]
    k = [k_ref[0, 0, r, :] for r in rows]
    scores = [_dot_nt(q[n].astype(BF16), k[n].astype(BF16)) for n in chunks]
    kv = [_dot((k[n] * kdec).T.astype(BF16), v_ref[0, 0, rows[n], :]) for n in chunks]
    states = [state_ref[...]]
    for n in chunks:
        states.append(states[n] * cdec + kv[n])
    state_ref[...] = states[n_chunks]
    out = [
        _dot(
            jnp.concatenate([(scores[n] * dint).astype(BF16), (q[n] * qdec).astype(BF16)], axis=1),
            jnp.concatenate([v_ref[0, 0, rows[n], :], states[n].astype(BF16)], axis=0),
        )
        for n in chunks
    ]
    for n in chunks:
        y = _rms(out[n], gn) * _silu(g_ref[0, 0, rows[n], :])
        o_ref[0, rows[n], :] = y.astype(o_ref.dtype)


def _retention(qk, v_all, gate, tables, gn, *, rows_per_step=4096):
    B, H2, S, d = qk.shape
    H = H2 // 2
    tr = min(rows_per_step, S)
    dint, kdec, qdec, cdec = tables
    C = RET_CHUNK
    kern = functools.partial(_retention_kernel, n_chunks=tr // C)
    head_blk = (1, 1, tr, d)
    return pl.pallas_call(
        kern,
        grid=(B, H, S // tr),
        in_specs=[
            pl.BlockSpec(head_blk, lambda b, h, r: (b, h, r, 0)),
            pl.BlockSpec(head_blk, lambda b, h, r: (b, H + h, r, 0)),
            pl.BlockSpec(head_blk, lambda b, h, r: (b, h, r, 0)),
            pl.BlockSpec(head_blk, lambda b, h, r: (b, h, r, 0)),
            pl.BlockSpec((1, C, C), lambda b, h, r: (h, 0, 0)),
            pl.BlockSpec((1, C, d), lambda b, h, r: (h, 0, 0)),
            pl.BlockSpec((1, C, d), lambda b, h, r: (h, 0, 0)),
            pl.BlockSpec((1, 1, d), lambda b, h, r: (h, 0, 0)),
            pl.BlockSpec((1, d), lambda b, h, r: (0, 0)),
        ],
        out_specs=pl.BlockSpec((1, tr, d), lambda b, h, r: (b, r, h)),
        out_shape=jax.ShapeDtypeStruct((B, S, H * d), BF16),
        scratch_shapes=[pltpu.VMEM((d, d), F32)],
        compiler_params=_cparams("parallel", "parallel", "arbitrary"),
        name="retention",
    )(qk, qk, v_all, gate, dint, kdec, qdec, cdec, gn.reshape(1, d))


def _sb_kernel(q_ref, k_ref, v_ref, w2_ref, gn_ref, o_ref, *, tq, tk, group, n_sub):
    scale = HEAD_DIM**-0.5
    w2 = w2_ref[...]
    gn = gn_ref[...]
    heads = range(group)

    def guard(unit, x):
        _, _, _, _, mask, valid = unit
        if mask is not None:
            head = [] if x.shape[1] == tk else [x[:, : x.shape[1] - tk]]
            x = jnp.concatenate(head + [jnp.where(mask, x[:, x.shape[1] - tk :], 0.0)], axis=1)
        if valid is not None:
            x = jnp.where(valid, x, 0.0)
        return x

    def key_rows(unit):
        return pl.ds(unit[2], unit[3] * tk)

    def scan_keys(units):
        newest = lambda x: x[:, x.shape[1] - tk :]
        z2 = [_dot_nt(u[1], k_ref[0, u[0], key_rows(u), :]) * (scale * LOG2_E) for u in units]
        hi, lo = [], []
        for u, z in zip(units, z2):
            sp = guard(u, jnp.maximum(z, 0.0) + jnp.log2(1.0 + jnp.exp2(-jnp.abs(z))))
            h = sp.astype(BF16)
            hi.append(h)
            lo.append((sp - h.astype(F32)).astype(BF16))
        cr_new = [_dot(jnp.concatenate([newest(h), newest(l)], axis=1), w2) for h, l in zip(hi, lo)]
        cr_old = [_dot(jnp.concatenate([h[:, :tk], l[:, :tk]], axis=1), w2) if u[3] == 2 else None
                  for u, h, l in zip(units, hi, lo)]
        return list(zip(z2, cr_new, cr_old))

    def weigh(units, scans, runs, accs):
        a, new_runs = [], []
        for u, (z2, cr_new, cr_old), run in zip(units, scans, runs):
            run_mid = run + cr_new[:, tk:]
            if cr_old is not None:
                behind = jnp.concatenate([cr_old[:, :tk] + run_mid, cr_new[:, :tk] + run], axis=1)
                new_runs.append(run_mid + cr_old[:, tk:])
            else:
                behind = cr_new[:, :tk] + run
                new_runs.append(run_mid)
            a.append(guard(u, jnp.exp2(z2 - behind)).astype(BF16))
        new_accs = [acc + _dot(w, v_ref[0, u[0], key_rows(u), :]) for u, w, acc in zip(units, a, accs)]
        return new_runs, new_accs

    mask = lax.broadcasted_iota(jnp.int32, (tk, tk), 1) < lax.broadcasted_iota(jnp.int32, (tk, tk), 0)

    def query_block(sub, carry):
        qi = pl.program_id(2) * n_sub + sub
        row0 = pl.multiple_of(sub * tq, tq)
        q_rows = lambda g, start, size: q_ref[0, g, pl.ds(pl.multiple_of(row0 + start, tk), size), :]
        q0 = pl.multiple_of(qi * tq, tq)
        diag = [(g, q_rows(g, half * tk, tk), q0, half + 1, mask, None) for g in heads for half in range(2)]
        prev_start = pl.multiple_of(jnp.maximum(qi - 1, 0) * tq, tq)
        prev = [(g, q_rows(g, 0, tq), prev_start, 2, None, qi > 0) for g in heads]
        scans = scan_keys(diag + prev)
        zero_run, zero_acc = jnp.zeros((tk, tk), F32), jnp.zeros((tk, HEAD_DIM), F32)
        runs, accs = weigh(diag, scans[: len(diag)], [zero_run] * len(diag), [zero_acc] * len(diag))
        runs = [jnp.concatenate([runs[2 * g], runs[2 * g + 1]], axis=0) for g in heads]
        accs = [jnp.concatenate([accs[2 * g], accs[2 * g + 1]], axis=0) for g in heads]
        runs, accs = weigh(prev, scans[len(diag) :], runs, accs)

        def cond(c):
            return jnp.logical_and(c[0] >= 0, jnp.min(functools.reduce(jnp.minimum, c[1])) < SB_EXP2_UNDERFLOW)

        def body(c):
            kblk, runs, accs = c
            kstart = pl.multiple_of(kblk * tq, tq)
            units = [(g, q_rows(g, 0, tq), kstart, 2, None, None) for g in heads]
            runs, accs = weigh(units, scan_keys(units), runs, accs)
            return kblk - 1, tuple(runs), tuple(accs)

        _, _, accs = lax.while_loop(cond, body, (qi - 2, tuple(runs), tuple(accs)))
        for g in heads:
            o_ref[0, pl.ds(row0, tq), g * HEAD_DIM : (g + 1) * HEAD_DIM] = _rms(accs[g], gn).astype(o_ref.dtype)
        return carry

    lax.fori_loop(0, n_sub, query_block, 0)


def _stick_breaking(sqk, v_all, gn, *, tk=128, group=4, blocks_per_step=4):
    B, H2, S, d = sqk.shape
    H = H2 // 2
    tq = 2 * tk
    group = min(group, H)
    n_groups = H // group
    idx = jnp.arange(tk)
    w2 = jnp.concatenate([(idx[:, None] >= idx[None, :]).astype(BF16), jnp.ones((tk, tk), BF16)], axis=1)
    w2 = jnp.concatenate([w2, w2], axis=0)
    n_sub = min(blocks_per_step, S // tq)
    kern = functools.partial(_sb_kernel, tq=tq, tk=tk, group=group, n_sub=n_sub)
    kv_spec = lambda: pl.BlockSpec((1, group, S, d), lambda b, h, qi: (b, n_groups + h, 0, 0))
    return pl.pallas_call(
        kern,
        grid=(B, n_groups, S // (tq * n_sub)),
        in_specs=[
            pl.BlockSpec((1, group, tq * n_sub, d), lambda b, h, qi: (b, h, qi, 0)),
            kv_spec(),
            kv_spec(),
            pl.BlockSpec((2 * tk, 2 * tk), lambda b, h, qi: (0, 0)),
            pl.BlockSpec((1, d), lambda b, h, qi: (0, 0)),
        ],
        out_specs=pl.BlockSpec((1, tq * n_sub, group * d), lambda b, h, qi: (b, qi, h)),
        out_shape=jax.ShapeDtypeStruct((B, S, H * d), BF16),
        compiler_params=_cparams("parallel", "parallel", "arbitrary"),
        name="stick_breaking",
    )(sqk, sqk, v_all, w2, gn.reshape(1, d))


def _outproj_kernel(a1_ref, a2_ref, w1_ref, w2_ref, x_ref, o_ref):
    for rows in _row_chunks(o_ref.shape[0]):
        acc = _dot(a1_ref[rows, :], w1_ref[...]) + _dot(a2_ref[rows, :], w2_ref[...])
        o_ref[rows, :] = x_ref[rows, :] + acc


def _outproj(ret2d, sb2d, w_out, x2d, *, tm=1024, tn=1024):
    T, K = ret2d.shape
    N = w_out.shape[1]
    tm, tn = min(tm, T), min(tn, N)
    return pl.pallas_call(
        _outproj_kernel,
        grid=(T // tm, N // tn),
        in_specs=[
            pl.BlockSpec((tm, K), lambda i, j: (i, 0)),
            pl.BlockSpec((tm, K), lambda i, j: (i, 0)),
            pl.BlockSpec((K, tn), lambda i, j: (0, j)),
            pl.BlockSpec((K, tn), lambda i, j: (1, j)),
            pl.BlockSpec((tm, tn), lambda i, j: (i, j)),
        ],
        out_specs=pl.BlockSpec((tm, tn), lambda i, j: (i, j)),
        out_shape=jax.ShapeDtypeStruct((T, N), F32),
        compiler_params=_cparams("parallel", "arbitrary"),
        name="outproj",
    )(ret2d, sb2d, w_out, w_out, x2d)


def _memkv_kernel(m_ref, gm_ref, w_ref, gk_ref, k_ref, v_ref):
    mn = _rms(m_ref[0], gm_ref[...]).astype(BF16)
    kv = _dot(mn, w_ref[...])
    cw = kv.shape[1] // 2
    gk = gk_ref[...]
    for hh in range(cw // HEAD_DIM):
        sl = slice(hh * HEAD_DIM, (hh + 1) * HEAD_DIM)
        k_ref[0, :, sl] = _rms(kv[:, sl], gk).astype(k_ref.dtype)
    v_ref[0] = kv[:, cw:].astype(v_ref.dtype)


def _memkv(mem, g_mem, w_kv, g_k):
    B, M, D = mem.shape
    cw = w_kv.shape[1] // 2
    return pl.pallas_call(
        _memkv_kernel,
        grid=(B,),
        in_specs=[
            pl.BlockSpec((1, M, D), lambda b: (b, 0, 0)),
            pl.BlockSpec((1, D), lambda b: (0, 0)),
            pl.BlockSpec((D, 2 * cw), lambda b: (0, 0)),
            pl.BlockSpec((1, HEAD_DIM), lambda b: (0, 0)),
        ],
        out_specs=[pl.BlockSpec((1, M, cw), lambda b: (b, 0, 0))] * 2,
        out_shape=[jax.ShapeDtypeStruct((B, M, cw), BF16)] * 2,
        compiler_params=_cparams("parallel"),
        name="memkv",
    )(mem, g_mem.reshape(1, D), w_kv, g_k.reshape(1, HEAD_DIM))


def _cross_kernel(h_ref, gc_ref, wq_ref, gq_ref, k_ref, v_ref, wo_ref, gf_ref, h2_ref, xn_ref):
    n_rows = h_ref.shape[0]
    part_rows = min(LANES, n_rows)
    parts = [slice(r0, r0 + part_rows) for r0 in range(0, n_rows, part_rows)]
    gq = gq_ref[...]
    scale = HEAD_DIM**-0.5
    h = [h_ref[rows, :] for rows in parts]
    xn = [_rms(hp, gc_ref[...]).astype(BF16) for hp in h]
    q = [_dot(x, wq_ref[...]) for x in xn]
    head_cols = [slice(c0, c0 + HEAD_DIM) for c0 in range(0, wq_ref.shape[1], HEAD_DIM)]
    s = [[_dot_nt(_rms(qp[:, sl], gq).astype(BF16), k_ref[0, :, sl]) * scale for sl in head_cols] for qp in q]
    o = []
    for sp in s:
        heads_out = []
        for sh, sl in zip(sp, head_cols):
            e = jnp.exp(sh - jnp.max(sh, axis=-1, keepdims=True))
            p = e / jnp.sum(e, axis=-1, keepdims=True)
            heads_out.append(_dot(p.astype(BF16), v_ref[0, :, sl]).astype(BF16))
        o.append(jnp.concatenate(heads_out, axis=-1))
    h2 = [hp + _dot(op, wo_ref[...]) for hp, op in zip(h, o)]
    for rows, h2p in zip(parts, h2):
        h2_ref[rows, :] = h2p
        xn_ref[rows, :] = _rms(h2p, gf_ref[...]).astype(xn_ref.dtype)


def _cross_attention(h2d, seq, g_cross, w_q, g_q, kmem, vmem, w_o, g_ffn, *, tm=256):
    T, D = h2d.shape
    _, M, cw = kmem.shape
    tm = min(tm, seq)
    tiles_per_seq = seq // tm
    const = lambda i: (0, 0)
    return pl.pallas_call(
        _cross_kernel,
        grid=(T // tm,),
        in_specs=[
            pl.BlockSpec((tm, D), lambda i: (i, 0)),
            pl.BlockSpec((1, D), const),
            pl.BlockSpec((D, cw), const),
            pl.BlockSpec((1, HEAD_DIM), const),
            pl.BlockSpec((1, M, cw), lambda i: (i // tiles_per_seq, 0, 0)),
            pl.BlockSpec((1, M, cw), lambda i: (i // tiles_per_seq, 0, 0)),
            pl.BlockSpec((cw, D), const),
            pl.BlockSpec((1, D), const),
        ],
        out_specs=[pl.BlockSpec((tm, D), lambda i: (i, 0))] * 2,
        out_shape=[jax.ShapeDtypeStruct((T, D), F32), jax.ShapeDtypeStruct((T, D), BF16)],
        compiler_params=_cparams("parallel"),
        name="cross_attention",
    )(h2d, g_cross.reshape(1, D), w_q, g_q.reshape(1, HEAD_DIM), kmem, vmem, w_o, g_ffn.reshape(1, D))


CONV_PIECE_ROWS = 64


def _ffn_up_kernel(a_ref, wg_ref, wv_ref, cwg_ref, cwv_ref, cbg_ref, cbv_ref, src_ref, o_ref, dst_ref, ubuf_ref,
                   carry_ref, *, tiles_per_seq, cast_blocks):
    i = pl.program_id(0)
    j = pl.program_id(1)

    @pl.when(jnp.logical_and(i == 0, j == 0))
    def _():
        carry_ref[...] = jnp.zeros_like(carry_ref)

    _cast_block(src_ref, dst_ref, i * pl.num_programs(1) + j, cast_blocks[0], cast_blocks[1])
    seq_start = (i % tiles_per_seq) == 0
    halo = SUBLANES
    chunks = _row_chunks(a_ref.shape[0])
    size = lambda c: chunks[c].stop - chunks[c].start
    lane_tiles = range(o_ref.shape[1] // LANES)
    projections = ((wg_ref, cwg_ref, cbg_ref), (wv_ref, cwv_ref, cbv_ref))

    def project(c):
        par = c % 2
        a = a_ref[chunks[c], :]
        for slot, (w_ref, _, _) in enumerate(projections):
            u = _dot(a, w_ref[...])
            for lt in lane_tiles:
                if c == 0:
                    ubuf_ref[par, slot, lt, 0:halo, :] = jnp.where(seq_start, 0.0, carry_ref[j, slot, lt])
                else:
                    ubuf_ref[par, slot, lt, 0:halo, :] = ubuf_ref[1 - par, slot, lt, size(c - 1) : size(c - 1) + halo, :]
                ubuf_ref[par, slot, lt, halo : halo + size(c), :] = u[:, lt * LANES : (lt + 1) * LANES]

    def activate(c):
        par = c % 2
        piece = min(CONV_PIECE_ROWS, size(c))
        for lt in lane_tiles:
            cols = slice(lt * LANES, (lt + 1) * LANES)
            for p0 in range(0, size(c), piece):

                def conv(slot):
                    _, cw_ref, cb_ref = projections[slot]
                    cw = cw_ref[:, cols]
                    c2 = cb_ref[:, cols] + ubuf_ref[par, slot, lt, pl.ds(p0 + halo - 2, piece), :] * cw[0:1, :]
                    c2 = c2 + ubuf_ref[par, slot, lt, pl.ds(p0 + halo - 1, piece), :] * cw[1:2, :]
                    return c2 + ubuf_ref[par, slot, lt, pl.ds(p0 + halo, piece), :] * cw[2:3, :]

                out_rows = slice(chunks[c].start + p0, chunks[c].start + p0 + piece)
                o_ref[out_rows, cols] = (_silu(conv(0)) * conv(1)).astype(o_ref.dtype)

    project(0)
    for c in range(1, len(chunks)):
        project(c)
        activate(c - 1)
    activate(len(chunks) - 1)
    last = len(chunks) - 1
    for slot in range(2):
        for lt in lane_tiles:
            carry_ref[j, slot, lt] = ubuf_ref[last % 2, slot, lt, size(last) : size(last) + halo, :]


def _ffn_up(xn, seq, w_up, conv_w, conv_b, cast, *, tm=2048, tf=256):
    T, D = xn.shape
    F = w_up.shape[1] // 2
    tm = min(tm, seq)
    assert F % tf == 0 and seq % tm == 0
    n_j = F // tf
    n_steps = (T // tm) * n_j
    n_blocks, src_spec, dst_spec, dst_shape = _cast_plan(cast, n_steps, lambda i, j: i * n_j + j)
    kern = functools.partial(_ffn_up_kernel, tiles_per_seq=seq // tm, cast_blocks=(n_blocks, n_steps))
    gate_col = lambda i, j: (0, j)
    value_col = lambda i, j: (0, n_j + j)
    return pl.pallas_call(
        kern,
        grid=(T // tm, n_j),
        in_specs=[
            pl.BlockSpec((tm, D), lambda i, j: (i, 0)),
            pl.BlockSpec((D, tf), gate_col),
            pl.BlockSpec((D, tf), value_col),
            pl.BlockSpec((CONV_WIDTH, tf), gate_col),
            pl.BlockSpec((CONV_WIDTH, tf), value_col),
            pl.BlockSpec((1, tf), gate_col),
            pl.BlockSpec((1, tf), value_col),
            src_spec,
        ],
        out_specs=[pl.BlockSpec((tm, tf), lambda i, j: (i, j)), dst_spec],
        out_shape=[jax.ShapeDtypeStruct((T, F), BF16), dst_shape],
        scratch_shapes=[
            pltpu.VMEM((2, 2, tf // LANES, SUBLANES + min(ROW_CHUNK, tm), LANES), F32),
            pltpu.VMEM((n_j, 2, tf // LANES, SUBLANES, LANES), F32),
        ],
        compiler_params=_cparams("arbitrary", "arbitrary"),
        name="ffn_up",
    )(xn, w_up, w_up, conv_w, conv_w, conv_b, conv_b, cast)


def _ffn_down_kernel(a_ref, w_ref, h_ref, o_ref):
    for rows in _row_chunks(o_ref.shape[0]):
        o_ref[rows, :] = h_ref[rows, :] + _dot(a_ref[rows, :], w_ref[...])


def _ffn_down(act, w_down, h2d, *, tm=512, tn=512):
    T, F = act.shape
    N = w_down.shape[1]
    tm, tn = min(tm, T), min(tn, N)
    return pl.pallas_call(
        _ffn_down_kernel,
        grid=(N // tn, T // tm),
        in_specs=[
            pl.BlockSpec((tm, F), lambda j, i: (i, 0)),
            pl.BlockSpec((F, tn), lambda j, i: (0, j)),
            pl.BlockSpec((tm, tn), lambda j, i: (i, j)),
        ],
        out_specs=pl.BlockSpec((tm, tn), lambda j, i: (i, j)),
        out_shape=jax.ShapeDtypeStruct((T, N), F32),
        compiler_params=_cparams("parallel", "arbitrary"),
        name="ffn_down",
    )(act, w_down, h2d)


def _rope_tables(S):
    inv_freq = ROPE_BASE ** (-jnp.linspace(0.0, 1.0, HEAD_DIM // 2, dtype=F32))
    ang = jnp.arange(S, dtype=F32)[:, None] * inv_freq[None, :]
    cos, sin = jnp.cos(ang), jnp.sin(ang)
    return jnp.concatenate([cos, cos], axis=-1), jnp.concatenate([-sin, sin], axis=-1)


def _retention_tables(H):
    C = RET_CHUNK
    log_g = jnp.log1p(-jnp.exp2(-5.0 - jnp.arange(H, dtype=F32)))
    idx = jnp.arange(C, dtype=F32)
    diff = idx[:, None] - idx[None, :]
    dint = jnp.where(diff >= 0, jnp.exp(jnp.maximum(diff, 0.0)[None] * log_g[:, None, None]), 0.0)
    kdec = jnp.exp((C - 1 - idx)[None, :] * log_g[:, None])
    qdec = jnp.exp((idx + 1.0)[None, :] * log_g[:, None])
    cdec = jnp.exp(C * log_g)
    lanes = lambda t: jnp.broadcast_to(t[..., None], t.shape + (HEAD_DIM,))
    return dint, lanes(kdec), lanes(qdec), lanes(cdec[:, None])


def _layer(h, mem, attn_norm, w_in, ret_norm, sb_q_norm, sb_k_norm, sb_out_norm, w_out, cross_norm, mem_norm,
           cross_w_q, cross_w_kv, cross_q_norm, cross_k_norm, cross_w_o, ffn_norm, ffn_w_up, ffn_conv_w,
           ffn_conv_b, ffn_w_down):
    B, S, D = h.shape
    T = B * S
    W = D // 2
    H = W // HEAD_DIM
    tn = min(1024, W)
    seg = W // tn
    x2d = h.reshape(T, D)

    xn = _rmsnorm_rows(x2d, attn_norm, BF16)
    w_in_b = w_in.astype(BF16)
    cos2, sin2 = _rope_tables(S)
    tm_in = min(1024, S)
    tiles_per_seq = S // tm_in
    rope_spec = pl.BlockSpec((tm_in, HEAD_DIM), lambda i, j: (i % tiles_per_seq, 0))
    ret_qk = _inproj(xn, w_in_b, B, S, n_col_tiles=2 * seg, col_map=lambda j: j, mode="rope", out_dtype=F32,
                     extra=(cos2, sin2), extra_specs=(rope_spec, rope_spec), n_scaled_tiles=seg, tn=tn,
                     name="inproj_ret_qk")
    v_all, w_up_b = _inproj(xn, w_in_b, B, S, n_col_tiles=2 * seg,
                            col_map=lambda j: jnp.where(j < seg, 2 * seg + j, 5 * seg + j), mode="plain",
                            out_dtype=BF16, tn=tn, name="inproj_v", cast=ffn_w_up)
    gate, w_out_b = _inproj(xn, w_in_b, B, S, n_col_tiles=seg, col_map=lambda j: 3 * seg + j, mode="plain",
                            out_dtype=F32, tn=tn, name="inproj_gate", cast=w_out)
    sb_gains = jnp.stack([sb_q_norm, sb_k_norm]).reshape(2, 1, HEAD_DIM)
    gain_spec = pl.BlockSpec((1, 1, HEAD_DIM), lambda i, j: (j // seg, 0, 0))
    sb_qk = _inproj(xn, w_in_b, B, S, n_col_tiles=2 * seg, col_map=lambda j: 4 * seg + j, mode="norm",
                    out_dtype=BF16, extra=(sb_gains,), extra_specs=(gain_spec,), tn=tn, name="inproj_sb_qk")

    ret = _retention(ret_qk, v_all, gate, _retention_tables(H), ret_norm)
    sb = _stick_breaking(sb_qk, v_all, sb_out_norm)
    h1 = _outproj(ret.reshape(T, W), sb.reshape(T, W), w_out_b, x2d)

    kmem, vmem = _memkv(mem, mem_norm, cross_w_kv.astype(BF16), cross_k_norm)
    h2, xn_ffn = _cross_attention(h1, S, cross_norm, cross_w_q.astype(BF16), cross_q_norm, kmem, vmem,
                                  cross_w_o.astype(BF16), ffn_norm)

    act, w_down_b = _ffn_up(xn_ffn, S, w_up_b, ffn_conv_w, ffn_conv_b.reshape(1, -1), ffn_w_down)
    out = _ffn_down(act, w_down_b, h2)
    return out.reshape(B, S, D)


def kernel(x, mem, attn_norm, w_in, ret_norm, sb_q_norm, sb_k_norm, sb_out_norm, w_out, cross_norm, mem_norm,
           cross_w_q, cross_w_kv, cross_q_norm, cross_k_norm, cross_w_o, ffn_norm, ffn_w_up, ffn_conv_w,
           ffn_conv_b, ffn_w_down):
    h = x
    for l in range(attn_norm.shape[0]):
        h = _layer(h, mem, attn_norm[l], w_in[l], ret_norm[l], sb_q_norm[l], sb_k_norm[l], sb_out_norm[l],
                   w_out[l], cross_norm[l], mem_norm[l], cross_w_q[l], cross_w_kv[l], cross_q_norm[l],
                   cross_k_norm[l], cross_w_o[l], ffn_norm[l], ffn_w_up[l], ffn_conv_w[l], ffn_conv_b[l],
                   ffn_w_down[l])
    return h
```

```python
import functools

import jax
import jax.numpy as jnp
from jax import lax
from jax.experimental import pallas as pl
from jax.experimental.pallas import tpu as pltpu

HEAD_DIM = 128
EPS = 1e-6
ROPE_BASE = 10000.0
RET_CHUNK = 128
N_CROSS_HEADS = 4
CONV_WIDTH = 3
SUBLANES = 8
LANES = 128

VMEM_LIMIT_BYTES = 56 * 1024 * 1024

ROW_CHUNK = 256

SB_EXP2_UNDERFLOW = 150.0
LOG2_E = 1.4426950408889634

F32 = jnp.float32
BF16 = jnp.bfloat16


def _cparams(*sem):
    return pltpu.CompilerParams(dimension_semantics=sem, vmem_limit_bytes=VMEM_LIMIT_BYTES)


def _rms(x, g):
    return x * lax.rsqrt(jnp.mean(x * x, axis=-1, keepdims=True) + EPS) * g


def _silu(x):
    return x * (1.0 / (1.0 + jnp.exp(-x)))


def _dot(a, b):
    return jnp.dot(a, b, preferred_element_type=F32)


def _dot_nt(a, b):
    return lax.dot_general(a, b, (((1,), (1,)), ((), ())), preferred_element_type=F32)


def _row_chunks(n_rows):
    rc = min(ROW_CHUNK, n_rows)
    return [slice(r0, r0 + rc) for r0 in range(0, n_rows, rc)]


def _rmsnorm_kernel(x_ref, g_ref, o_ref):
    o_ref[...] = _rms(x_ref[...], g_ref[...]).astype(o_ref.dtype)


def _rmsnorm_rows(x2d, g, out_dtype, tm=256):
    T, D = x2d.shape
    tm = min(tm, T)
    return pl.pallas_call(
        _rmsnorm_kernel,
        grid=(T // tm,),
        in_specs=[pl.BlockSpec((tm, D), lambda i: (i, 0)), pl.BlockSpec((1, D), lambda i: (0, 0))],
        out_specs=pl.BlockSpec((tm, D), lambda i: (i, 0)),
        out_shape=jax.ShapeDtypeStruct((T, D), out_dtype),
        compiler_params=_cparams("parallel"),
        name="rmsnorm_rows",
    )(x2d, g.reshape(1, D))


def _cast_block(src_ref, dst_ref, step, n_blocks, n_steps):
    if n_blocks == n_steps:
        dst_ref[...] = src_ref[...].astype(dst_ref.dtype)
    else:

        @pl.when(step < n_blocks)
        def _():
            dst_ref[...] = src_ref[...].astype(dst_ref.dtype)


def _cast_plan(w, n_steps, step_of):
    n_rows = w.shape[0]
    packed_rows = 16
    n_blocks = max(n for n in range(1, n_steps + 1) if n_rows % n == 0 and (n_rows // n) % packed_rows == 0)
    spec = lambda: pl.BlockSpec((n_rows // n_blocks, w.shape[1]),
                                lambda *g: (jnp.minimum(step_of(*g), n_blocks - 1), 0))
    return n_blocks, spec(), spec(), jax.ShapeDtypeStruct(w.shape, BF16)


def _inproj_kernel(a_ref, w_ref, *rest, mode, heads_per_tile, n_scaled_tiles, n_extra, cast_blocks):
    j = pl.program_id(1)
    extra = rest[:n_extra]
    if cast_blocks:
        src_ref, o_ref, dst_ref = rest[n_extra:]
        _cast_block(src_ref, dst_ref, pl.program_id(0) * pl.num_programs(1) + j, cast_blocks[0], cast_blocks[1])
    else:
        (o_ref,) = rest[n_extra:]
    if mode == "rope":
        scale = jnp.where(j < n_scaled_tiles, HEAD_DIM**-0.5, 1.0).astype(F32)
    elif mode == "norm":
        gain = extra[0][0]
    for rows in _row_chunks(a_ref.shape[0]):
        acc = _dot(a_ref[rows, :], w_ref[...])
        for hh in range(heads_per_tile):
            xh = acc[:, hh * HEAD_DIM : (hh + 1) * HEAD_DIM]
            if mode == "rope":
                xh = (xh * extra[0][rows, :] + pltpu.roll(xh, HEAD_DIM // 2, axis=1) * extra[1][rows, :]) * scale
            elif mode == "norm":
                xh = _rms(xh, gain)
            o_ref[0, hh, rows, :] = xh.astype(o_ref.dtype)


def _inproj(xn, w, batch, seq, *, n_col_tiles, col_map, mode, out_dtype, extra=(), extra_specs=(),
            n_scaled_tiles=0, tm=1024, tn=1024, name, cast=None):
    T, D = xn.shape
    tm = min(tm, seq)
    hpt = tn // HEAD_DIM
    tiles_per_seq = seq // tm
    n_steps = (T // tm) * n_col_tiles
    in_specs = [
        pl.BlockSpec((tm, D), lambda i, j: (i, 0)),
        pl.BlockSpec((D, tn), lambda i, j: (0, col_map(j))),
        *extra_specs,
    ]
    out_specs = pl.BlockSpec((1, hpt, tm, HEAD_DIM), lambda i, j: (i // tiles_per_seq, j, i % tiles_per_seq, 0))
    out_shape = jax.ShapeDtypeStruct((batch, n_col_tiles * hpt, seq, HEAD_DIM), out_dtype)
    operands = [xn, w, *extra]
    cast_blocks = None
    if cast is not None:
        n_blocks, src_spec, dst_spec, dst_shape = _cast_plan(cast, n_steps, lambda i, j: i * n_col_tiles + j)
        in_specs.append(src_spec)
        out_specs, out_shape = [out_specs, dst_spec], [out_shape, dst_shape]
        operands.append(cast)
        cast_blocks = (n_blocks, n_steps)
    kern = functools.partial(_inproj_kernel, mode=mode, heads_per_tile=hpt, n_scaled_tiles=n_scaled_tiles,
                             n_extra=len(extra), cast_blocks=cast_blocks)
    return pl.pallas_call(
        kern,
        grid=(T // tm, n_col_tiles),
        in_specs=in_specs,
        out_specs=out_specs,
        out_shape=out_shape,
        compiler_params=_cparams("arbitrary", "arbitrary"),
        name=name,
    )(*operands)


def _retention_kernel(q_ref, k_ref, v_ref, g_ref, dint_ref, kdec_ref, qdec_ref, cdec_ref, gn_ref, o_ref,
                      state_ref, *, n_chunks):
    @pl.when(pl.program_id(2) == 0)
    def _():
        state_ref[...] = jnp.zeros_like(state_ref)

    dint = dint_ref[0]
    kdec = kdec_ref[0]
    qdec = qdec_ref[0]
    cdec = cdec_ref[0]
    gn = gn_ref[...]
    C = RET_CHUNK
    chunks = range(n_chunks)
    rows = [pl.ds(n * C, C) for n in chunks]
    q = [q_ref[0, 0, r, :] for r in rows]
    k = [k_ref[0, 0, r, :] for r in rows]
    scores = [_dot_nt(q[n].astype(BF16), k[n].astype(BF16)) for n in chunks]
    kv = [_dot((k[n] * kdec).T.astype(BF16), v_ref[0, 0, rows[n], :]) for n in chunks]
    states = [state_ref[...]]
    for n in chunks:
        states.append(states[n] * cdec + kv[n])
    state_ref[...] = states[n_chunks]
    out = [
        _dot(
            jnp.concatenate([(scores[n] * dint).astype(BF16), (q[n] * qdec).astype(BF16)], axis=1),
            jnp.concatenate([v_ref[0, 0, rows[n], :], states[n].astype(BF16)], axis=0),
        )
        for n in chunks
    ]
    for n in chunks:
        y = _rms(out[n], gn) * _silu(g_ref[0, 0, rows[n], :])
        o_ref[0, rows[n], :] = y.astype(o_ref.dtype)


def _retention(qk, v_all, gate, tables, gn, *, rows_per_step=8192):
    B, H2, S, d = qk.shape
    H = H2 // 2
    tr = min(rows_per_step, S)
    dint, kdec, qdec, cdec = tables
    C = RET_CHUNK
    kern = functools.partial(_retention_kernel, n_chunks=tr // C)
    head_blk = (1, 1, tr, d)
    return pl.pallas_call(
        kern,
        grid=(B, H, S // tr),
        in_specs=[
            pl.BlockSpec(head_blk, lambda b, h, r: (b, h, r, 0)),
            pl.BlockSpec(head_blk, lambda b, h, r: (b, H + h, r, 0)),
            pl.BlockSpec(head_blk, lambda b, h, r: (b, h, r, 0)),
            pl.BlockSpec(head_blk, lambda b, h, r: (b, h, r, 0)),
            pl.BlockSpec((1, C, C), lambda b, h, r: (h, 0, 0)),
            pl.BlockSpec((1, C, d), lambda b, h, r: (h, 0, 0)),
            pl.BlockSpec((1, C, d), lambda b, h, r: (h, 0, 0)),
            pl.BlockSpec((1, 1, d), lambda b, h, r: (h, 0, 0)),
            pl.BlockSpec((1, d), lambda b, h, r: (0, 0)),
        ],
        out_specs=pl.BlockSpec((1, tr, d), lambda b, h, r: (b, r, h)),
        out_shape=jax.ShapeDtypeStruct((B, S, H * d), BF16),
        scratch_shapes=[pltpu.VMEM((d, d), F32)],
        compiler_params=_cparams("parallel", "parallel", "arbitrary"),
        name="retention",
    )(qk, qk, v_all, gate, dint, kdec, qdec, cdec, gn.reshape(1, d))


def _sb_kernel(q_ref, k_ref, v_ref, w2_ref, gn_ref, o_ref, *, tq, tk, group, n_sub):
    scale = HEAD_DIM**-0.5
    w2 = w2_ref[...]
    gn = gn_ref[...]
    heads = range(group)

    def guard(unit, x):
        mask = unit[4]
        if mask is not None:
            head = [] if x.shape[1] == tk else [x[:, : x.shape[1] - tk]]
            x = jnp.concatenate(head + [jnp.where(mask, x[:, x.shape[1] - tk :], 0.0)], axis=1)
        return x

    def key_rows(unit):
        return pl.ds(unit[2], unit[3] * tk)

    def scan_keys(units):
        newest = lambda x: x[:, x.shape[1] - tk :]
        z2 = [_dot_nt(u[1], k_ref[0, u[0], key_rows(u), :]) * (scale * LOG2_E) for u in units]
        hi, lo = [], []
        for u, z in zip(units, z2):
            sp = guard(u, jnp.maximum(z, 0.0) + jnp.log2(1.0 + jnp.exp2(-jnp.abs(z))))
            h = sp.astype(BF16)
            hi.append(h)
            lo.append((sp - h.astype(F32)).astype(BF16))
        cr_new = [_dot(jnp.concatenate([newest(h), newest(l)], axis=1), w2) for h, l in zip(hi, lo)]
        cr_old = [_dot(jnp.concatenate([h[:, :tk], l[:, :tk]], axis=1), w2) if u[3] == 2 else None
                  for u, h, l in zip(units, hi, lo)]
        return list(zip(z2, cr_new, cr_old))

    def weigh(units, scans, runs, accs):
        a, new_runs = [], []
        for u, (z2, cr_new, cr_old), run in zip(units, scans, runs):
            run_mid = run + cr_new[:, tk:]
            if cr_old is not None:
                behind = jnp.concatenate([cr_old[:, :tk] + run_mid, cr_new[:, :tk] + run], axis=1)
                new_runs.append(run_mid + cr_old[:, tk:])
            else:
                behind = cr_new[:, :tk] + run
                new_runs.append(run_mid)
            a.append(guard(u, jnp.exp2(z2 - behind)).astype(BF16))
        new_accs = []
        for u, w, acc in zip(units, a, accs):
            out = _dot(w, v_ref[0, u[0], key_rows(u), :])
            new_accs.append(acc + (out if u[5] is None else jnp.where(u[5], out, 0.0)))
        return new_runs, new_accs

    mask = lax.broadcasted_iota(jnp.int32, (tk, tk), 1) < lax.broadcasted_iota(jnp.int32, (tk, tk), 0)

    def query_block(sub, carry):
        qi = pl.program_id(2) * n_sub + sub
        row0 = pl.multiple_of(sub * tq, tq)
        q_rows = lambda g, start, size: q_ref[0, g, pl.ds(pl.multiple_of(row0 + start, tk), size), :]
        q0 = pl.multiple_of(qi * tq, tq)
        diag = [(g, q_rows(g, half * tk, tk), q0, half + 1, mask, None) for g in heads for half in range(2)]
        prev_start = pl.multiple_of(jnp.maximum(qi - 1, 0) * tq, tq)
        prev = [(g, q_rows(g, 0, tq), prev_start, 2, None, qi > 0) for g in heads]
        scans = scan_keys(diag + prev)
        zero_run, zero_acc = jnp.zeros((tk, tk), F32), jnp.zeros((tk, HEAD_DIM), F32)
        runs, accs = weigh(diag, scans[: len(diag)], [zero_run] * len(diag), [zero_acc] * len(diag))
        runs = [jnp.concatenate([runs[2 * g], runs[2 * g + 1]], axis=0) for g in heads]
        accs = [jnp.concatenate([accs[2 * g], accs[2 * g + 1]], axis=0) for g in heads]
        runs, accs = weigh(prev, scans[len(diag) :], runs, accs)

        def cond(c):
            return jnp.logical_and(c[0] >= 0, jnp.min(functools.reduce(jnp.minimum, c[1])) < SB_EXP2_UNDERFLOW)

        def body(c):
            kblk, runs, accs = c
            kstart = pl.multiple_of(kblk * tq, tq)
            units = [(g, q_rows(g, 0, tq), kstart, 2, None, None) for g in heads]
            runs, accs = weigh(units, scan_keys(units), runs, accs)
            return kblk - 1, tuple(runs), tuple(accs)

        _, _, accs = lax.while_loop(cond, body, (qi - 2, tuple(runs), tuple(accs)))
        for g in heads:
            o_ref[0, pl.ds(row0, tq), g * HEAD_DIM : (g + 1) * HEAD_DIM] = _rms(accs[g], gn).astype(o_ref.dtype)
        return carry

    lax.fori_loop(0, n_sub, query_block, 0)


def _stick_breaking(sqk, v_all, gn, *, tk=128, group=4, blocks_per_step=8):
    B, H2, S, d = sqk.shape
    H = H2 // 2
    tq = 2 * tk
    group = min(group, H)
    n_groups = H // group
    idx = jnp.arange(tk)
    w2 = jnp.concatenate([(idx[:, None] >= idx[None, :]).astype(BF16), jnp.ones((tk, tk), BF16)], axis=1)
    w2 = jnp.concatenate([w2, w2], axis=0)
    n_sub = min(blocks_per_step, S // tq)
    kern = functools.partial(_sb_kernel, tq=tq, tk=tk, group=group, n_sub=n_sub)
    kv_spec = lambda: pl.BlockSpec((1, group, S, d), lambda b, h, qi: (b, n_groups + h, 0, 0))
    return pl.pallas_call(
        kern,
        grid=(B, n_groups, S // (tq * n_sub)),
        in_specs=[
            pl.BlockSpec((1, group, tq * n_sub, d), lambda b, h, qi: (b, h, qi, 0)),
            kv_spec(),
            kv_spec(),
            pl.BlockSpec((2 * tk, 2 * tk), lambda b, h, qi: (0, 0)),
            pl.BlockSpec((1, d), lambda b, h, qi: (0, 0)),
        ],
        out_specs=pl.BlockSpec((1, tq * n_sub, group * d), lambda b, h, qi: (b, qi, h)),
        out_shape=jax.ShapeDtypeStruct((B, S, H * d), BF16),
        compiler_params=_cparams("parallel", "parallel", "arbitrary"),
        name="stick_breaking",
    )(sqk, sqk, v_all, w2, gn.reshape(1, d))


def _outproj_kernel(a1_ref, a2_ref, w1_ref, w2_ref, x_ref, o_ref):
    for rows in _row_chunks(o_ref.shape[0]):
        acc = _dot(a1_ref[rows, :], w1_ref[...]) + _dot(a2_ref[rows, :], w2_ref[...])
        o_ref[rows, :] = x_ref[rows, :] + acc


def _outproj(ret2d, sb2d, w_out, x2d, *, tm=1024, tn=1024):
    T, K = ret2d.shape
    N = w_out.shape[1]
    tm, tn = min(tm, T), min(tn, N)
    return pl.pallas_call(
        _outproj_kernel,
        grid=(T // tm, N // tn),
        in_specs=[
            pl.BlockSpec((tm, K), lambda i, j: (i, 0)),
            pl.BlockSpec((tm, K), lambda i, j: (i, 0)),
            pl.BlockSpec((K, tn), lambda i, j: (0, j)),
            pl.BlockSpec((K, tn), lambda i, j: (1, j)),
            pl.BlockSpec((tm, tn), lambda i, j: (i, j)),
        ],
        out_specs=pl.BlockSpec((tm, tn), lambda i, j: (i, j)),
        out_shape=jax.ShapeDtypeStruct((T, N), F32),
        compiler_params=_cparams("parallel", "arbitrary"),
        name="outproj",
    )(ret2d, sb2d, w_out, w_out, x2d)


def _memkv_kernel(m_ref, gm_ref, w_ref, gk_ref, k_ref, v_ref):
    mn = _rms(m_ref[0], gm_ref[...]).astype(BF16)
    kv = _dot(mn, w_ref[...])
    cw = kv.shape[1] // 2
    gk = gk_ref[...]
    for hh in range(cw // HEAD_DIM):
        sl = slice(hh * HEAD_DIM, (hh + 1) * HEAD_DIM)
        k_ref[0, :, sl] = _rms(kv[:, sl], gk).astype(k_ref.dtype)
    v_ref[0] = kv[:, cw:].astype(v_ref.dtype)


def _memkv(mem, g_mem, w_kv, g_k):
    B, M, D = mem.shape
    cw = w_kv.shape[1] // 2
    return pl.pallas_call(
        _memkv_kernel,
        grid=(B,),
        in_specs=[
            pl.BlockSpec((1, M, D), lambda b: (b, 0, 0)),
            pl.BlockSpec((1, D), lambda b: (0, 0)),
            pl.BlockSpec((D, 2 * cw), lambda b: (0, 0)),
            pl.BlockSpec((1, HEAD_DIM), lambda b: (0, 0)),
        ],
        out_specs=[pl.BlockSpec((1, M, cw), lambda b: (b, 0, 0))] * 2,
        out_shape=[jax.ShapeDtypeStruct((B, M, cw), BF16)] * 2,
        compiler_params=_cparams("parallel"),
        name="memkv",
    )(mem, g_mem.reshape(1, D), w_kv, g_k.reshape(1, HEAD_DIM))


def _cross_kernel(h_ref, gc_ref, wq_ref, gq_ref, k_ref, v_ref, wo_ref, gf_ref, h2_ref, xn_ref):
    n_rows = h_ref.shape[0]
    part_rows = min(LANES, n_rows)
    parts = [slice(r0, r0 + part_rows) for r0 in range(0, n_rows, part_rows)]
    gq = gq_ref[...]
    scale = HEAD_DIM**-0.5
    h = [h_ref[rows, :] for rows in parts]
    xn = [_rms(hp, gc_ref[...]).astype(BF16) for hp in h]
    q = [_dot(x, wq_ref[...]) for x in xn]
    head_cols = [slice(c0, c0 + HEAD_DIM) for c0 in range(0, wq_ref.shape[1], HEAD_DIM)]
    s = [[_dot_nt(_rms(qp[:, sl], gq).astype(BF16), k_ref[0, :, sl]) * scale for sl in head_cols] for qp in q]
    o = []
    for sp in s:
        heads_out = []
        for sh, sl in zip(sp, head_cols):
            e = jnp.exp(sh - jnp.max(sh, axis=-1, keepdims=True))
            p = e / jnp.sum(e, axis=-1, keepdims=True)
            heads_out.append(_dot(p.astype(BF16), v_ref[0, :, sl]).astype(BF16))
        o.append(jnp.concatenate(heads_out, axis=-1))
    h2 = [hp + _dot(op, wo_ref[...]) for hp, op in zip(h, o)]
    for rows, h2p in zip(parts, h2):
        h2_ref[rows, :] = h2p
        xn_ref[rows, :] = _rms(h2p, gf_ref[...]).astype(xn_ref.dtype)


def _cross_attention(h2d, seq, g_cross, w_q, g_q, kmem, vmem, w_o, g_ffn, *, tm=256):
    T, D = h2d.shape
    _, M, cw = kmem.shape
    tm = min(tm, seq)
    tiles_per_seq = seq // tm
    const = lambda i: (0, 0)
    return pl.pallas_call(
        _cross_kernel,
        grid=(T // tm,),
        in_specs=[
            pl.BlockSpec((tm, D), lambda i: (i, 0)),
            pl.BlockSpec((1, D), const),
            pl.BlockSpec((D, cw), const),
            pl.BlockSpec((1, HEAD_DIM), const),
            pl.BlockSpec((1, M, cw), lambda i: (i // tiles_per_seq, 0, 0)),
            pl.BlockSpec((1, M, cw), lambda i: (i // tiles_per_seq, 0, 0)),
            pl.BlockSpec((cw, D), const),
            pl.BlockSpec((1, D), const),
        ],
        out_specs=[pl.BlockSpec((tm, D), lambda i: (i, 0))] * 2,
        out_shape=[jax.ShapeDtypeStruct((T, D), F32), jax.ShapeDtypeStruct((T, D), BF16)],
        compiler_params=_cparams("parallel"),
        name="cross_attention",
    )(h2d, g_cross.reshape(1, D), w_q, g_q.reshape(1, HEAD_DIM), kmem, vmem, w_o, g_ffn.reshape(1, D))


CONV_PIECE_ROWS = 64


def _ffn_up_kernel(a_ref, wg_ref, wv_ref, cwg_ref, cwv_ref, cbg_ref, cbv_ref, src_ref, o_ref, dst_ref, ubuf_ref,
                   carry_ref, *, tiles_per_seq, cast_blocks):
    i = pl.program_id(0)
    j = pl.program_id(1)

    @pl.when(jnp.logical_and(i == 0, j == 0))
    def _():
        carry_ref[...] = jnp.zeros_like(carry_ref)

    _cast_block(src_ref, dst_ref, i * pl.num_programs(1) + j, cast_blocks[0], cast_blocks[1])
    seq_start = (i % tiles_per_seq) == 0
    halo = SUBLANES
    chunks = _row_chunks(a_ref.shape[0])
    size = lambda c: chunks[c].stop - chunks[c].start
    lane_tiles = range(o_ref.shape[1] // LANES)
    projections = ((wg_ref, cwg_ref, cbg_ref), (wv_ref, cwv_ref, cbv_ref))

    def project(c):
        par = c % 2
        a = a_ref[chunks[c], :]
        for slot, (w_ref, _, _) in enumerate(projections):
            u = _dot(a, w_ref[...])
            for lt in lane_tiles:
                if c == 0:
                    ubuf_ref[par, slot, lt, 0:halo, :] = jnp.where(seq_start, 0.0, carry_ref[j, slot, lt])
                else:
                    ubuf_ref[par, slot, lt, 0:halo, :] = ubuf_ref[1 - par, slot, lt, size(c - 1) : size(c - 1) + halo, :]
                ubuf_ref[par, slot, lt, halo : halo + size(c), :] = u[:, lt * LANES : (lt + 1) * LANES]

    def activate(c):
        par = c % 2
        piece = min(CONV_PIECE_ROWS, size(c))
        for lt in lane_tiles:
            cols = slice(lt * LANES, (lt + 1) * LANES)
            for p0 in range(0, size(c), piece):

                def conv(slot):
                    _, cw_ref, cb_ref = projections[slot]
                    cw = cw_ref[:, cols]
                    c2 = cb_ref[:, cols] + ubuf_ref[par, slot, lt, pl.ds(p0 + halo - 2, piece), :] * cw[0:1, :]
                    c2 = c2 + ubuf_ref[par, slot, lt, pl.ds(p0 + halo - 1, piece), :] * cw[1:2, :]
                    return c2 + ubuf_ref[par, slot, lt, pl.ds(p0 + halo, piece), :] * cw[2:3, :]

                out_rows = slice(chunks[c].start + p0, chunks[c].start + p0 + piece)
                o_ref[out_rows, cols] = (_silu(conv(0)) * conv(1)).astype(o_ref.dtype)

    project(0)
    for c in range(1, len(chunks)):
        project(c)
        activate(c - 1)
    activate(len(chunks) - 1)
    last = len(chunks) - 1
    for slot in range(2):
        for lt in lane_tiles:
            carry_ref[j, slot, lt] = ubuf_ref[last % 2, slot, lt, size(last) : size(last) + halo, :]


def _ffn_up(xn, seq, w_up, conv_w, conv_b, cast, *, tm=2048, tf=256):
    T, D = xn.shape
    F = w_up.shape[1] // 2
    tm = min(tm, seq)
    assert F % tf == 0 and seq % tm == 0
    n_j = F // tf
    n_steps = (T // tm) * n_j
    n_blocks, src_spec, dst_spec, dst_shape = _cast_plan(cast, n_steps, lambda i, j: i * n_j + j)
    kern = functools.partial(_ffn_up_kernel, tiles_per_seq=seq // tm, cast_blocks=(n_blocks, n_steps))
    gate_col = lambda i, j: (0, j)
    value_col = lambda i, j: (0, n_j + j)
    return pl.pallas_call(
        kern,
        grid=(T // tm, n_j),
        in_specs=[
            pl.BlockSpec((tm, D), lambda i, j: (i, 0)),
            pl.BlockSpec((D, tf), gate_col),
            pl.BlockSpec((D, tf), value_col),
            pl.BlockSpec((CONV_WIDTH, tf), gate_col),
            pl.BlockSpec((CONV_WIDTH, tf), value_col),
            pl.BlockSpec((1, tf), gate_col),
            pl.BlockSpec((1, tf), value_col),
            src_spec,
        ],
        out_specs=[pl.BlockSpec((tm, tf), lambda i, j: (i, j)), dst_spec],
        out_shape=[jax.ShapeDtypeStruct((T, F), BF16), dst_shape],
        scratch_shapes=[
            pltpu.VMEM((2, 2, tf // LANES, SUBLANES + min(ROW_CHUNK, tm), LANES), F32),
            pltpu.VMEM((n_j, 2, tf // LANES, SUBLANES, LANES), F32),
        ],
        compiler_params=_cparams("arbitrary", "arbitrary"),
        name="ffn_up",
    )(xn, w_up, w_up, conv_w, conv_w, conv_b, conv_b, cast)


def _ffn_down_kernel(a_ref, w_ref, h_ref, o_ref):
    for rows in _row_chunks(o_ref.shape[0]):
        o_ref[rows, :] = h_ref[rows, :] + _dot(a_ref[rows, :], w_ref[...])


def _ffn_down(act, w_down, h2d, *, tm=512, tn=512):
    T, F = act.shape
    N = w_down.shape[1]
    tm, tn = min(tm, T), min(tn, N)
    return pl.pallas_call(
        _ffn_down_kernel,
        grid=(N // tn, T // tm),
        in_specs=[
            pl.BlockSpec((tm, F), lambda j, i: (i, 0)),
            pl.BlockSpec((F, tn), lambda j, i: (0, j)),
            pl.BlockSpec((tm, tn), lambda j, i: (i, j)),
        ],
        out_specs=pl.BlockSpec((tm, tn), lambda j, i: (i, j)),
        out_shape=jax.ShapeDtypeStruct((T, N), F32),
        compiler_params=_cparams("parallel", "arbitrary"),
        name="ffn_down",
    )(act, w_down, h2d)


def _rope_tables(S):
    inv_freq = ROPE_BASE ** (-jnp.linspace(0.0, 1.0, HEAD_DIM // 2, dtype=F32))
    ang = jnp.arange(S, dtype=F32)[:, None] * inv_freq[None, :]
    cos, sin = jnp.cos(ang), jnp.sin(ang)
    return jnp.concatenate([cos, cos], axis=-1), jnp.concatenate([-sin, sin], axis=-1)


def _retention_tables(H):
    C = RET_CHUNK
    log_g = jnp.log1p(-jnp.exp2(-5.0 - jnp.arange(H, dtype=F32)))
    idx = jnp.arange(C, dtype=F32)
    diff = idx[:, None] - idx[None, :]
    dint = jnp.where(diff >= 0, jnp.exp(jnp.maximum(diff, 0.0)[None] * log_g[:, None, None]), 0.0)
    kdec = jnp.exp((C - 1 - idx)[None, :] * log_g[:, None])
    qdec = jnp.exp((idx + 1.0)[None, :] * log_g[:, None])
    cdec = jnp.exp(C * log_g)
    lanes = lambda t: jnp.broadcast_to(t[..., None], t.shape + (HEAD_DIM,))
    return dint, lanes(kdec), lanes(qdec), lanes(cdec[:, None])


def _layer(h, mem, attn_norm, w_in, ret_norm, sb_q_norm, sb_k_norm, sb_out_norm, w_out, cross_norm, mem_norm,
           cross_w_q, cross_w_kv, cross_q_norm, cross_k_norm, cross_w_o, ffn_norm, ffn_w_up, ffn_conv_w,
           ffn_conv_b, ffn_w_down):
    B, S, D = h.shape
    T = B * S
    W = D // 2
    H = W // HEAD_DIM
    tn = min(1024, W)
    seg = W // tn
    x2d = h.reshape(T, D)

    xn = _rmsnorm_rows(x2d, attn_norm, BF16)
    w_in_b = w_in.astype(BF16)
    cos2, sin2 = _rope_tables(S)
    tm_in = min(1024, S)
    tiles_per_seq = S // tm_in
    rope_spec = pl.BlockSpec((tm_in, HEAD_DIM), lambda i, j: (i % tiles_per_seq, 0))
    ret_qk = _inproj(xn, w_in_b, B, S, n_col_tiles=2 * seg, col_map=lambda j: j, mode="rope", out_dtype=F32,
                     extra=(cos2, sin2), extra_specs=(rope_spec, rope_spec), n_scaled_tiles=seg, tn=tn,
                     name="inproj_ret_qk")
    v_all, w_up_b = _inproj(xn, w_in_b, B, S, n_col_tiles=2 * seg,
                            col_map=lambda j: jnp.where(j < seg, 2 * seg + j, 5 * seg + j), mode="plain",
                            out_dtype=BF16, tn=tn, name="inproj_v", cast=ffn_w_up)
    gate, w_out_b = _inproj(xn, w_in_b, B, S, n_col_tiles=seg, col_map=lambda j: 3 * seg + j, mode="plain",
                            out_dtype=F32, tn=tn, name="inproj_gate", cast=w_out)
    sb_gains = jnp.stack([sb_q_norm, sb_k_norm]).reshape(2, 1, HEAD_DIM)
    gain_spec = pl.BlockSpec((1, 1, HEAD_DIM), lambda i, j: (j // seg, 0, 0))
    sb_qk = _inproj(xn, w_in_b, B, S, n_col_tiles=2 * seg, col_map=lambda j: 4 * seg + j, mode="norm",
                    out_dtype=BF16, extra=(sb_gains,), extra_specs=(gain_spec,), tn=tn, name="inproj_sb_qk")

    ret = _retention(ret_qk, v_all, gate, _retention_tables(H), ret_norm)
    sb = _stick_breaking(sb_qk, v_all, sb_out_norm)
    h1 = _outproj(ret.reshape(T, W), sb.reshape(T, W), w_out_b, x2d)

    kmem, vmem = _memkv(mem, mem_norm, cross_w_kv.astype(BF16), cross_k_norm)
    h2, xn_ffn = _cross_attention(h1, S, cross_norm, cross_w_q.astype(BF16), cross_q_norm, kmem, vmem,
                                  cross_w_o.astype(BF16), ffn_norm)

    act, w_down_b = _ffn_up(xn_ffn, S, w_up_b, ffn_conv_w, ffn_conv_b.reshape(1, -1), ffn_w_down)
    out = _ffn_down(act, w_down_b, h2)
    return out.reshape(B, S, D)


def kernel(x, mem, attn_norm, w_in, ret_norm, sb_q_norm, sb_k_norm, sb_out_norm, w_out, cross_norm, mem_norm,
           cross_w_q, cross_w_kv, cross_q_norm, cross_k_norm, cross_w_o, ffn_norm, ffn_w_up, ffn_conv_w,
           ffn_conv_b, ffn_w_down):
    h = x
    for l in range(attn_norm.shape[0]):
        h = _layer(h, mem, attn_norm[l], w_in[l], ret_norm[l], sb_q_norm[l], sb_k_norm[l], sb_out_norm[l],
                   w_out[l], cross_norm[l], mem_norm[l], cross_w_q[l], cross_w_kv[l], cross_q_norm[l],
                   cross_k_norm[l], cross_w_o[l], ffn_norm[l], ffn_w_up[l], ffn_conv_w[l], ffn_conv_b[l],
                   ffn_w_down[l])
    return h
```

```python
import functools

import jax
import jax.numpy as jnp
from jax import lax
from jax.experimental import pallas as pl
from jax.experimental.pallas import tpu as pltpu

HEAD_DIM = 128
EPS = 1e-6
ROPE_BASE = 10000.0
RET_CHUNK = 128
CONV_WIDTH = 3
SUBLANES = 8
LANES = 128

VMEM_LIMIT_BYTES = 56 * 1024 * 1024

PROJ_TILE_ROWS = 1024
PROJ_TILE_COLS = 1024

ROW_CHUNK = 256

SB_EXP2_UNDERFLOW = 150.0
LOG2_E = 1.4426950408889634

F32 = jnp.float32
BF16 = jnp.bfloat16


def _cparams(*sem):
    return pltpu.CompilerParams(dimension_semantics=sem, vmem_limit_bytes=VMEM_LIMIT_BYTES)


def _rms(x, g):
    return x * lax.rsqrt(jnp.mean(x * x, axis=-1, keepdims=True) + EPS) * g


def _silu(x):
    return x * (1.0 / (1.0 + jnp.exp(-x)))


def _dot(a, b):
    return jnp.dot(a, b, preferred_element_type=F32)


def _dot_nt(a, b):
    return lax.dot_general(a, b, (((1,), (1,)), ((), ())), preferred_element_type=F32)


def _row_chunks(n_rows):
    rc = min(ROW_CHUNK, n_rows)
    return [slice(r0, r0 + rc) for r0 in range(0, n_rows, rc)]


def _rmsnorm_kernel(x_ref, g_ref, o_ref):
    o_ref[...] = _rms(x_ref[...], g_ref[...]).astype(o_ref.dtype)


def _rmsnorm_rows(x2d, g, out_dtype, tm=256):
    T, D = x2d.shape
    tm = min(tm, T)
    return pl.pallas_call(
        _rmsnorm_kernel,
        grid=(T // tm,),
        in_specs=[pl.BlockSpec((tm, D), lambda i: (i, 0)), pl.BlockSpec((1, D), lambda i: (0, 0))],
        out_specs=pl.BlockSpec((tm, D), lambda i: (i, 0)),
        out_shape=jax.ShapeDtypeStruct((T, D), out_dtype),
        compiler_params=_cparams("parallel"),
        name="rmsnorm_rows",
    )(x2d, g.reshape(1, D))


def _cast_block(src_ref, dst_ref, step, n_blocks, n_steps):
    if n_blocks == n_steps:
        dst_ref[...] = src_ref[...].astype(dst_ref.dtype)
    else:

        @pl.when(step < n_blocks)
        def _():
            dst_ref[...] = src_ref[...].astype(dst_ref.dtype)


def _cast_plan(w, n_steps, step_of):
    n_rows = w.shape[0]
    packed_rows = 16
    n_blocks = max(n for n in range(1, n_steps + 1) if n_rows % n == 0 and (n_rows // n) % packed_rows == 0)
    spec = lambda: pl.BlockSpec((n_rows // n_blocks, w.shape[1]),
                                lambda *g: (jnp.minimum(step_of(*g), n_blocks - 1), 0))
    return n_blocks, spec(), spec(), jax.ShapeDtypeStruct(w.shape, BF16)


def _inproj_kernel(a_ref, w_ref, *rest, mode, heads_per_tile, n_scaled_tiles, n_extra, cast_blocks):
    j = pl.program_id(1)
    extra = rest[:n_extra]
    if cast_blocks:
        src_ref, o_ref, dst_ref = rest[n_extra:]
        _cast_block(src_ref, dst_ref, pl.program_id(0) * pl.num_programs(1) + j, cast_blocks[0], cast_blocks[1])
    else:
        (o_ref,) = rest[n_extra:]
    if mode == "rope":
        scale = jnp.where(j < n_scaled_tiles, HEAD_DIM**-0.5, 1.0).astype(F32)
    elif mode == "norm":
        gain = extra[0][0]
    for rows in _row_chunks(a_ref.shape[0]):
        acc = _dot(a_ref[rows, :], w_ref[...])
        for hh in range(heads_per_tile):
            xh = acc[:, hh * HEAD_DIM : (hh + 1) * HEAD_DIM]
            if mode == "rope":
                xh = (xh * extra[0][rows, :] + pltpu.roll(xh, HEAD_DIM // 2, axis=1) * extra[1][rows, :]) * scale
            elif mode == "norm":
                xh = _rms(xh, gain)
            o_ref[0, hh, rows, :] = xh.astype(o_ref.dtype)


def _inproj(xn, w, batch, seq, *, n_col_tiles, col_map, mode, out_dtype, extra=(), extra_specs=(),
            n_scaled_tiles=0, tm=PROJ_TILE_ROWS, tn=PROJ_TILE_COLS, name, cast=None):
    T, D = xn.shape
    tm = min(tm, seq)
    hpt = tn // HEAD_DIM
    tiles_per_seq = seq // tm
    n_steps = (T // tm) * n_col_tiles
    in_specs = [
        pl.BlockSpec((tm, D), lambda i, j: (i, 0)),
        pl.BlockSpec((D, tn), lambda i, j: (0, col_map(j))),
        *extra_specs,
    ]
    out_specs = pl.BlockSpec((1, hpt, tm, HEAD_DIM), lambda i, j: (i // tiles_per_seq, j, i % tiles_per_seq, 0))
    out_shape = jax.ShapeDtypeStruct((batch, n_col_tiles * hpt, seq, HEAD_DIM), out_dtype)
    operands = [xn, w, *extra]
    cast_blocks = None
    if cast is not None:
        n_blocks, src_spec, dst_spec, dst_shape = _cast_plan(cast, n_steps, lambda i, j: i * n_col_tiles + j)
        in_specs.append(src_spec)
        out_specs, out_shape = [out_specs, dst_spec], [out_shape, dst_shape]
        operands.append(cast)
        cast_blocks = (n_blocks, n_steps)
    kern = functools.partial(_inproj_kernel, mode=mode, heads_per_tile=hpt, n_scaled_tiles=n_scaled_tiles,
                             n_extra=len(extra), cast_blocks=cast_blocks)
    return pl.pallas_call(
        kern,
        grid=(T // tm, n_col_tiles),
        in_specs=in_specs,
        out_specs=out_specs,
        out_shape=out_shape,
        compiler_params=_cparams("arbitrary", "arbitrary"),
        name=name,
    )(*operands)


def _retention_kernel(q_ref, k_ref, v_ref, g_ref, dint_ref, kdec_ref, qdec_ref, cdec_ref, gn_ref, o_ref,
                      state_ref, *, n_chunks):
    @pl.when(pl.program_id(2) == 0)
    def _():
        state_ref[...] = jnp.zeros_like(state_ref)

    dint = dint_ref[0]
    kdec = kdec_ref[0]
    qdec = qdec_ref[0]
    cdec = cdec_ref[0]
    gn = gn_ref[...]
    C = RET_CHUNK
    chunks = range(n_chunks)
    rows = [pl.ds(n * C, C) for n in chunks]
    q = [q_ref[0, 0, r, :] for r in rows]
    k = [k_ref[0, 0, r, :] for r in rows]
    scores = [_dot_nt(q[n].astype(BF16), k[n].astype(BF16)) for n in chunks]
    kv = [_dot((k[n] * kdec).T.astype(BF16), v_ref[0, 0, rows[n], :]) for n in chunks]
    states = [state_ref[...]]
    for n in chunks:
        states.append(states[n] * cdec + kv[n])
    state_ref[...] = states[n_chunks]
    out = [
        _dot(
            jnp.concatenate([(scores[n] * dint).astype(BF16), (q[n] * qdec).astype(BF16)], axis=1),
            jnp.concatenate([v_ref[0, 0, rows[n], :], states[n].astype(BF16)], axis=0),
        )
        for n in chunks
    ]
    for n in chunks:
        y = _rms(out[n], gn) * _silu(g_ref[0, 0, rows[n], :])
        o_ref[0, rows[n], :] = y.astype(o_ref.dtype)


def _retention(qk, v_all, gate, tables, gn, *, rows_per_step=8192):
    B, H2, S, d = qk.shape
    H = H2 // 2
    tr = min(rows_per_step, S)
    dint, kdec, qdec, cdec = tables
    C = RET_CHUNK
    kern = functools.partial(_retention_kernel, n_chunks=tr // C)
    head_blk = (1, 1, tr, d)
    return pl.pallas_call(
        kern,
        grid=(B, H, S // tr),
        in_specs=[
            pl.BlockSpec(head_blk, lambda b, h, r: (b, h, r, 0)),
            pl.BlockSpec(head_blk, lambda b, h, r: (b, H + h, r, 0)),
            pl.BlockSpec(head_blk, lambda b, h, r: (b, h, r, 0)),
            pl.BlockSpec(head_blk, lambda b, h, r: (b, h, r, 0)),
            pl.BlockSpec((1, C, C), lambda b, h, r: (h, 0, 0)),
            pl.BlockSpec((1, C, d), lambda b, h, r: (h, 0, 0)),
            pl.BlockSpec((1, C, d), lambda b, h, r: (h, 0, 0)),
            pl.BlockSpec((1, 1, d), lambda b, h, r: (h, 0, 0)),
            pl.BlockSpec((1, d), lambda b, h, r: (0, 0)),
        ],
        out_specs=pl.BlockSpec((1, tr, d), lambda b, h, r: (b, r, h)),
        out_shape=jax.ShapeDtypeStruct((B, S, H * d), BF16),
        scratch_shapes=[pltpu.VMEM((d, d), F32)],
        compiler_params=_cparams("parallel", "parallel", "arbitrary"),
        name="retention",
    )(qk, qk, v_all, gate, dint, kdec, qdec, cdec, gn.reshape(1, d))


def _sb_kernel(q_ref, k_ref, v_ref, w2_ref, gn_ref, o_ref, *, tq, tk, group, n_sub):
    scale = HEAD_DIM**-0.5
    w2 = w2_ref[...]
    gn = gn_ref[...]
    heads = range(group)

    def guard(unit, x):
        mask = unit[4]
        if mask is not None:
            head = [] if x.shape[1] == tk else [x[:, : x.shape[1] - tk]]
            x = jnp.concatenate(head + [jnp.where(mask, x[:, x.shape[1] - tk :], 0.0)], axis=1)
        return x

    def key_rows(unit):
        return pl.ds(unit[2], unit[3] * tk)

    def scan_keys(units):
        newest = lambda x: x[:, x.shape[1] - tk :]
        z2 = [_dot_nt(u[1], k_ref[0, u[0], key_rows(u), :]) * (scale * LOG2_E) for u in units]
        hi, lo = [], []
        for u, z in zip(units, z2):
            sp = guard(u, jnp.maximum(z, 0.0) + jnp.log2(1.0 + jnp.exp2(-jnp.abs(z))))
            h = sp.astype(BF16)
            hi.append(h)
            lo.append((sp - h.astype(F32)).astype(BF16))
        cr_new = [_dot(jnp.concatenate([newest(h), newest(l)], axis=1), w2) for h, l in zip(hi, lo)]
        cr_old = [_dot(jnp.concatenate([h[:, :tk], l[:, :tk]], axis=1), w2) if u[3] == 2 else None
                  for u, h, l in zip(units, hi, lo)]
        return list(zip(z2, cr_new, cr_old))

    def weigh(units, scans, runs, accs):
        a, new_runs = [], []
        for u, (z2, cr_new, cr_old), run in zip(units, scans, runs):
            run_mid = run + cr_new[:, tk:]
            if cr_old is not None:
                behind = jnp.concatenate([cr_old[:, :tk] + run_mid, cr_new[:, :tk] + run], axis=1)
                new_runs.append(run_mid + cr_old[:, tk:])
            else:
                behind = cr_new[:, :tk] + run
                new_runs.append(run_mid)
            a.append(guard(u, jnp.exp2(z2 - behind)).astype(BF16))
        new_accs = []
        for u, w, acc in zip(units, a, accs):
            out = _dot(w, v_ref[0, u[0], key_rows(u), :])
            new_accs.append(acc + (out if u[5] is None else jnp.where(u[5], out, 0.0)))
        return new_runs, new_accs

    mask = lax.broadcasted_iota(jnp.int32, (tk, tk), 1) < lax.broadcasted_iota(jnp.int32, (tk, tk), 0)

    def query_block(sub, carry):
        qi = pl.program_id(2) * n_sub + sub
        row0 = pl.multiple_of(sub * tq, tq)
        q_rows = lambda g, start, size: q_ref[0, g, pl.ds(pl.multiple_of(row0 + start, tk), size), :]
        q0 = pl.multiple_of(qi * tq, tq)
        diag = [(g, q_rows(g, half * tk, tk), q0, half + 1, mask, None) for g in heads for half in range(2)]
        prev_start = pl.multiple_of(jnp.maximum(qi - 1, 0) * tq, tq)
        prev = [(g, q_rows(g, 0, tq), prev_start, 2, None, qi > 0) for g in heads]
        scans = scan_keys(diag + prev)
        zero_run, zero_acc = jnp.zeros((tk, tk), F32), jnp.zeros((tk, HEAD_DIM), F32)
        runs, accs = weigh(diag, scans[: len(diag)], [zero_run] * len(diag), [zero_acc] * len(diag))
        runs = [jnp.concatenate([runs[2 * g], runs[2 * g + 1]], axis=0) for g in heads]
        accs = [jnp.concatenate([accs[2 * g], accs[2 * g + 1]], axis=0) for g in heads]
        runs, accs = weigh(prev, scans[len(diag) :], runs, accs)

        def cond(c):
            return jnp.logical_and(c[0] >= 0, jnp.min(functools.reduce(jnp.minimum, c[1])) < SB_EXP2_UNDERFLOW)

        def body(c):
            kblk, runs, accs = c
            kstart = pl.multiple_of(kblk * tq, tq)
            units = [(g, q_rows(g, 0, tq), kstart, 2, None, None) for g in heads]
            runs, accs = weigh(units, scan_keys(units), runs, accs)
            return kblk - 1, tuple(runs), tuple(accs)

        _, _, accs = lax.while_loop(cond, body, (qi - 2, tuple(runs), tuple(accs)))
        for g in heads:
            o_ref[0, pl.ds(row0, tq), g * HEAD_DIM : (g + 1) * HEAD_DIM] = _rms(accs[g], gn).astype(o_ref.dtype)
        return carry

    lax.fori_loop(0, n_sub, query_block, 0)


def _stick_breaking(sqk, v_all, gn, *, tk=128, group=4, blocks_per_step=8):
    B, H2, S, d = sqk.shape
    H = H2 // 2
    tq = 2 * tk
    group = min(group, H)
    n_groups = H // group
    idx = jnp.arange(tk)
    w2 = jnp.concatenate([(idx[:, None] >= idx[None, :]).astype(BF16), jnp.ones((tk, tk), BF16)], axis=1)
    w2 = jnp.concatenate([w2, w2], axis=0)
    n_sub = min(blocks_per_step, S // tq)
    kern = functools.partial(_sb_kernel, tq=tq, tk=tk, group=group, n_sub=n_sub)
    kv_spec = lambda: pl.BlockSpec((1, group, S, d), lambda b, h, qi: (b, n_groups + h, 0, 0))
    return pl.pallas_call(
        kern,
        grid=(B, n_groups, S // (tq * n_sub)),
        in_specs=[
            pl.BlockSpec((1, group, tq * n_sub, d), lambda b, h, qi: (b, h, qi, 0)),
            kv_spec(),
            kv_spec(),
            pl.BlockSpec((2 * tk, 2 * tk), lambda b, h, qi: (0, 0)),
            pl.BlockSpec((1, d), lambda b, h, qi: (0, 0)),
        ],
        out_specs=pl.BlockSpec((1, tq * n_sub, group * d), lambda b, h, qi: (b, qi, h)),
        out_shape=jax.ShapeDtypeStruct((B, S, H * d), BF16),
        compiler_params=_cparams("parallel", "parallel", "arbitrary"),
        name="stick_breaking",
    )(sqk, sqk, v_all, w2, gn.reshape(1, d))


def _outproj_kernel(a1_ref, a2_ref, w1_ref, w2_ref, x_ref, o_ref):
    for rows in _row_chunks(o_ref.shape[0]):
        acc = _dot(a1_ref[rows, :], w1_ref[...]) + _dot(a2_ref[rows, :], w2_ref[...])
        o_ref[rows, :] = x_ref[rows, :] + acc


def _outproj(ret2d, sb2d, w_out, x2d, *, tm=PROJ_TILE_ROWS, tn=PROJ_TILE_COLS):
    T, K = ret2d.shape
    N = w_out.shape[1]
    tm, tn = min(tm, T), min(tn, N)
    return pl.pallas_call(
        _outproj_kernel,
        grid=(T // tm, N // tn),
        in_specs=[
            pl.BlockSpec((tm, K), lambda i, j: (i, 0)),
            pl.BlockSpec((tm, K), lambda i, j: (i, 0)),
            pl.BlockSpec((K, tn), lambda i, j: (0, j)),
            pl.BlockSpec((K, tn), lambda i, j: (1, j)),
            pl.BlockSpec((tm, tn), lambda i, j: (i, j)),
        ],
        out_specs=pl.BlockSpec((tm, tn), lambda i, j: (i, j)),
        out_shape=jax.ShapeDtypeStruct((T, N), F32),
        compiler_params=_cparams("parallel", "arbitrary"),
        name="outproj",
    )(ret2d, sb2d, w_out, w_out, x2d)


def _memkv_kernel(m_ref, gm_ref, w_ref, gk_ref, k_ref, v_ref):
    mn = _rms(m_ref[0], gm_ref[...]).astype(BF16)
    kv = _dot(mn, w_ref[...])
    cw = kv.shape[1] // 2
    gk = gk_ref[...]
    for hh in range(cw // HEAD_DIM):
        sl = slice(hh * HEAD_DIM, (hh + 1) * HEAD_DIM)
        k_ref[0, :, sl] = _rms(kv[:, sl], gk).astype(k_ref.dtype)
    v_ref[0] = kv[:, cw:].astype(v_ref.dtype)


def _memkv(mem, g_mem, w_kv, g_k):
    B, M, D = mem.shape
    cw = w_kv.shape[1] // 2
    return pl.pallas_call(
        _memkv_kernel,
        grid=(B,),
        in_specs=[
            pl.BlockSpec((1, M, D), lambda b: (b, 0, 0)),
            pl.BlockSpec((1, D), lambda b: (0, 0)),
            pl.BlockSpec((D, 2 * cw), lambda b: (0, 0)),
            pl.BlockSpec((1, HEAD_DIM), lambda b: (0, 0)),
        ],
        out_specs=[pl.BlockSpec((1, M, cw), lambda b: (b, 0, 0))] * 2,
        out_shape=[jax.ShapeDtypeStruct((B, M, cw), BF16)] * 2,
        compiler_params=_cparams("parallel"),
        name="memkv",
    )(mem, g_mem.reshape(1, D), w_kv, g_k.reshape(1, HEAD_DIM))


def _cross_kernel(h_ref, gc_ref, wq_ref, gq_ref, k_ref, v_ref, wo_ref, gf_ref, h2_ref, xn_ref):
    n_rows = h_ref.shape[0]
    part_rows = min(LANES, n_rows)
    parts = [slice(r0, r0 + part_rows) for r0 in range(0, n_rows, part_rows)]
    gq = gq_ref[...]
    scale = HEAD_DIM**-0.5
    h = [h_ref[rows, :] for rows in parts]
    xn = [_rms(hp, gc_ref[...]).astype(BF16) for hp in h]
    q = [_dot(x, wq_ref[...]) for x in xn]
    head_cols = [slice(c0, c0 + HEAD_DIM) for c0 in range(0, wq_ref.shape[1], HEAD_DIM)]
    s = [[_dot_nt(_rms(qp[:, sl], gq).astype(BF16), k_ref[0, :, sl]) * scale for sl in head_cols] for qp in q]
    o = []
    for sp in s:
        heads_out = []
        for sh, sl in zip(sp, head_cols):
            e = jnp.exp(sh - jnp.max(sh, axis=-1, keepdims=True))
            p = e / jnp.sum(e, axis=-1, keepdims=True)
            heads_out.append(_dot(p.astype(BF16), v_ref[0, :, sl]).astype(BF16))
        o.append(jnp.concatenate(heads_out, axis=-1))
    h2 = [hp + _dot(op, wo_ref[...]) for hp, op in zip(h, o)]
    for rows, h2p in zip(parts, h2):
        h2_ref[rows, :] = h2p
        xn_ref[rows, :] = _rms(h2p, gf_ref[...]).astype(xn_ref.dtype)


def _cross_attention(h2d, seq, g_cross, w_q, g_q, kmem, vmem, w_o, g_ffn, *, tm=256):
    T, D = h2d.shape
    _, M, cw = kmem.shape
    tm = min(tm, seq)
    tiles_per_seq = seq // tm
    const = lambda i: (0, 0)
    return pl.pallas_call(
        _cross_kernel,
        grid=(T // tm,),
        in_specs=[
            pl.BlockSpec((tm, D), lambda i: (i, 0)),
            pl.BlockSpec((1, D), const),
            pl.BlockSpec((D, cw), const),
            pl.BlockSpec((1, HEAD_DIM), const),
            pl.BlockSpec((1, M, cw), lambda i: (i // tiles_per_seq, 0, 0)),
            pl.BlockSpec((1, M, cw), lambda i: (i // tiles_per_seq, 0, 0)),
            pl.BlockSpec((cw, D), const),
            pl.BlockSpec((1, D), const),
        ],
        out_specs=[pl.BlockSpec((tm, D), lambda i: (i, 0))] * 2,
        out_shape=[jax.ShapeDtypeStruct((T, D), F32), jax.ShapeDtypeStruct((T, D), BF16)],
        compiler_params=_cparams("parallel"),
        name="cross_attention",
    )(h2d, g_cross.reshape(1, D), w_q, g_q.reshape(1, HEAD_DIM), kmem, vmem, w_o, g_ffn.reshape(1, D))


CONV_PIECE_ROWS = 64


def _ffn_up_kernel(a_ref, wg_ref, wv_ref, cwg_ref, cwv_ref, cbg_ref, cbv_ref, src_ref, o_ref, dst_ref, ubuf_ref,
                   carry_ref, *, tiles_per_seq, cast_blocks):
    i = pl.program_id(0)
    j = pl.program_id(1)

    @pl.when(jnp.logical_and(i == 0, j == 0))
    def _():
        carry_ref[...] = jnp.zeros_like(carry_ref)

    _cast_block(src_ref, dst_ref, i * pl.num_programs(1) + j, cast_blocks[0], cast_blocks[1])
    seq_start = (i % tiles_per_seq) == 0
    halo = SUBLANES
    chunks = _row_chunks(a_ref.shape[0])
    size = lambda c: chunks[c].stop - chunks[c].start
    lane_tiles = range(o_ref.shape[1] // LANES)
    projections = ((wg_ref, cwg_ref, cbg_ref), (wv_ref, cwv_ref, cbv_ref))

    def project(c):
        par = c % 2
        a = a_ref[chunks[c], :]
        for slot, (w_ref, _, _) in enumerate(projections):
            u = _dot(a, w_ref[...])
            for lt in lane_tiles:
                if c == 0:
                    ubuf_ref[par, slot, lt, 0:halo, :] = jnp.where(seq_start, 0.0, carry_ref[j, slot, lt])
                else:
                    ubuf_ref[par, slot, lt, 0:halo, :] = ubuf_ref[1 - par, slot, lt, size(c - 1) : size(c - 1) + halo, :]
                ubuf_ref[par, slot, lt, halo : halo + size(c), :] = u[:, lt * LANES : (lt + 1) * LANES]

    def activate(c):
        par = c % 2
        piece = min(CONV_PIECE_ROWS, size(c))
        for lt in lane_tiles:
            cols = slice(lt * LANES, (lt + 1) * LANES)
            for p0 in range(0, size(c), piece):

                def conv(slot):
                    _, cw_ref, cb_ref = projections[slot]
                    cw = cw_ref[:, cols]
                    c2 = cb_ref[:, cols] + ubuf_ref[par, slot, lt, pl.ds(p0 + halo - 2, piece), :] * cw[0:1, :]
                    c2 = c2 + ubuf_ref[par, slot, lt, pl.ds(p0 + halo - 1, piece), :] * cw[1:2, :]
                    return c2 + ubuf_ref[par, slot, lt, pl.ds(p0 + halo, piece), :] * cw[2:3, :]

                out_rows = slice(chunks[c].start + p0, chunks[c].start + p0 + piece)
                o_ref[out_rows, cols] = (_silu(conv(0)) * conv(1)).astype(o_ref.dtype)

    project(0)
    for c in range(1, len(chunks)):
        project(c)
        activate(c - 1)
    activate(len(chunks) - 1)
    last = len(chunks) - 1
    for slot in range(2):
        for lt in lane_tiles:
            carry_ref[j, slot, lt] = ubuf_ref[last % 2, slot, lt, size(last) : size(last) + halo, :]


def _ffn_up(xn, seq, w_up, conv_w, conv_b, cast, *, tm=2048, tf=256):
    T, D = xn.shape
    F = w_up.shape[1] // 2
    tm = min(tm, seq)
    assert F % tf == 0 and seq % tm == 0
    n_j = F // tf
    n_steps = (T // tm) * n_j
    n_blocks, src_spec, dst_spec, dst_shape = _cast_plan(cast, n_steps, lambda i, j: i * n_j + j)
    kern = functools.partial(_ffn_up_kernel, tiles_per_seq=seq // tm, cast_blocks=(n_blocks, n_steps))
    gate_col = lambda i, j: (0, j)
    value_col = lambda i, j: (0, n_j + j)
    return pl.pallas_call(
        kern,
        grid=(T // tm, n_j),
        in_specs=[
            pl.BlockSpec((tm, D), lambda i, j: (i, 0)),
            pl.BlockSpec((D, tf), gate_col),
            pl.BlockSpec((D, tf), value_col),
            pl.BlockSpec((CONV_WIDTH, tf), gate_col),
            pl.BlockSpec((CONV_WIDTH, tf), value_col),
            pl.BlockSpec((1, tf), gate_col),
            pl.BlockSpec((1, tf), value_col),
            src_spec,
        ],
        out_specs=[pl.BlockSpec((tm, tf), lambda i, j: (i, j)), dst_spec],
        out_shape=[jax.ShapeDtypeStruct((T, F), BF16), dst_shape],
        scratch_shapes=[
            pltpu.VMEM((2, 2, tf // LANES, SUBLANES + min(ROW_CHUNK, tm), LANES), F32),
            pltpu.VMEM((n_j, 2, tf // LANES, SUBLANES, LANES), F32),
        ],
        compiler_params=_cparams("arbitrary", "arbitrary"),
        name="ffn_up",
    )(xn, w_up, w_up, conv_w, conv_w, conv_b, conv_b, cast)


def _ffn_down_kernel(a_ref, w_ref, h_ref, o_ref):
    for rows in _row_chunks(o_ref.shape[0]):
        o_ref[rows, :] = h_ref[rows, :] + _dot(a_ref[rows, :], w_ref[...])


def _ffn_down(act, w_down, h2d, *, tm=512, tn=512):
    T, F = act.shape
    N = w_down.shape[1]
    tm, tn = min(tm, T), min(tn, N)
    return pl.pallas_call(
        _ffn_down_kernel,
        grid=(N // tn, T // tm),
        in_specs=[
            pl.BlockSpec((tm, F), lambda j, i: (i, 0)),
            pl.BlockSpec((F, tn), lambda j, i: (0, j)),
            pl.BlockSpec((tm, tn), lambda j, i: (i, j)),
        ],
        out_specs=pl.BlockSpec((tm, tn), lambda j, i: (i, j)),
        out_shape=jax.ShapeDtypeStruct((T, N), F32),
        compiler_params=_cparams("parallel", "arbitrary"),
        name="ffn_down",
    )(act, w_down, h2d)


def _rope_tables(S):
    inv_freq = ROPE_BASE ** (-jnp.linspace(0.0, 1.0, HEAD_DIM // 2, dtype=F32))
    ang = jnp.arange(S, dtype=F32)[:, None] * inv_freq[None, :]
    cos, sin = jnp.cos(ang), jnp.sin(ang)
    return jnp.concatenate([cos, cos], axis=-1), jnp.concatenate([-sin, sin], axis=-1)


def _retention_tables(H):
    C = RET_CHUNK
    log_g = jnp.log1p(-jnp.exp2(-5.0 - jnp.arange(H, dtype=F32)))
    idx = jnp.arange(C, dtype=F32)
    diff = idx[:, None] - idx[None, :]
    dint = jnp.where(diff >= 0, jnp.exp(jnp.maximum(diff, 0.0)[None] * log_g[:, None, None]), 0.0)
    kdec = jnp.exp((C - 1 - idx)[None, :] * log_g[:, None])
    qdec = jnp.exp((idx + 1.0)[None, :] * log_g[:, None])
    cdec = jnp.exp(C * log_g)
    lanes = lambda t: jnp.broadcast_to(t[..., None], t.shape + (HEAD_DIM,))
    return dint, lanes(kdec), lanes(qdec), lanes(cdec[:, None])


def _layer(h, mem, attn_norm, w_in, ret_norm, sb_q_norm, sb_k_norm, sb_out_norm, w_out, cross_norm, mem_norm,
           cross_w_q, cross_w_kv, cross_q_norm, cross_k_norm, cross_w_o, ffn_norm, ffn_w_up, ffn_conv_w,
           ffn_conv_b, ffn_w_down):
    B, S, D = h.shape
    T = B * S
    W = D // 2
    H = W // HEAD_DIM
    tn = min(PROJ_TILE_COLS, W)
    seg = W // tn
    x2d = h.reshape(T, D)

    xn = _rmsnorm_rows(x2d, attn_norm, BF16)
    w_in_b = w_in.astype(BF16)
    cos2, sin2 = _rope_tables(S)
    tm_in = min(PROJ_TILE_ROWS, S)
    tiles_per_seq = S // tm_in
    rope_spec = pl.BlockSpec((tm_in, HEAD_DIM), lambda i, j: (i % tiles_per_seq, 0))
    ret_qk = _inproj(xn, w_in_b, B, S, n_col_tiles=2 * seg, col_map=lambda j: j, mode="rope", out_dtype=F32,
                     extra=(cos2, sin2), extra_specs=(rope_spec, rope_spec), n_scaled_tiles=seg, tn=tn,
                     name="inproj_ret_qk")
    v_all, w_up_b = _inproj(xn, w_in_b, B, S, n_col_tiles=2 * seg,
                            col_map=lambda j: jnp.where(j < seg, 2 * seg + j, 5 * seg + j), mode="plain",
                            out_dtype=BF16, tn=tn, name="inproj_v", cast=ffn_w_up)
    gate, w_out_b = _inproj(xn, w_in_b, B, S, n_col_tiles=seg, col_map=lambda j: 3 * seg + j, mode="plain",
                            out_dtype=F32, tn=tn, name="inproj_gate", cast=w_out)
    sb_gains = jnp.stack([sb_q_norm, sb_k_norm]).reshape(2, 1, HEAD_DIM)
    gain_spec = pl.BlockSpec((1, 1, HEAD_DIM), lambda i, j: (j // seg, 0, 0))
    sb_qk = _inproj(xn, w_in_b, B, S, n_col_tiles=2 * seg, col_map=lambda j: 4 * seg + j, mode="norm",
                    out_dtype=BF16, extra=(sb_gains,), extra_specs=(gain_spec,), tn=tn, name="inproj_sb_qk")

    ret = _retention(ret_qk, v_all, gate, _retention_tables(H), ret_norm)
    sb = _stick_breaking(sb_qk, v_all, sb_out_norm)
    h1 = _outproj(ret.reshape(T, W), sb.reshape(T, W), w_out_b, x2d)

    kmem, vmem = _memkv(mem, mem_norm, cross_w_kv.astype(BF16), cross_k_norm)
    h2, xn_ffn = _cross_attention(h1, S, cross_norm, cross_w_q.astype(BF16), cross_q_norm, kmem, vmem,
                                  cross_w_o.astype(BF16), ffn_norm)

    act, w_down_b = _ffn_up(xn_ffn, S, w_up_b, ffn_conv_w, ffn_conv_b.reshape(1, -1), ffn_w_down)
    out = _ffn_down(act, w_down_b, h2)
    return out.reshape(B, S, D)


def kernel(x, mem, attn_norm, w_in, ret_norm, sb_q_norm, sb_k_norm, sb_out_norm, w_out, cross_norm, mem_norm,
           cross_w_q, cross_w_kv, cross_q_norm, cross_k_norm, cross_w_o, ffn_norm, ffn_w_up, ffn_conv_w,
           ffn_conv_b, ffn_w_down):
    h = x
    for l in range(attn_norm.shape[0]):
        h = _layer(h, mem, attn_norm[l], w_in[l], ret_norm[l], sb_q_norm[l], sb_k_norm[l], sb_out_norm[l],
                   w_out[l], cross_norm[l], mem_norm[l], cross_w_q[l], cross_w_kv[l], cross_q_norm[l],
                   cross_k_norm[l], cross_w_o[l], ffn_norm[l], ffn_w_up[l], ffn_conv_w[l], ffn_conv_b[l],
                   ffn_w_down[l])
    return h
```

```python
import functools

import jax
import jax.numpy as jnp
from jax import lax
from jax.experimental import pallas as pl
from jax.experimental.pallas import tpu as pltpu

HEAD_DIM = 128
EPS = 1e-6
ROPE_BASE = 10000.0
RET_CHUNK = 128
CONV_WIDTH = 3
SUBLANES = 8
LANES = 128

VMEM_LIMIT_BYTES = 56 * 1024 * 1024

PROJ_TILE_ROWS = 1024
PROJ_TILE_COLS = 1024

ROW_CHUNK = 256

SB_EXP2_UNDERFLOW = 150.0
LOG2_E = 1.4426950408889634

F32 = jnp.float32
BF16 = jnp.bfloat16


def _cparams(*sem):
    return pltpu.CompilerParams(dimension_semantics=sem, vmem_limit_bytes=VMEM_LIMIT_BYTES)


def _rms(x, g):
    return x * lax.rsqrt(jnp.mean(x * x, axis=-1, keepdims=True) + EPS) * g


def _silu(x):
    return x * (1.0 / (1.0 + jnp.exp(-x)))


def _dot(a, b):
    return jnp.dot(a, b, preferred_element_type=F32)


def _dot_nt(a, b):
    return lax.dot_general(a, b, (((1,), (1,)), ((), ())), preferred_element_type=F32)


def _row_chunks(n_rows):
    rc = min(ROW_CHUNK, n_rows)
    return [slice(r0, r0 + rc) for r0 in range(0, n_rows, rc)]


def _rmsnorm_kernel(x_ref, g_ref, o_ref):
    o_ref[...] = _rms(x_ref[...], g_ref[...]).astype(o_ref.dtype)


def _rmsnorm_rows(x2d, g, out_dtype, tm=512):
    T, D = x2d.shape
    tm = min(tm, T)
    return pl.pallas_call(
        _rmsnorm_kernel,
        grid=(T // tm,),
        in_specs=[pl.BlockSpec((tm, D), lambda i: (i, 0)), pl.BlockSpec((1, D), lambda i: (0, 0))],
        out_specs=pl.BlockSpec((tm, D), lambda i: (i, 0)),
        out_shape=jax.ShapeDtypeStruct((T, D), out_dtype),
        compiler_params=_cparams("parallel"),
        name="rmsnorm_rows",
    )(x2d, g.reshape(1, D))


def _cast_block(src_ref, dst_ref, step, n_blocks, n_steps):
    if n_blocks == n_steps:
        dst_ref[...] = src_ref[...].astype(dst_ref.dtype)
    else:

        @pl.when(step < n_blocks)
        def _():
            dst_ref[...] = src_ref[...].astype(dst_ref.dtype)


def _cast_plan(w, n_steps, step_of):
    n_rows = w.shape[0]
    packed_rows = 16
    n_blocks = max(n for n in range(1, n_steps + 1) if n_rows % n == 0 and (n_rows // n) % packed_rows == 0)
    spec = lambda: pl.BlockSpec((n_rows // n_blocks, w.shape[1]),
                                lambda *g: (jnp.minimum(step_of(*g), n_blocks - 1), 0))
    return n_blocks, spec(), spec(), jax.ShapeDtypeStruct(w.shape, BF16)


def _inproj_kernel(a_ref, w_ref, *rest, mode, heads_per_tile, n_scaled_tiles, n_extra, cast_blocks):
    j = pl.program_id(1)
    extra = rest[:n_extra]
    if cast_blocks:
        src_ref, o_ref, dst_ref = rest[n_extra:]
        _cast_block(src_ref, dst_ref, pl.program_id(0) * pl.num_programs(1) + j, cast_blocks[0], cast_blocks[1])
    else:
        (o_ref,) = rest[n_extra:]
    if mode == "rope":
        scale = jnp.where(j < n_scaled_tiles, HEAD_DIM**-0.5, 1.0).astype(F32)
    elif mode == "norm":
        gain = extra[0][0]
    for rows in _row_chunks(a_ref.shape[0]):
        acc = _dot(a_ref[rows, :], w_ref[...])
        for hh in range(heads_per_tile):
            xh = acc[:, hh * HEAD_DIM : (hh + 1) * HEAD_DIM]
            if mode == "rope":
                xh = (xh * extra[0][rows, :] + pltpu.roll(xh, HEAD_DIM // 2, axis=1) * extra[1][rows, :]) * scale
            elif mode == "norm":
                xh = _rms(xh, gain)
            o_ref[0, hh, rows, :] = xh.astype(o_ref.dtype)


def _inproj(xn, w, batch, seq, *, n_col_tiles, col_map, mode, out_dtype, extra=(), extra_specs=(),
            n_scaled_tiles=0, tm=PROJ_TILE_ROWS, tn=PROJ_TILE_COLS, name, cast=None):
    T, D = xn.shape
    tm = min(tm, seq)
    hpt = tn // HEAD_DIM
    tiles_per_seq = seq // tm
    n_steps = (T // tm) * n_col_tiles
    in_specs = [
        pl.BlockSpec((tm, D), lambda i, j: (i, 0)),
        pl.BlockSpec((D, tn), lambda i, j: (0, col_map(j))),
        *extra_specs,
    ]
    out_specs = pl.BlockSpec((1, hpt, tm, HEAD_DIM), lambda i, j: (i // tiles_per_seq, j, i % tiles_per_seq, 0))
    out_shape = jax.ShapeDtypeStruct((batch, n_col_tiles * hpt, seq, HEAD_DIM), out_dtype)
    operands = [xn, w, *extra]
    cast_blocks = None
    if cast is not None:
        n_blocks, src_spec, dst_spec, dst_shape = _cast_plan(cast, n_steps, lambda i, j: i * n_col_tiles + j)
        in_specs.append(src_spec)
        out_specs, out_shape = [out_specs, dst_spec], [out_shape, dst_shape]
        operands.append(cast)
        cast_blocks = (n_blocks, n_steps)
    kern = functools.partial(_inproj_kernel, mode=mode, heads_per_tile=hpt, n_scaled_tiles=n_scaled_tiles,
                             n_extra=len(extra), cast_blocks=cast_blocks)
    return pl.pallas_call(
        kern,
        grid=(T // tm, n_col_tiles),
        in_specs=in_specs,
        out_specs=out_specs,
        out_shape=out_shape,
        compiler_params=_cparams("arbitrary", "arbitrary"),
        name=name,
    )(*operands)


def _retention_kernel(q_ref, k_ref, v_ref, g_ref, dint_ref, kdec_ref, qdec_ref, cdec_ref, gn_ref, o_ref,
                      state_ref, *, n_chunks):
    @pl.when(pl.program_id(2) == 0)
    def _():
        state_ref[...] = jnp.zeros_like(state_ref)

    dint = dint_ref[0]
    kdec = kdec_ref[0]
    qdec = qdec_ref[0]
    cdec = cdec_ref[0]
    gn = gn_ref[...]
    C = RET_CHUNK
    chunks = range(n_chunks)
    rows = [pl.ds(n * C, C) for n in chunks]
    q = [q_ref[0, 0, r, :] for r in rows]
    k = [k_ref[0, 0, r, :] for r in rows]
    scores = [_dot_nt(q[n].astype(BF16), k[n].astype(BF16)) for n in chunks]
    kv = [_dot((k[n] * kdec).T.astype(BF16), v_ref[0, 0, rows[n], :]) for n in chunks]
    states = [state_ref[...]]
    for n in chunks:
        states.append(states[n] * cdec + kv[n])
    state_ref[...] = states[n_chunks]
    out = [
        _dot(
            jnp.concatenate([(scores[n] * dint).astype(BF16), (q[n] * qdec).astype(BF16)], axis=1),
            jnp.concatenate([v_ref[0, 0, rows[n], :], states[n].astype(BF16)], axis=0),
        )
        for n in chunks
    ]
    for n in chunks:
        y = _rms(out[n], gn) * _silu(g_ref[0, 0, rows[n], :])
        o_ref[0, rows[n], :] = y.astype(o_ref.dtype)


def _retention(qk, v_all, gate, tables, gn, *, rows_per_step=8192):
    B, H2, S, d = qk.shape
    H = H2 // 2
    tr = min(rows_per_step, S)
    dint, kdec, qdec, cdec = tables
    C = RET_CHUNK
    kern = functools.partial(_retention_kernel, n_chunks=tr // C)
    head_blk = (1, 1, tr, d)
    return pl.pallas_call(
        kern,
        grid=(B, H, S // tr),
        in_specs=[
            pl.BlockSpec(head_blk, lambda b, h, r: (b, h, r, 0)),
            pl.BlockSpec(head_blk, lambda b, h, r: (b, H + h, r, 0)),
            pl.BlockSpec(head_blk, lambda b, h, r: (b, h, r, 0)),
            pl.BlockSpec(head_blk, lambda b, h, r: (b, h, r, 0)),
            pl.BlockSpec((1, C, C), lambda b, h, r: (h, 0, 0)),
            pl.BlockSpec((1, C, d), lambda b, h, r: (h, 0, 0)),
            pl.BlockSpec((1, C, d), lambda b, h, r: (h, 0, 0)),
            pl.BlockSpec((1, 1, d), lambda b, h, r: (h, 0, 0)),
            pl.BlockSpec((1, d), lambda b, h, r: (0, 0)),
        ],
        out_specs=pl.BlockSpec((1, tr, d), lambda b, h, r: (b, r, h)),
        out_shape=jax.ShapeDtypeStruct((B, S, H * d), BF16),
        scratch_shapes=[pltpu.VMEM((d, d), F32)],
        compiler_params=_cparams("parallel", "parallel", "arbitrary"),
        name="retention",
    )(qk, qk, v_all, gate, dint, kdec, qdec, cdec, gn.reshape(1, d))


def _sb_kernel(q_ref, k_ref, v_ref, w2_ref, gn_ref, o_ref, *, tq, tk, group, n_sub):
    scale = HEAD_DIM**-0.5
    w2 = w2_ref[...]
    gn = gn_ref[...]
    heads = range(group)

    def guard(unit, x):
        mask = unit[4]
        if mask is not None:
            head = [] if x.shape[1] == tk else [x[:, : x.shape[1] - tk]]
            x = jnp.concatenate(head + [jnp.where(mask, x[:, x.shape[1] - tk :], 0.0)], axis=1)
        return x

    def key_rows(unit):
        return pl.ds(unit[2], unit[3] * tk)

    def scan_keys(units):
        newest = lambda x: x[:, x.shape[1] - tk :]
        z2 = [_dot_nt(u[1], k_ref[0, u[0], key_rows(u), :]) * (scale * LOG2_E) for u in units]
        hi, lo = [], []
        for u, z in zip(units, z2):
            sp = guard(u, jnp.maximum(z, 0.0) + jnp.log2(1.0 + jnp.exp2(-jnp.abs(z))))
            h = sp.astype(BF16)
            hi.append(h)
            lo.append((sp - h.astype(F32)).astype(BF16))
        cr_new = [_dot(jnp.concatenate([newest(h), newest(l)], axis=1), w2) for h, l in zip(hi, lo)]
        cr_old = [_dot(jnp.concatenate([h[:, :tk], l[:, :tk]], axis=1), w2) if u[3] == 2 else None
                  for u, h, l in zip(units, hi, lo)]
        return list(zip(z2, cr_new, cr_old))

    def weigh(units, scans, runs, accs):
        a, new_runs = [], []
        for u, (z2, cr_new, cr_old), run in zip(units, scans, runs):
            run_mid = run + cr_new[:, tk:]
            if cr_old is not None:
                behind = jnp.concatenate([cr_old[:, :tk] + run_mid, cr_new[:, :tk] + run], axis=1)
                new_runs.append(run_mid + cr_old[:, tk:])
            else:
                behind = cr_new[:, :tk] + run
                new_runs.append(run_mid)
            a.append(guard(u, jnp.exp2(z2 - behind)).astype(BF16))
        new_accs = []
        for u, w, acc in zip(units, a, accs):
            out = _dot(w, v_ref[0, u[0], key_rows(u), :])
            new_accs.append(acc + (out if u[5] is None else jnp.where(u[5], out, 0.0)))
        return new_runs, new_accs

    mask = lax.broadcasted_iota(jnp.int32, (tk, tk), 1) < lax.broadcasted_iota(jnp.int32, (tk, tk), 0)

    def query_block(sub, carry):
        qi = pl.program_id(2) * n_sub + sub
        row0 = pl.multiple_of(sub * tq, tq)
        q_rows = lambda g, start, size: q_ref[0, g, pl.ds(pl.multiple_of(row0 + start, tk), size), :]
        q0 = pl.multiple_of(qi * tq, tq)
        diag = [(g, q_rows(g, half * tk, tk), q0, half + 1, mask, None) for g in heads for half in range(2)]
        prev_start = pl.multiple_of(jnp.maximum(qi - 1, 0) * tq, tq)
        prev = [(g, q_rows(g, 0, tq), prev_start, 2, None, qi > 0) for g in heads]
        scans = scan_keys(diag + prev)
        zero_run, zero_acc = jnp.zeros((tk, tk), F32), jnp.zeros((tk, HEAD_DIM), F32)
        runs, accs = weigh(diag, scans[: len(diag)], [zero_run] * len(diag), [zero_acc] * len(diag))
        runs = [jnp.concatenate([runs[2 * g], runs[2 * g + 1]], axis=0) for g in heads]
        accs = [jnp.concatenate([accs[2 * g], accs[2 * g + 1]], axis=0) for g in heads]
        runs, accs = weigh(prev, scans[len(diag) :], runs, accs)

        def cond(c):
            return jnp.logical_and(c[0] >= 0, jnp.min(functools.reduce(jnp.minimum, c[1])) < SB_EXP2_UNDERFLOW)

        def body(c):
            kblk, runs, accs = c
            kstart = pl.multiple_of(kblk * tq, tq)
            units = [(g, q_rows(g, 0, tq), kstart, 2, None, None) for g in heads]
            runs, accs = weigh(units, scan_keys(units), runs, accs)
            return kblk - 1, tuple(runs), tuple(accs)

        _, _, accs = lax.while_loop(cond, body, (qi - 2, tuple(runs), tuple(accs)))
        for g in heads:
            o_ref[0, pl.ds(row0, tq), g * HEAD_DIM : (g + 1) * HEAD_DIM] = _rms(accs[g], gn).astype(o_ref.dtype)
        return carry

    lax.fori_loop(0, n_sub, query_block, 0)


def _stick_breaking(sqk, v_all, gn, *, tk=128, group=4, blocks_per_step=8):
    B, H2, S, d = sqk.shape
    H = H2 // 2
    tq = 2 * tk
    group = min(group, H)
    n_groups = H // group
    idx = jnp.arange(tk)
    w2 = jnp.concatenate([(idx[:, None] >= idx[None, :]).astype(BF16), jnp.ones((tk, tk), BF16)], axis=1)
    w2 = jnp.concatenate([w2, w2], axis=0)
    n_sub = min(blocks_per_step, S // tq)
    kern = functools.partial(_sb_kernel, tq=tq, tk=tk, group=group, n_sub=n_sub)
    kv_spec = lambda: pl.BlockSpec((1, group, S, d), lambda b, h, qi: (b, n_groups + h, 0, 0))
    return pl.pallas_call(
        kern,
        grid=(B, n_groups, S // (tq * n_sub)),
        in_specs=[
            pl.BlockSpec((1, group, tq * n_sub, d), lambda b, h, qi: (b, h, qi, 0)),
            kv_spec(),
            kv_spec(),
            pl.BlockSpec((2 * tk, 2 * tk), lambda b, h, qi: (0, 0)),
            pl.BlockSpec((1, d), lambda b, h, qi: (0, 0)),
        ],
        out_specs=pl.BlockSpec((1, tq * n_sub, group * d), lambda b, h, qi: (b, qi, h)),
        out_shape=jax.ShapeDtypeStruct((B, S, H * d), BF16),
        compiler_params=_cparams("parallel", "parallel", "arbitrary"),
        name="stick_breaking",
    )(sqk, sqk, v_all, w2, gn.reshape(1, d))


def _outproj_kernel(a1_ref, a2_ref, w1_ref, w2_ref, x_ref, o_ref):
    for rows in _row_chunks(o_ref.shape[0]):
        acc = _dot(a1_ref[rows, :], w1_ref[...]) + _dot(a2_ref[rows, :], w2_ref[...])
        o_ref[rows, :] = x_ref[rows, :] + acc


def _outproj(ret2d, sb2d, w_out, x2d, *, tm=PROJ_TILE_ROWS, tn=PROJ_TILE_COLS):
    T, K = ret2d.shape
    N = w_out.shape[1]
    tm, tn = min(tm, T), min(tn, N)
    return pl.pallas_call(
        _outproj_kernel,
        grid=(T // tm, N // tn),
        in_specs=[
            pl.BlockSpec((tm, K), lambda i, j: (i, 0)),
            pl.BlockSpec((tm, K), lambda i, j: (i, 0)),
            pl.BlockSpec((K, tn), lambda i, j: (0, j)),
            pl.BlockSpec((K, tn), lambda i, j: (1, j)),
            pl.BlockSpec((tm, tn), lambda i, j: (i, j)),
        ],
        out_specs=pl.BlockSpec((tm, tn), lambda i, j: (i, j)),
        out_shape=jax.ShapeDtypeStruct((T, N), F32),
        compiler_params=_cparams("parallel", "arbitrary"),
        name="outproj",
    )(ret2d, sb2d, w_out, w_out, x2d)


def _memkv_kernel(m_ref, gm_ref, w_ref, gk_ref, k_ref, v_ref):
    mn = _rms(m_ref[0], gm_ref[...]).astype(BF16)
    kv = _dot(mn, w_ref[...])
    cw = kv.shape[1] // 2
    gk = gk_ref[...]
    for hh in range(cw // HEAD_DIM):
        sl = slice(hh * HEAD_DIM, (hh + 1) * HEAD_DIM)
        k_ref[0, :, sl] = _rms(kv[:, sl], gk).astype(k_ref.dtype)
    v_ref[0] = kv[:, cw:].astype(v_ref.dtype)


def _memkv(mem, g_mem, w_kv, g_k):
    B, M, D = mem.shape
    cw = w_kv.shape[1] // 2
    return pl.pallas_call(
        _memkv_kernel,
        grid=(B,),
        in_specs=[
            pl.BlockSpec((1, M, D), lambda b: (b, 0, 0)),
            pl.BlockSpec((1, D), lambda b: (0, 0)),
            pl.BlockSpec((D, 2 * cw), lambda b: (0, 0)),
            pl.BlockSpec((1, HEAD_DIM), lambda b: (0, 0)),
        ],
        out_specs=[pl.BlockSpec((1, M, cw), lambda b: (b, 0, 0))] * 2,
        out_shape=[jax.ShapeDtypeStruct((B, M, cw), BF16)] * 2,
        compiler_params=_cparams("parallel"),
        name="memkv",
    )(mem, g_mem.reshape(1, D), w_kv, g_k.reshape(1, HEAD_DIM))


def _cross_kernel(h_ref, gc_ref, wq_ref, gq_ref, k_ref, v_ref, wo_ref, gf_ref, h2_ref, xn_ref):
    n_rows = h_ref.shape[0]
    part_rows = min(LANES, n_rows)
    parts = [slice(r0, r0 + part_rows) for r0 in range(0, n_rows, part_rows)]
    gq = gq_ref[...]
    scale = HEAD_DIM**-0.5
    h = [h_ref[rows, :] for rows in parts]
    xn = [_rms(hp, gc_ref[...]).astype(BF16) for hp in h]
    q = [_dot(x, wq_ref[...]) for x in xn]
    head_cols = [slice(c0, c0 + HEAD_DIM) for c0 in range(0, wq_ref.shape[1], HEAD_DIM)]
    s = [[_dot_nt(_rms(qp[:, sl], gq).astype(BF16), k_ref[0, :, sl]) * scale for sl in head_cols] for qp in q]
    o = []
    for sp in s:
        heads_out = []
        for sh, sl in zip(sp, head_cols):
            e = jnp.exp(sh - jnp.max(sh, axis=-1, keepdims=True))
            p = e / jnp.sum(e, axis=-1, keepdims=True)
            heads_out.append(_dot(p.astype(BF16), v_ref[0, :, sl]).astype(BF16))
        o.append(jnp.concatenate(heads_out, axis=-1))
    h2 = [hp + _dot(op, wo_ref[...]) for hp, op in zip(h, o)]
    for rows, h2p in zip(parts, h2):
        h2_ref[rows, :] = h2p
        xn_ref[rows, :] = _rms(h2p, gf_ref[...]).astype(xn_ref.dtype)


def _cross_attention(h2d, seq, g_cross, w_q, g_q, kmem, vmem, w_o, g_ffn, *, tm=256):
    T, D = h2d.shape
    _, M, cw = kmem.shape
    tm = min(tm, seq)
    tiles_per_seq = seq // tm
    const = lambda i: (0, 0)
    return pl.pallas_call(
        _cross_kernel,
        grid=(T // tm,),
        in_specs=[
            pl.BlockSpec((tm, D), lambda i: (i, 0)),
            pl.BlockSpec((1, D), const),
            pl.BlockSpec((D, cw), const),
            pl.BlockSpec((1, HEAD_DIM), const),
            pl.BlockSpec((1, M, cw), lambda i: (i // tiles_per_seq, 0, 0)),
            pl.BlockSpec((1, M, cw), lambda i: (i // tiles_per_seq, 0, 0)),
            pl.BlockSpec((cw, D), const),
            pl.BlockSpec((1, D), const),
        ],
        out_specs=[pl.BlockSpec((tm, D), lambda i: (i, 0))] * 2,
        out_shape=[jax.ShapeDtypeStruct((T, D), F32), jax.ShapeDtypeStruct((T, D), BF16)],
        compiler_params=_cparams("parallel"),
        name="cross_attention",
    )(h2d, g_cross.reshape(1, D), w_q, g_q.reshape(1, HEAD_DIM), kmem, vmem, w_o, g_ffn.reshape(1, D))


CONV_PIECE_ROWS = 64


def _ffn_up_kernel(a_ref, wg_ref, wv_ref, cwg_ref, cwv_ref, cbg_ref, cbv_ref, src_ref, o_ref, dst_ref, ubuf_ref,
                   carry_ref, *, tiles_per_seq, cast_blocks):
    i = pl.program_id(0)
    j = pl.program_id(1)

    @pl.when(jnp.logical_and(i == 0, j == 0))
    def _():
        carry_ref[...] = jnp.zeros_like(carry_ref)

    _cast_block(src_ref, dst_ref, i * pl.num_programs(1) + j, cast_blocks[0], cast_blocks[1])
    seq_start = (i % tiles_per_seq) == 0
    halo = SUBLANES
    chunks = _row_chunks(a_ref.shape[0])
    size = lambda c: chunks[c].stop - chunks[c].start
    lane_tiles = range(o_ref.shape[1] // LANES)
    projections = ((wg_ref, cwg_ref, cbg_ref), (wv_ref, cwv_ref, cbv_ref))

    def project(c):
        par = c % 2
        a = a_ref[chunks[c], :]
        for slot, (w_ref, _, _) in enumerate(projections):
            u = _dot(a, w_ref[...])
            for lt in lane_tiles:
                if c == 0:
                    ubuf_ref[par, slot, lt, 0:halo, :] = jnp.where(seq_start, 0.0, carry_ref[j, slot, lt])
                else:
                    ubuf_ref[par, slot, lt, 0:halo, :] = ubuf_ref[1 - par, slot, lt, size(c - 1) : size(c - 1) + halo, :]
                ubuf_ref[par, slot, lt, halo : halo + size(c), :] = u[:, lt * LANES : (lt + 1) * LANES]

    def activate(c):
        par = c % 2
        piece = min(CONV_PIECE_ROWS, size(c))
        for lt in lane_tiles:
            cols = slice(lt * LANES, (lt + 1) * LANES)
            for p0 in range(0, size(c), piece):

                def conv(slot):
                    _, cw_ref, cb_ref = projections[slot]
                    cw = cw_ref[:, cols]
                    c2 = cb_ref[:, cols] + ubuf_ref[par, slot, lt, pl.ds(p0 + halo - 2, piece), :] * cw[0:1, :]
                    c2 = c2 + ubuf_ref[par, slot, lt, pl.ds(p0 + halo - 1, piece), :] * cw[1:2, :]
                    return c2 + ubuf_ref[par, slot, lt, pl.ds(p0 + halo, piece), :] * cw[2:3, :]

                out_rows = slice(chunks[c].start + p0, chunks[c].start + p0 + piece)
                o_ref[out_rows, cols] = (_silu(conv(0)) * conv(1)).astype(o_ref.dtype)

    project(0)
    for c in range(1, len(chunks)):
        project(c)
        activate(c - 1)
    activate(len(chunks) - 1)
    last = len(chunks) - 1
    for slot in range(2):
        for lt in lane_tiles:
            carry_ref[j, slot, lt] = ubuf_ref[last % 2, slot, lt, size(last) : size(last) + halo, :]


def _ffn_up(xn, seq, w_up, conv_w, conv_b, cast, *, tm=2048, tf=256):
    T, D = xn.shape
    F = w_up.shape[1] // 2
    tm = min(tm, seq)
    assert F % tf == 0 and seq % tm == 0
    n_j = F // tf
    n_steps = (T // tm) * n_j
    n_blocks, src_spec, dst_spec, dst_shape = _cast_plan(cast, n_steps, lambda i, j: i * n_j + j)
    kern = functools.partial(_ffn_up_kernel, tiles_per_seq=seq // tm, cast_blocks=(n_blocks, n_steps))
    gate_col = lambda i, j: (0, j)
    value_col = lambda i, j: (0, n_j + j)
    return pl.pallas_call(
        kern,
        grid=(T // tm, n_j),
        in_specs=[
            pl.BlockSpec((tm, D), lambda i, j: (i, 0)),
            pl.BlockSpec((D, tf), gate_col),
            pl.BlockSpec((D, tf), value_col),
            pl.BlockSpec((CONV_WIDTH, tf), gate_col),
            pl.BlockSpec((CONV_WIDTH, tf), value_col),
            pl.BlockSpec((1, tf), gate_col),
            pl.BlockSpec((1, tf), value_col),
            src_spec,
        ],
        out_specs=[pl.BlockSpec((tm, tf), lambda i, j: (i, j)), dst_spec],
        out_shape=[jax.ShapeDtypeStruct((T, F), BF16), dst_shape],
        scratch_shapes=[
            pltpu.VMEM((2, 2, tf // LANES, SUBLANES + min(ROW_CHUNK, tm), LANES), F32),
            pltpu.VMEM((n_j, 2, tf // LANES, SUBLANES, LANES), F32),
        ],
        compiler_params=_cparams("arbitrary", "arbitrary"),
        name="ffn_up",
    )(xn, w_up, w_up, conv_w, conv_w, conv_b, conv_b, cast)


def _ffn_down_kernel(a_ref, w_ref, h_ref, o_ref):
    for rows in _row_chunks(o_ref.shape[0]):
        o_ref[rows, :] = h_ref[rows, :] + _dot(a_ref[rows, :], w_ref[...])


def _ffn_down(act, w_down, h2d, *, tm=512, tn=1024):
    T, F = act.shape
    N = w_down.shape[1]
    tm, tn = min(tm, T), min(tn, N)
    return pl.pallas_call(
        _ffn_down_kernel,
        grid=(N // tn, T // tm),
        in_specs=[
            pl.BlockSpec((tm, F), lambda j, i: (i, 0)),
            pl.BlockSpec((F, tn), lambda j, i: (0, j), pipeline_mode=pl.Buffered(1)),
            pl.BlockSpec((tm, tn), lambda j, i: (i, j)),
        ],
        out_specs=pl.BlockSpec((tm, tn), lambda j, i: (i, j)),
        out_shape=jax.ShapeDtypeStruct((T, N), F32),
        compiler_params=_cparams("parallel", "arbitrary"),
        name="ffn_down",
    )(act, w_down, h2d)


def _rope_tables(S):
    inv_freq = ROPE_BASE ** (-jnp.linspace(0.0, 1.0, HEAD_DIM // 2, dtype=F32))
    ang = jnp.arange(S, dtype=F32)[:, None] * inv_freq[None, :]
    cos, sin = jnp.cos(ang), jnp.sin(ang)
    return jnp.concatenate([cos, cos], axis=-1), jnp.concatenate([-sin, sin], axis=-1)


def _retention_tables(H):
    C = RET_CHUNK
    log_g = jnp.log1p(-jnp.exp2(-5.0 - jnp.arange(H, dtype=F32)))
    idx = jnp.arange(C, dtype=F32)
    diff = idx[:, None] - idx[None, :]
    dint = jnp.where(diff >= 0, jnp.exp(jnp.maximum(diff, 0.0)[None] * log_g[:, None, None]), 0.0)
    kdec = jnp.exp((C - 1 - idx)[None, :] * log_g[:, None])
    qdec = jnp.exp((idx + 1.0)[None, :] * log_g[:, None])
    cdec = jnp.exp(C * log_g)
    lanes = lambda t: jnp.broadcast_to(t[..., None], t.shape + (HEAD_DIM,))
    return dint, lanes(kdec), lanes(qdec), lanes(cdec[:, None])


def _layer(h, mem, attn_norm, w_in, ret_norm, sb_q_norm, sb_k_norm, sb_out_norm, w_out, cross_norm, mem_norm,
           cross_w_q, cross_w_kv, cross_q_norm, cross_k_norm, cross_w_o, ffn_norm, ffn_w_up, ffn_conv_w,
           ffn_conv_b, ffn_w_down):
    B, S, D = h.shape
    T = B * S
    W = D // 2
    H = W // HEAD_DIM
    tn = min(PROJ_TILE_COLS, W)
    seg = W // tn
    x2d = h.reshape(T, D)

    xn = _rmsnorm_rows(x2d, attn_norm, BF16)
    w_in_b = w_in.astype(BF16)
    cos2, sin2 = _rope_tables(S)
    tm_in = min(PROJ_TILE_ROWS, S)
    tiles_per_seq = S // tm_in
    rope_spec = pl.BlockSpec((tm_in, HEAD_DIM), lambda i, j: (i % tiles_per_seq, 0))
    ret_qk = _inproj(xn, w_in_b, B, S, n_col_tiles=2 * seg, col_map=lambda j: j, mode="rope", out_dtype=F32,
                     extra=(cos2, sin2), extra_specs=(rope_spec, rope_spec), n_scaled_tiles=seg, tn=tn,
                     name="inproj_ret_qk")
    v_all, w_up_b = _inproj(xn, w_in_b, B, S, n_col_tiles=2 * seg,
                            col_map=lambda j: jnp.where(j < seg, 2 * seg + j, 5 * seg + j), mode="plain",
                            out_dtype=BF16, tn=tn, name="inproj_v", cast=ffn_w_up)
    gate, w_out_b = _inproj(xn, w_in_b, B, S, n_col_tiles=seg, col_map=lambda j: 3 * seg + j, mode="plain",
                            out_dtype=F32, tn=tn, name="inproj_gate", cast=w_out)
    sb_gains = jnp.stack([sb_q_norm, sb_k_norm]).reshape(2, 1, HEAD_DIM)
    gain_spec = pl.BlockSpec((1, 1, HEAD_DIM), lambda i, j: (j // seg, 0, 0))
    sb_qk = _inproj(xn, w_in_b, B, S, n_col_tiles=2 * seg, col_map=lambda j: 4 * seg + j, mode="norm",
                    out_dtype=BF16, extra=(sb_gains,), extra_specs=(gain_spec,), tn=tn, name="inproj_sb_qk")

    ret = _retention(ret_qk, v_all, gate, _retention_tables(H), ret_norm)
    sb = _stick_breaking(sb_qk, v_all, sb_out_norm)
    h1 = _outproj(ret.reshape(T, W), sb.reshape(T, W), w_out_b, x2d)

    kmem, vmem = _memkv(mem, mem_norm, cross_w_kv.astype(BF16), cross_k_norm)
    h2, xn_ffn = _cross_attention(h1, S, cross_norm, cross_w_q.astype(BF16), cross_q_norm, kmem, vmem,
                                  cross_w_o.astype(BF16), ffn_norm)

    act, w_down_b = _ffn_up(xn_ffn, S, w_up_b, ffn_conv_w, ffn_conv_b.reshape(1, -1), ffn_w_down)
    out = _ffn_down(act, w_down_b, h2)
    return out.reshape(B, S, D)


def kernel(x, mem, attn_norm, w_in, ret_norm, sb_q_norm, sb_k_norm, sb_out_norm, w_out, cross_norm, mem_norm,
           cross_w_q, cross_w_kv, cross_q_norm, cross_k_norm, cross_w_o, ffn_norm, ffn_w_up, ffn_conv_w,
           ffn_conv_b, ffn_w_down):
    h = x
    for l in range(attn_norm.shape[0]):
        h = _layer(h, mem, attn_norm[l], w_in[l], ret_norm[l], sb_q_norm[l], sb_k_norm[l], sb_out_norm[l],
                   w_out[l], cross_norm[l], mem_norm[l], cross_w_q[l], cross_w_kv[l], cross_q_norm[l],
                   cross_k_norm[l], cross_w_o[l], ffn_norm[l], ffn_w_up[l], ffn_conv_w[l], ffn_conv_b[l],
                   ffn_w_down[l])
    return h
```

```python
import functools

import jax
import jax.numpy as jnp
from jax import lax
from jax.experimental import pallas as pl
from jax.experimental.pallas import tpu as pltpu

HEAD_DIM = 128
EPS = 1e-6
ROPE_BASE = 10000.0
RET_CHUNK = 128
CONV_WIDTH = 3
SUBLANES = 8
LANES = 128

VMEM_LIMIT_BYTES = 56 * 1024 * 1024

PROJ_TILE_ROWS = 1024
PROJ_TILE_COLS = 1024

ROW_CHUNK = 256

SB_EXP2_UNDERFLOW = 150.0
LOG2_E = 1.4426950408889634
SB_BLOCKS_IN_FLIGHT = 2

F32 = jnp.float32
BF16 = jnp.bfloat16


def _cparams(*sem):
    return pltpu.CompilerParams(dimension_semantics=sem, vmem_limit_bytes=VMEM_LIMIT_BYTES)


def _rms(x, g):
    return x * lax.rsqrt(jnp.mean(x * x, axis=-1, keepdims=True) + EPS) * g


def _silu(x):
    return x * (1.0 / (1.0 + jnp.exp(-x)))


def _dot(a, b):
    return jnp.dot(a, b, preferred_element_type=F32)


def _dot_nt(a, b):
    return lax.dot_general(a, b, (((1,), (1,)), ((), ())), preferred_element_type=F32)


def _row_chunks(n_rows):
    rc = min(ROW_CHUNK, n_rows)
    return [slice(r0, r0 + rc) for r0 in range(0, n_rows, rc)]


def _rmsnorm_kernel(x_ref, g_ref, o_ref):
    o_ref[...] = _rms(x_ref[...], g_ref[...]).astype(o_ref.dtype)


def _rmsnorm_rows(x2d, g, out_dtype, tm=512):
    T, D = x2d.shape
    tm = min(tm, T)
    return pl.pallas_call(
        _rmsnorm_kernel,
        grid=(T // tm,),
        in_specs=[pl.BlockSpec((tm, D), lambda i: (i, 0)), pl.BlockSpec((1, D), lambda i: (0, 0))],
        out_specs=pl.BlockSpec((tm, D), lambda i: (i, 0)),
        out_shape=jax.ShapeDtypeStruct((T, D), out_dtype),
        compiler_params=_cparams("parallel"),
        name="rmsnorm_rows",
    )(x2d, g.reshape(1, D))


def _cast_block(src_ref, dst_ref, step, n_blocks, n_steps):
    if n_blocks == n_steps:
        dst_ref[...] = src_ref[...].astype(dst_ref.dtype)
    else:

        @pl.when(step < n_blocks)
        def _():
            dst_ref[...] = src_ref[...].astype(dst_ref.dtype)


def _cast_plan(w, n_steps, step_of):
    n_rows = w.shape[0]
    packed_rows = 16
    n_blocks = max(n for n in range(1, n_steps + 1) if n_rows % n == 0 and (n_rows // n) % packed_rows == 0)
    spec = lambda: pl.BlockSpec((n_rows // n_blocks, w.shape[1]),
                                lambda *g: (jnp.minimum(step_of(*g), n_blocks - 1), 0))
    return n_blocks, spec(), spec(), jax.ShapeDtypeStruct(w.shape, BF16)


def _inproj_kernel(a_ref, w_ref, *rest, mode, heads_per_tile, n_scaled_tiles, n_extra, cast_blocks):
    j = pl.program_id(1)
    extra = rest[:n_extra]
    if cast_blocks:
        src_ref, o_ref, dst_ref = rest[n_extra:]
        _cast_block(src_ref, dst_ref, pl.program_id(0) * pl.num_programs(1) + j, cast_blocks[0], cast_blocks[1])
    else:
        (o_ref,) = rest[n_extra:]
    if mode == "rope":
        scale = jnp.where(j < n_scaled_tiles, HEAD_DIM**-0.5, 1.0).astype(F32)
    elif mode == "norm":
        gain = extra[0][0]
    for rows in _row_chunks(a_ref.shape[0]):
        acc = _dot(a_ref[rows, :], w_ref[...])
        for hh in range(heads_per_tile):
            xh = acc[:, hh * HEAD_DIM : (hh + 1) * HEAD_DIM]
            if mode == "rope":
                xh = (xh * extra[0][rows, :] + pltpu.roll(xh, HEAD_DIM // 2, axis=1) * extra[1][rows, :]) * scale
            elif mode == "norm":
                xh = _rms(xh, gain)
            o_ref[0, hh, rows, :] = xh.astype(o_ref.dtype)


def _inproj(xn, w, batch, seq, *, n_col_tiles, col_map, mode, out_dtype, extra=(), extra_specs=(),
            n_scaled_tiles=0, tm=PROJ_TILE_ROWS, tn=PROJ_TILE_COLS, name, cast=None):
    T, D = xn.shape
    tm = min(tm, seq)
    hpt = tn // HEAD_DIM
    tiles_per_seq = seq // tm
    n_steps = (T // tm) * n_col_tiles
    in_specs = [
        pl.BlockSpec((tm, D), lambda i, j: (i, 0)),
        pl.BlockSpec((D, tn), lambda i, j: (0, col_map(j))),
        *extra_specs,
    ]
    out_specs = pl.BlockSpec((1, hpt, tm, HEAD_DIM), lambda i, j: (i // tiles_per_seq, j, i % tiles_per_seq, 0))
    out_shape = jax.ShapeDtypeStruct((batch, n_col_tiles * hpt, seq, HEAD_DIM), out_dtype)
    operands = [xn, w, *extra]
    cast_blocks = None
    if cast is not None:
        n_blocks, src_spec, dst_spec, dst_shape = _cast_plan(cast, n_steps, lambda i, j: i * n_col_tiles + j)
        in_specs.append(src_spec)
        out_specs, out_shape = [out_specs, dst_spec], [out_shape, dst_shape]
        operands.append(cast)
        cast_blocks = (n_blocks, n_steps)
    kern = functools.partial(_inproj_kernel, mode=mode, heads_per_tile=hpt, n_scaled_tiles=n_scaled_tiles,
                             n_extra=len(extra), cast_blocks=cast_blocks)
    return pl.pallas_call(
        kern,
        grid=(T // tm, n_col_tiles),
        in_specs=in_specs,
        out_specs=out_specs,
        out_shape=out_shape,
        compiler_params=_cparams("arbitrary", "arbitrary"),
        name=name,
    )(*operands)


def _retention_kernel(q_ref, k_ref, v_ref, g_ref, dint_ref, kdec_ref, qdec_ref, cdec_ref, gn_ref, o_ref,
                      state_ref, *, n_chunks):
    @pl.when(pl.program_id(2) == 0)
    def _():
        state_ref[...] = jnp.zeros_like(state_ref)

    dint = dint_ref[0]
    kdec = kdec_ref[0]
    qdec = qdec_ref[0]
    cdec = cdec_ref[0]
    gn = gn_ref[...]
    C = RET_CHUNK
    chunks = range(n_chunks)
    rows = [pl.ds(n * C, C) for n in chunks]
    q = [q_ref[0, 0, r, :] for r in rows]
    k = [k_ref[0, 0, r, :] for r in rows]
    scores = [_dot_nt(q[n].astype(BF16), k[n].astype(BF16)) for n in chunks]
    kv = [_dot((k[n] * kdec).T.astype(BF16), v_ref[0, 0, rows[n], :]) for n in chunks]
    states = [state_ref[...]]
    for n in chunks:
        states.append(states[n] * cdec + kv[n])
    state_ref[...] = states[n_chunks]
    out = [
        _dot(
            jnp.concatenate([(scores[n] * dint).astype(BF16), (q[n] * qdec).astype(BF16)], axis=1),
            jnp.concatenate([v_ref[0, 0, rows[n], :], states[n].astype(BF16)], axis=0),
        )
        for n in chunks
    ]
    for n in chunks:
        y = _rms(out[n], gn) * _silu(g_ref[0, 0, rows[n], :])
        o_ref[0, rows[n], :] = y.astype(o_ref.dtype)


def _retention(qk, v_all, gate, tables, gn, *, rows_per_step=8192):
    B, H2, S, d = qk.shape
    H = H2 // 2
    tr = min(rows_per_step, S)
    dint, kdec, qdec, cdec = tables
    C = RET_CHUNK
    kern = functools.partial(_retention_kernel, n_chunks=tr // C)
    head_blk = (1, 1, tr, d)
    return pl.pallas_call(
        kern,
        grid=(B, H, S // tr),
        in_specs=[
            pl.BlockSpec(head_blk, lambda b, h, r: (b, h, r, 0)),
            pl.BlockSpec(head_blk, lambda b, h, r: (b, H + h, r, 0)),
            pl.BlockSpec(head_blk, lambda b, h, r: (b, h, r, 0)),
            pl.BlockSpec(head_blk, lambda b, h, r: (b, h, r, 0)),
            pl.BlockSpec((1, C, C), lambda b, h, r: (h, 0, 0)),
            pl.BlockSpec((1, C, d), lambda b, h, r: (h, 0, 0)),
            pl.BlockSpec((1, C, d), lambda b, h, r: (h, 0, 0)),
            pl.BlockSpec((1, 1, d), lambda b, h, r: (h, 0, 0)),
            pl.BlockSpec((1, d), lambda b, h, r: (0, 0)),
        ],
        out_specs=pl.BlockSpec((1, tr, d), lambda b, h, r: (b, r, h)),
        out_shape=jax.ShapeDtypeStruct((B, S, H * d), BF16),
        scratch_shapes=[pltpu.VMEM((d, d), F32)],
        compiler_params=_cparams("parallel", "parallel", "arbitrary"),
        name="retention",
    )(qk, qk, v_all, gate, dint, kdec, qdec, cdec, gn.reshape(1, d))


def _sb_kernel(q_ref, k_ref, v_ref, w2_ref, gn_ref, o_ref, *, tq, tk, group, n_sub):
    scale = HEAD_DIM**-0.5
    w2 = w2_ref[...]
    gn = gn_ref[...]
    heads = range(group)

    def guard(unit, x):
        mask = unit[4]
        if mask is not None:
            head = [] if x.shape[1] == tk else [x[:, : x.shape[1] - tk]]
            x = jnp.concatenate(head + [jnp.where(mask, x[:, x.shape[1] - tk :], 0.0)], axis=1)
        return x

    def key_rows(unit):
        return pl.ds(unit[2], unit[3] * tk)

    def scan_keys(units):
        newest = lambda x: x[:, x.shape[1] - tk :]
        z2 = [_dot_nt(u[1], k_ref[0, u[0], key_rows(u), :]) * (scale * LOG2_E) for u in units]
        hi, lo = [], []
        for u, z in zip(units, z2):
            sp = guard(u, jnp.maximum(z, 0.0) + jnp.log2(1.0 + jnp.exp2(-jnp.abs(z))))
            h = sp.astype(BF16)
            hi.append(h)
            lo.append((sp - h.astype(F32)).astype(BF16))
        cr_new = [_dot(jnp.concatenate([newest(h), newest(l)], axis=1), w2) for h, l in zip(hi, lo)]
        cr_old = [_dot(jnp.concatenate([h[:, :tk], l[:, :tk]], axis=1), w2) if u[3] == 2 else None
                  for u, h, l in zip(units, hi, lo)]
        return list(zip(z2, cr_new, cr_old))

    def weigh(units, scans, runs, accs):
        a, new_runs = [], []
        for u, (z2, cr_new, cr_old), run in zip(units, scans, runs):
            run_mid = run + cr_new[:, tk:]
            if cr_old is not None:
                behind = jnp.concatenate([cr_old[:, :tk] + run_mid, cr_new[:, :tk] + run], axis=1)
                new_runs.append(run_mid + cr_old[:, tk:])
            else:
                behind = cr_new[:, :tk] + run
                new_runs.append(run_mid)
            a.append(guard(u, jnp.exp2(z2 - behind)).astype(BF16))
        new_accs = []
        for u, w, acc in zip(units, a, accs):
            out = _dot(w, v_ref[0, u[0], key_rows(u), :])
            new_accs.append(acc + (out if u[5] is None else jnp.where(u[5], out, 0.0)))
        return new_runs, new_accs

    mask = lax.broadcasted_iota(jnp.int32, (tk, tk), 1) < lax.broadcasted_iota(jnp.int32, (tk, tk), 0)

    def query_blocks(step, carry):
        subs = [step * SB_BLOCKS_IN_FLIGHT + n for n in range(SB_BLOCKS_IN_FLIGHT)]
        qis = [pl.program_id(2) * n_sub + sub for sub in subs]
        row0s = [pl.multiple_of(sub * tq, tq) for sub in subs]

        def q_rows(n, g, start, size):
            return q_ref[0, g, pl.ds(pl.multiple_of(row0s[n] + start, tk), size), :]

        diag = [[(g, q_rows(n, g, half * tk, tk), pl.multiple_of(qi * tq, tq), half + 1, mask, None)
                 for g in heads for half in range(2)] for n, qi in enumerate(qis)]
        prev = [[(g, q_rows(n, g, 0, tq), pl.multiple_of(jnp.maximum(qi - 1, 0) * tq, tq), 2, None, qi > 0)
                 for g in heads] for n, qi in enumerate(qis)]
        units = [u for n in range(len(qis)) for u in diag[n] + prev[n]]
        scans = scan_keys(units)
        per_block = len(diag[0]) + len(prev[0])
        zero_run, zero_acc = jnp.zeros((tk, tk), F32), jnp.zeros((tk, HEAD_DIM), F32)
        state = []
        for n in range(len(qis)):
            sc = scans[n * per_block : (n + 1) * per_block]
            n_diag = len(diag[n])
            runs, accs = weigh(diag[n], sc[:n_diag], [zero_run] * n_diag, [zero_acc] * n_diag)
            runs = [jnp.concatenate([runs[2 * g], runs[2 * g + 1]], axis=0) for g in heads]
            accs = [jnp.concatenate([accs[2 * g], accs[2 * g + 1]], axis=0) for g in heads]
            state.append(weigh(prev[n], sc[n_diag:], runs, accs))

        def cond(c):
            return jnp.logical_and(c[0] >= 0, jnp.min(functools.reduce(jnp.minimum, c[1])) < SB_EXP2_UNDERFLOW)

        for n, qi in enumerate(qis):

            def body(c, n=n):
                kblk, runs, accs = c
                kstart = pl.multiple_of(kblk * tq, tq)
                older = [(g, q_rows(n, g, 0, tq), kstart, 2, None, None) for g in heads]
                runs, accs = weigh(older, scan_keys(older), runs, accs)
                return kblk - 1, tuple(runs), tuple(accs)

            runs, accs = state[n]
            _, _, accs = lax.while_loop(cond, body, (qi - 2, tuple(runs), tuple(accs)))
            for g in heads:
                o_ref[0, pl.ds(row0s[n], tq), g * HEAD_DIM : (g + 1) * HEAD_DIM] = _rms(accs[g], gn).astype(o_ref.dtype)
        return carry

    lax.fori_loop(0, n_sub // SB_BLOCKS_IN_FLIGHT, query_blocks, 0)


def _stick_breaking(sqk, v_all, gn, *, tk=128, group=4, blocks_per_step=8):
    B, H2, S, d = sqk.shape
    H = H2 // 2
    tq = 2 * tk
    group = min(group, H)
    n_groups = H // group
    idx = jnp.arange(tk)
    w2 = jnp.concatenate([(idx[:, None] >= idx[None, :]).astype(BF16), jnp.ones((tk, tk), BF16)], axis=1)
    w2 = jnp.concatenate([w2, w2], axis=0)
    n_sub = min(blocks_per_step, S // tq)
    assert S % (tq * n_sub) == 0 and n_sub % SB_BLOCKS_IN_FLIGHT == 0
    kern = functools.partial(_sb_kernel, tq=tq, tk=tk, group=group, n_sub=n_sub)
    kv_spec = lambda: pl.BlockSpec((1, group, S, d), lambda b, h, qi: (b, n_groups + h, 0, 0))
    return pl.pallas_call(
        kern,
        grid=(B, n_groups, S // (tq * n_sub)),
        in_specs=[
            pl.BlockSpec((1, group, tq * n_sub, d), lambda b, h, qi: (b, h, qi, 0)),
            kv_spec(),
            kv_spec(),
            pl.BlockSpec((2 * tk, 2 * tk), lambda b, h, qi: (0, 0)),
            pl.BlockSpec((1, d), lambda b, h, qi: (0, 0)),
        ],
        out_specs=pl.BlockSpec((1, tq * n_sub, group * d), lambda b, h, qi: (b, qi, h)),
        out_shape=jax.ShapeDtypeStruct((B, S, H * d), BF16),
        compiler_params=_cparams("parallel", "parallel", "arbitrary"),
        name="stick_breaking",
    )(sqk, sqk, v_all, w2, gn.reshape(1, d))


def _outproj_kernel(a1_ref, a2_ref, w1_ref, w2_ref, x_ref, o_ref):
    for rows in _row_chunks(o_ref.shape[0]):
        acc = _dot(a1_ref[rows, :], w1_ref[...]) + _dot(a2_ref[rows, :], w2_ref[...])
        o_ref[rows, :] = x_ref[rows, :] + acc


def _outproj(ret2d, sb2d, w_out, x2d, *, tm=PROJ_TILE_ROWS, tn=PROJ_TILE_COLS):
    T, K = ret2d.shape
    N = w_out.shape[1]
    tm, tn = min(tm, T), min(tn, N)
    return pl.pallas_call(
        _outproj_kernel,
        grid=(T // tm, N // tn),
        in_specs=[
            pl.BlockSpec((tm, K), lambda i, j: (i, 0)),
            pl.BlockSpec((tm, K), lambda i, j: (i, 0)),
            pl.BlockSpec((K, tn), lambda i, j: (0, j)),
            pl.BlockSpec((K, tn), lambda i, j: (1, j)),
            pl.BlockSpec((tm, tn), lambda i, j: (i, j)),
        ],
        out_specs=pl.BlockSpec((tm, tn), lambda i, j: (i, j)),
        out_shape=jax.ShapeDtypeStruct((T, N), F32),
        compiler_params=_cparams("parallel", "arbitrary"),
        name="outproj",
    )(ret2d, sb2d, w_out, w_out, x2d)


def _memkv_kernel(m_ref, gm_ref, w_ref, gk_ref, k_ref, v_ref):
    mn = _rms(m_ref[0], gm_ref[...]).astype(BF16)
    kv = _dot(mn, w_ref[...])
    cw = kv.shape[1] // 2
    gk = gk_ref[...]
    for hh in range(cw // HEAD_DIM):
        sl = slice(hh * HEAD_DIM, (hh + 1) * HEAD_DIM)
        k_ref[0, :, sl] = _rms(kv[:, sl], gk).astype(k_ref.dtype)
    v_ref[0] = kv[:, cw:].astype(v_ref.dtype)


def _memkv(mem, g_mem, w_kv, g_k):
    B, M, D = mem.shape
    cw = w_kv.shape[1] // 2
    return pl.pallas_call(
        _memkv_kernel,
        grid=(B,),
        in_specs=[
            pl.BlockSpec((1, M, D), lambda b: (b, 0, 0)),
            pl.BlockSpec((1, D), lambda b: (0, 0)),
            pl.BlockSpec((D, 2 * cw), lambda b: (0, 0)),
            pl.BlockSpec((1, HEAD_DIM), lambda b: (0, 0)),
        ],
        out_specs=[pl.BlockSpec((1, M, cw), lambda b: (b, 0, 0))] * 2,
        out_shape=[jax.ShapeDtypeStruct((B, M, cw), BF16)] * 2,
        compiler_params=_cparams("parallel"),
        name="memkv",
    )(mem, g_mem.reshape(1, D), w_kv, g_k.reshape(1, HEAD_DIM))


def _cross_kernel(h_ref, gc_ref, wq_ref, gq_ref, k_ref, v_ref, wo_ref, gf_ref, h2_ref, xn_ref):
    n_rows = h_ref.shape[0]
    part_rows = min(LANES, n_rows)
    parts = [slice(r0, r0 + part_rows) for r0 in range(0, n_rows, part_rows)]
    gq = gq_ref[...]
    scale = HEAD_DIM**-0.5
    h = [h_ref[rows, :] for rows in parts]
    xn = [_rms(hp, gc_ref[...]).astype(BF16) for hp in h]
    q = [_dot(x, wq_ref[...]) for x in xn]
    head_cols = [slice(c0, c0 + HEAD_DIM) for c0 in range(0, wq_ref.shape[1], HEAD_DIM)]
    s = [[_dot_nt(_rms(qp[:, sl], gq).astype(BF16), k_ref[0, :, sl]) * scale for sl in head_cols] for qp in q]
    o = []
    for sp in s:
        heads_out = []
        for sh, sl in zip(sp, head_cols):
            e = jnp.exp(sh - jnp.max(sh, axis=-1, keepdims=True))
            p = e / jnp.sum(e, axis=-1, keepdims=True)
            heads_out.append(_dot(p.astype(BF16), v_ref[0, :, sl]).astype(BF16))
        o.append(jnp.concatenate(heads_out, axis=-1))
    h2 = [hp + _dot(op, wo_ref[...]) for hp, op in zip(h, o)]
    for rows, h2p in zip(parts, h2):
        h2_ref[rows, :] = h2p
        xn_ref[rows, :] = _rms(h2p, gf_ref[...]).astype(xn_ref.dtype)


def _cross_attention(h2d, seq, g_cross, w_q, g_q, kmem, vmem, w_o, g_ffn, *, tm=256):
    T, D = h2d.shape
    _, M, cw = kmem.shape
    tm = min(tm, seq)
    tiles_per_seq = seq // tm
    const = lambda i: (0, 0)
    return pl.pallas_call(
        _cross_kernel,
        grid=(T // tm,),
        in_specs=[
            pl.BlockSpec((tm, D), lambda i: (i, 0)),
            pl.BlockSpec((1, D), const),
            pl.BlockSpec((D, cw), const),
            pl.BlockSpec((1, HEAD_DIM), const),
            pl.BlockSpec((1, M, cw), lambda i: (i // tiles_per_seq, 0, 0)),
            pl.BlockSpec((1, M, cw), lambda i: (i // tiles_per_seq, 0, 0)),
            pl.BlockSpec((cw, D), const),
            pl.BlockSpec((1, D), const),
        ],
        out_specs=[pl.BlockSpec((tm, D), lambda i: (i, 0))] * 2,
        out_shape=[jax.ShapeDtypeStruct((T, D), F32), jax.ShapeDtypeStruct((T, D), BF16)],
        compiler_params=_cparams("parallel"),
        name="cross_attention",
    )(h2d, g_cross.reshape(1, D), w_q, g_q.reshape(1, HEAD_DIM), kmem, vmem, w_o, g_ffn.reshape(1, D))


CONV_PIECE_ROWS = 64


def _ffn_up_kernel(a_ref, wg_ref, wv_ref, cwg_ref, cwv_ref, cbg_ref, cbv_ref, src_ref, o_ref, dst_ref, ubuf_ref,
                   carry_ref, *, tiles_per_seq, cast_blocks):
    i = pl.program_id(0)
    j = pl.program_id(1)

    @pl.when(jnp.logical_and(i == 0, j == 0))
    def _():
        carry_ref[...] = jnp.zeros_like(carry_ref)

    _cast_block(src_ref, dst_ref, i * pl.num_programs(1) + j, cast_blocks[0], cast_blocks[1])
    seq_start = (i % tiles_per_seq) == 0
    halo = SUBLANES
    chunks = _row_chunks(a_ref.shape[0])
    size = lambda c: chunks[c].stop - chunks[c].start
    lane_tiles = range(o_ref.shape[1] // LANES)
    projections = ((wg_ref, cwg_ref, cbg_ref), (wv_ref, cwv_ref, cbv_ref))

    def project(c):
        par = c % 2
        a = a_ref[chunks[c], :]
        for slot, (w_ref, _, _) in enumerate(projections):
            u = _dot(a, w_ref[...])
            for lt in lane_tiles:
                if c == 0:
                    ubuf_ref[par, slot, lt, 0:halo, :] = jnp.where(seq_start, 0.0, carry_ref[j, slot, lt])
                else:
                    ubuf_ref[par, slot, lt, 0:halo, :] = ubuf_ref[1 - par, slot, lt, size(c - 1) : size(c - 1) + halo, :]
                ubuf_ref[par, slot, lt, halo : halo + size(c), :] = u[:, lt * LANES : (lt + 1) * LANES]

    def activate(c):
        par = c % 2
        piece = min(CONV_PIECE_ROWS, size(c))
        for lt in lane_tiles:
            cols = slice(lt * LANES, (lt + 1) * LANES)
            for p0 in range(0, size(c), piece):

                def conv(slot):
                    _, cw_ref, cb_ref = projections[slot]
                    cw = cw_ref[:, cols]
                    c2 = cb_ref[:, cols] + ubuf_ref[par, slot, lt, pl.ds(p0 + halo - 2, piece), :] * cw[0:1, :]
                    c2 = c2 + ubuf_ref[par, slot, lt, pl.ds(p0 + halo - 1, piece), :] * cw[1:2, :]
                    return c2 + ubuf_ref[par, slot, lt, pl.ds(p0 + halo, piece), :] * cw[2:3, :]

                out_rows = slice(chunks[c].start + p0, chunks[c].start + p0 + piece)
                o_ref[out_rows, cols] = (_silu(conv(0)) * conv(1)).astype(o_ref.dtype)

    project(0)
    for c in range(1, len(chunks)):
        project(c)
        activate(c - 1)
    activate(len(chunks) - 1)
    last = len(chunks) - 1
    for slot in range(2):
        for lt in lane_tiles:
            carry_ref[j, slot, lt] = ubuf_ref[last % 2, slot, lt, size(last) : size(last) + halo, :]


def _ffn_up(xn, seq, w_up, conv_w, conv_b, cast, *, tm=2048, tf=256):
    T, D = xn.shape
    F = w_up.shape[1] // 2
    tm = min(tm, seq)
    assert F % tf == 0 and seq % tm == 0
    n_j = F // tf
    n_steps = (T // tm) * n_j
    n_blocks, src_spec, dst_spec, dst_shape = _cast_plan(cast, n_steps, lambda i, j: i * n_j + j)
    kern = functools.partial(_ffn_up_kernel, tiles_per_seq=seq // tm, cast_blocks=(n_blocks, n_steps))
    gate_col = lambda i, j: (0, j)
    value_col = lambda i, j: (0, n_j + j)
    return pl.pallas_call(
        kern,
        grid=(T // tm, n_j),
        in_specs=[
            pl.BlockSpec((tm, D), lambda i, j: (i, 0)),
            pl.BlockSpec((D, tf), gate_col),
            pl.BlockSpec((D, tf), value_col),
            pl.BlockSpec((CONV_WIDTH, tf), gate_col),
            pl.BlockSpec((CONV_WIDTH, tf), value_col),
            pl.BlockSpec((1, tf), gate_col),
            pl.BlockSpec((1, tf), value_col),
            src_spec,
        ],
        out_specs=[pl.BlockSpec((tm, tf), lambda i, j: (i, j)), dst_spec],
        out_shape=[jax.ShapeDtypeStruct((T, F), BF16), dst_shape],
        scratch_shapes=[
            pltpu.VMEM((2, 2, tf // LANES, SUBLANES + min(ROW_CHUNK, tm), LANES), F32),
            pltpu.VMEM((n_j, 2, tf // LANES, SUBLANES, LANES), F32),
        ],
        compiler_params=_cparams("arbitrary", "arbitrary"),
        name="ffn_up",
    )(xn, w_up, w_up, conv_w, conv_w, conv_b, conv_b, cast)


def _ffn_down_kernel(a_ref, w_ref, h_ref, o_ref):
    for rows in _row_chunks(o_ref.shape[0]):
        o_ref[rows, :] = h_ref[rows, :] + _dot(a_ref[rows, :], w_ref[...])


def _ffn_down(act, w_down, h2d, *, tm=512, tn=1024):
    T, F = act.shape
    N = w_down.shape[1]
    tm, tn = min(tm, T), min(tn, N)
    return pl.pallas_call(
        _ffn_down_kernel,
        grid=(N // tn, T // tm),
        in_specs=[
            pl.BlockSpec((tm, F), lambda j, i: (i, 0)),
            pl.BlockSpec((F, tn), lambda j, i: (0, j), pipeline_mode=pl.Buffered(1)),
            pl.BlockSpec((tm, tn), lambda j, i: (i, j)),
        ],
        out_specs=pl.BlockSpec((tm, tn), lambda j, i: (i, j)),
        out_shape=jax.ShapeDtypeStruct((T, N), F32),
        compiler_params=_cparams("parallel", "arbitrary"),
        name="ffn_down",
    )(act, w_down, h2d)


def _rope_tables(S):
    inv_freq = ROPE_BASE ** (-jnp.linspace(0.0, 1.0, HEAD_DIM // 2, dtype=F32))
    ang = jnp.arange(S, dtype=F32)[:, None] * inv_freq[None, :]
    cos, sin = jnp.cos(ang), jnp.sin(ang)
    return jnp.concatenate([cos, cos], axis=-1), jnp.concatenate([-sin, sin], axis=-1)


def _retention_tables(H):
    C = RET_CHUNK
    log_g = jnp.log1p(-jnp.exp2(-5.0 - jnp.arange(H, dtype=F32)))
    idx = jnp.arange(C, dtype=F32)
    diff = idx[:, None] - idx[None, :]
    dint = jnp.where(diff >= 0, jnp.exp(jnp.maximum(diff, 0.0)[None] * log_g[:, None, None]), 0.0)
    kdec = jnp.exp((C - 1 - idx)[None, :] * log_g[:, None])
    qdec = jnp.exp((idx + 1.0)[None, :] * log_g[:, None])
    cdec = jnp.exp(C * log_g)
    lanes = lambda t: jnp.broadcast_to(t[..., None], t.shape + (HEAD_DIM,))
    return dint, lanes(kdec), lanes(qdec), lanes(cdec[:, None])


def _layer(h, mem, attn_norm, w_in, ret_norm, sb_q_norm, sb_k_norm, sb_out_norm, w_out, cross_norm, mem_norm,
           cross_w_q, cross_w_kv, cross_q_norm, cross_k_norm, cross_w_o, ffn_norm, ffn_w_up, ffn_conv_w,
           ffn_conv_b, ffn_w_down):
    B, S, D = h.shape
    T = B * S
    W = D // 2
    H = W // HEAD_DIM
    tn = min(PROJ_TILE_COLS, W)
    seg = W // tn
    x2d = h.reshape(T, D)

    xn = _rmsnorm_rows(x2d, attn_norm, BF16)
    w_in_b = w_in.astype(BF16)
    cos2, sin2 = _rope_tables(S)
    tm_in = min(PROJ_TILE_ROWS, S)
    tiles_per_seq = S // tm_in
    rope_spec = pl.BlockSpec((tm_in, HEAD_DIM), lambda i, j: (i % tiles_per_seq, 0))
    ret_qk = _inproj(xn, w_in_b, B, S, n_col_tiles=2 * seg, col_map=lambda j: j, mode="rope", out_dtype=F32,
                     extra=(cos2, sin2), extra_specs=(rope_spec, rope_spec), n_scaled_tiles=seg, tn=tn,
                     name="inproj_ret_qk")
    v_all, w_up_b = _inproj(xn, w_in_b, B, S, n_col_tiles=2 * seg,
                            col_map=lambda j: jnp.where(j < seg, 2 * seg + j, 5 * seg + j), mode="plain",
                            out_dtype=BF16, tn=tn, name="inproj_v", cast=ffn_w_up)
    gate, w_out_b = _inproj(xn, w_in_b, B, S, n_col_tiles=seg, col_map=lambda j: 3 * seg + j, mode="plain",
                            out_dtype=F32, tn=tn, name="inproj_gate", cast=w_out)
    sb_gains = jnp.stack([sb_q_norm, sb_k_norm]).reshape(2, 1, HEAD_DIM)
    gain_spec = pl.BlockSpec((1, 1, HEAD_DIM), lambda i, j: (j // seg, 0, 0))
    sb_qk = _inproj(xn, w_in_b, B, S, n_col_tiles=2 * seg, col_map=lambda j: 4 * seg + j, mode="norm",
                    out_dtype=BF16, extra=(sb_gains,), extra_specs=(gain_spec,), tn=tn, name="inproj_sb_qk")

    ret = _retention(ret_qk, v_all, gate, _retention_tables(H), ret_norm)
    sb = _stick_breaking(sb_qk, v_all, sb_out_norm)
    h1 = _outproj(ret.reshape(T, W), sb.reshape(T, W), w_out_b, x2d)

    kmem, vmem = _memkv(mem, mem_norm, cross_w_kv.astype(BF16), cross_k_norm)
    h2, xn_ffn = _cross_attention(h1, S, cross_norm, cross_w_q.astype(BF16), cross_q_norm, kmem, vmem,
                                  cross_w_o.astype(BF16), ffn_norm)

    act, w_down_b = _ffn_up(xn_ffn, S, w_up_b, ffn_conv_w, ffn_conv_b.reshape(1, -1), ffn_w_down)
    out = _ffn_down(act, w_down_b, h2)
    return out.reshape(B, S, D)


def kernel(x, mem, attn_norm, w_in, ret_norm, sb_q_norm, sb_k_norm, sb_out_norm, w_out, cross_norm, mem_norm,
           cross_w_q, cross_w_kv, cross_q_norm, cross_k_norm, cross_w_o, ffn_norm, ffn_w_up, ffn_conv_w,
           ffn_conv_b, ffn_w_down):
    h = x
    for l in range(attn_norm.shape[0]):
        h = _layer(h, mem, attn_norm[l], w_in[l], ret_norm[l], sb_q_norm[l], sb_k_norm[l], sb_out_norm[l],
                   w_out[l], cross_norm[l], mem_norm[l], cross_w_q[l], cross_w_kv[l], cross_q_norm[l],
                   cross_k_norm[l], cross_w_o[l], ffn_norm[l], ffn_w_up[l], ffn_conv_w[l], ffn_conv_b[l],
                   ffn_w_down[l])
    return h
```

```python
import functools

import jax
import jax.numpy as jnp
from jax import lax
from jax.experimental import pallas as pl
from jax.experimental.pallas import tpu as pltpu

HEAD_DIM = 128
EPS = 1e-6
ROPE_BASE = 10000.0
RET_CHUNK = 128
CONV_WIDTH = 3
SUBLANES = 8
LANES = 128

VMEM_LIMIT_BYTES = 60 * 1024 * 1024

PROJ_TILE_ROWS = 1024
PROJ_TILE_COLS = 1024

ROW_CHUNK = 256

SB_EXP2_UNDERFLOW = 150.0
LOG2_E = 1.4426950408889634
SB_BLOCKS_IN_FLIGHT = 4

F32 = jnp.float32
BF16 = jnp.bfloat16


def _cparams(*sem):
    return pltpu.CompilerParams(dimension_semantics=sem, vmem_limit_bytes=VMEM_LIMIT_BYTES)


def _rms(x, g):
    return x * lax.rsqrt(jnp.mean(x * x, axis=-1, keepdims=True) + EPS) * g


def _silu(x):
    return x * (1.0 / (1.0 + jnp.exp(-x)))


def _dot(a, b):
    return jnp.dot(a, b, preferred_element_type=F32)


def _dot_nt(a, b):
    return lax.dot_general(a, b, (((1,), (1,)), ((), ())), preferred_element_type=F32)


def _row_chunks(n_rows):
    rc = min(ROW_CHUNK, n_rows)
    return [slice(r0, r0 + rc) for r0 in range(0, n_rows, rc)]


def _rmsnorm_kernel(x_ref, g_ref, o_ref):
    o_ref[...] = _rms(x_ref[...], g_ref[...]).astype(o_ref.dtype)


def _rmsnorm_rows(x2d, g, out_dtype, tm=512):
    T, D = x2d.shape
    tm = min(tm, T)
    return pl.pallas_call(
        _rmsnorm_kernel,
        grid=(T // tm,),
        in_specs=[pl.BlockSpec((tm, D), lambda i: (i, 0)), pl.BlockSpec((1, D), lambda i: (0, 0))],
        out_specs=pl.BlockSpec((tm, D), lambda i: (i, 0)),
        out_shape=jax.ShapeDtypeStruct((T, D), out_dtype),
        compiler_params=_cparams("parallel"),
        name="rmsnorm_rows",
    )(x2d, g.reshape(1, D))


def _cast_block(src_ref, dst_ref, step, n_blocks, n_steps):
    if n_blocks == n_steps:
        dst_ref[...] = src_ref[...].astype(dst_ref.dtype)
    else:

        @pl.when(step < n_blocks)
        def _():
            dst_ref[...] = src_ref[...].astype(dst_ref.dtype)


def _cast_plan(w, n_steps, step_of):
    n_rows = w.shape[0]
    packed_rows = 16
    n_blocks = max(n for n in range(1, n_steps + 1) if n_rows % n == 0 and (n_rows // n) % packed_rows == 0)
    spec = lambda: pl.BlockSpec((n_rows // n_blocks, w.shape[1]),
                                lambda *g: (jnp.minimum(step_of(*g), n_blocks - 1), 0))
    return n_blocks, spec(), spec(), jax.ShapeDtypeStruct(w.shape, BF16)


def _inproj_kernel(a_ref, w_ref, *rest, mode, heads_per_tile, n_scaled_tiles, n_extra, cast_blocks):
    j = pl.program_id(1)
    extra = rest[:n_extra]
    if cast_blocks:
        src_ref, o_ref, dst_ref = rest[n_extra:]
        _cast_block(src_ref, dst_ref, pl.program_id(0) * pl.num_programs(1) + j, cast_blocks[0], cast_blocks[1])
    else:
        (o_ref,) = rest[n_extra:]
    if mode == "rope":
        scale = jnp.where(j < n_scaled_tiles, HEAD_DIM**-0.5, 1.0).astype(F32)
    elif mode == "norm":
        gain = extra[0][0]
    for rows in _row_chunks(a_ref.shape[0]):
        acc = _dot(a_ref[rows, :], w_ref[...])
        for hh in range(heads_per_tile):
            xh = acc[:, hh * HEAD_DIM : (hh + 1) * HEAD_DIM]
            if mode == "rope":
                xh = (xh * extra[0][rows, :] + pltpu.roll(xh, HEAD_DIM // 2, axis=1) * extra[1][rows, :]) * scale
            elif mode == "norm":
                xh = _rms(xh, gain)
            o_ref[0, hh, rows, :] = xh.astype(o_ref.dtype)


def _inproj(xn, w, batch, seq, *, n_col_tiles, col_map, mode, out_dtype, extra=(), extra_specs=(),
            n_scaled_tiles=0, tm=PROJ_TILE_ROWS, tn=PROJ_TILE_COLS, name, cast=None):
    T, D = xn.shape
    tm = min(tm, seq)
    hpt = tn // HEAD_DIM
    tiles_per_seq = seq // tm
    n_steps = (T // tm) * n_col_tiles
    in_specs = [
        pl.BlockSpec((tm, D), lambda i, j: (i, 0)),
        pl.BlockSpec((D, tn), lambda i, j: (0, col_map(j))),
        *extra_specs,
    ]
    out_specs = pl.BlockSpec((1, hpt, tm, HEAD_DIM), lambda i, j: (i // tiles_per_seq, j, i % tiles_per_seq, 0))
    out_shape = jax.ShapeDtypeStruct((batch, n_col_tiles * hpt, seq, HEAD_DIM), out_dtype)
    operands = [xn, w, *extra]
    cast_blocks = None
    if cast is not None:
        n_blocks, src_spec, dst_spec, dst_shape = _cast_plan(cast, n_steps, lambda i, j: i * n_col_tiles + j)
        in_specs.append(src_spec)
        out_specs, out_shape = [out_specs, dst_spec], [out_shape, dst_shape]
        operands.append(cast)
        cast_blocks = (n_blocks, n_steps)
    kern = functools.partial(_inproj_kernel, mode=mode, heads_per_tile=hpt, n_scaled_tiles=n_scaled_tiles,
                             n_extra=len(extra), cast_blocks=cast_blocks)
    return pl.pallas_call(
        kern,
        grid=(T // tm, n_col_tiles),
        in_specs=in_specs,
        out_specs=out_specs,
        out_shape=out_shape,
        compiler_params=_cparams("arbitrary", "arbitrary"),
        name=name,
    )(*operands)


def _retention_kernel(q_ref, k_ref, v_ref, g_ref, dint_ref, kdec_ref, qdec_ref, cdec_ref, gn_ref, o_ref,
                      state_ref, *, n_chunks):
    @pl.when(pl.program_id(2) == 0)
    def _():
        state_ref[...] = jnp.zeros_like(state_ref)

    dint = dint_ref[0]
    kdec = kdec_ref[0]
    qdec = qdec_ref[0]
    cdec = cdec_ref[0]
    gn = gn_ref[...]
    C = RET_CHUNK
    chunks = range(n_chunks)
    rows = [pl.ds(n * C, C) for n in chunks]
    q = [q_ref[0, 0, r, :] for r in rows]
    k = [k_ref[0, 0, r, :] for r in rows]
    scores = [_dot_nt(q[n].astype(BF16), k[n].astype(BF16)) for n in chunks]
    kv = [_dot((k[n] * kdec).T.astype(BF16), v_ref[0, 0, rows[n], :]) for n in chunks]
    states = [state_ref[...]]
    for n in chunks:
        states.append(states[n] * cdec + kv[n])
    state_ref[...] = states[n_chunks]
    out = [
        _dot(
            jnp.concatenate([(scores[n] * dint).astype(BF16), (q[n] * qdec).astype(BF16)], axis=1),
            jnp.concatenate([v_ref[0, 0, rows[n], :], states[n].astype(BF16)], axis=0),
        )
        for n in chunks
    ]
    for n in chunks:
        y = _rms(out[n], gn) * _silu(g_ref[0, 0, rows[n], :])
        o_ref[0, rows[n], :] = y.astype(o_ref.dtype)


def _retention(qk, v_all, gate, tables, gn, *, rows_per_step=8192):
    B, H2, S, d = qk.shape
    H = H2 // 2
    tr = min(rows_per_step, S)
    dint, kdec, qdec, cdec = tables
    C = RET_CHUNK
    kern = functools.partial(_retention_kernel, n_chunks=tr // C)
    head_blk = (1, 1, tr, d)
    return pl.pallas_call(
        kern,
        grid=(B, H, S // tr),
        in_specs=[
            pl.BlockSpec(head_blk, lambda b, h, r: (b, h, r, 0)),
            pl.BlockSpec(head_blk, lambda b, h, r: (b, H + h, r, 0)),
            pl.BlockSpec(head_blk, lambda b, h, r: (b, h, r, 0)),
            pl.BlockSpec(head_blk, lambda b, h, r: (b, h, r, 0)),
            pl.BlockSpec((1, C, C), lambda b, h, r: (h, 0, 0)),
            pl.BlockSpec((1, C, d), lambda b, h, r: (h, 0, 0)),
            pl.BlockSpec((1, C, d), lambda b, h, r: (h, 0, 0)),
            pl.BlockSpec((1, 1, d), lambda b, h, r: (h, 0, 0)),
            pl.BlockSpec((1, d), lambda b, h, r: (0, 0)),
        ],
        out_specs=pl.BlockSpec((1, tr, d), lambda b, h, r: (b, r, h)),
        out_shape=jax.ShapeDtypeStruct((B, S, H * d), BF16),
        scratch_shapes=[pltpu.VMEM((d, d), F32)],
        compiler_params=_cparams("parallel", "parallel", "arbitrary"),
        name="retention",
    )(qk, qk, v_all, gate, dint, kdec, qdec, cdec, gn.reshape(1, d))


def _sb_kernel(q_ref, k_ref, v_ref, w2_ref, gn_ref, o_ref, *, tq, tk, group, n_sub):
    scale = HEAD_DIM**-0.5
    w2 = w2_ref[...]
    gn = gn_ref[...]
    heads = range(group)

    def guard(unit, x):
        mask = unit[4]
        if mask is not None:
            head = [] if x.shape[1] == tk else [x[:, : x.shape[1] - tk]]
            x = jnp.concatenate(head + [jnp.where(mask, x[:, x.shape[1] - tk :], 0.0)], axis=1)
        return x

    def key_rows(unit):
        return pl.ds(unit[2], unit[3] * tk)

    def scan_keys(units):
        newest = lambda x: x[:, x.shape[1] - tk :]
        z2 = [_dot_nt(u[1], k_ref[0, u[0], key_rows(u), :]) * (scale * LOG2_E) for u in units]
        hi, lo = [], []
        for u, z in zip(units, z2):
            sp = guard(u, jnp.maximum(z, 0.0) + jnp.log2(1.0 + jnp.exp2(-jnp.abs(z))))
            h = sp.astype(BF16)
            hi.append(h)
            lo.append((sp - h.astype(F32)).astype(BF16))
        cr_new = [_dot(jnp.concatenate([newest(h), newest(l)], axis=1), w2) for h, l in zip(hi, lo)]
        cr_old = [_dot(jnp.concatenate([h[:, :tk], l[:, :tk]], axis=1), w2) if u[3] == 2 else None
                  for u, h, l in zip(units, hi, lo)]
        return list(zip(z2, cr_new, cr_old))

    def weigh(units, scans, runs, accs):
        a, new_runs = [], []
        for u, (z2, cr_new, cr_old), run in zip(units, scans, runs):
            run_mid = run + cr_new[:, tk:]
            if cr_old is not None:
                behind = jnp.concatenate([cr_old[:, :tk] + run_mid, cr_new[:, :tk] + run], axis=1)
                new_runs.append(run_mid + cr_old[:, tk:])
            else:
                behind = cr_new[:, :tk] + run
                new_runs.append(run_mid)
            a.append(guard(u, jnp.exp2(z2 - behind)).astype(BF16))
        new_accs = []
        for u, w, acc in zip(units, a, accs):
            out = _dot(w, v_ref[0, u[0], key_rows(u), :])
            new_accs.append(acc + (out if u[5] is None else jnp.where(u[5], out, 0.0)))
        return new_runs, new_accs

    mask = lax.broadcasted_iota(jnp.int32, (tk, tk), 1) < lax.broadcasted_iota(jnp.int32, (tk, tk), 0)

    def query_blocks(step, carry):
        subs = [step * SB_BLOCKS_IN_FLIGHT + n for n in range(SB_BLOCKS_IN_FLIGHT)]
        qis = [pl.program_id(2) * n_sub + sub for sub in subs]
        row0s = [pl.multiple_of(sub * tq, tq) for sub in subs]

        def q_rows(n, g, start, size):
            return q_ref[0, g, pl.ds(pl.multiple_of(row0s[n] + start, tk), size), :]

        diag = [[(g, q_rows(n, g, half * tk, tk), pl.multiple_of(qi * tq, tq), half + 1, mask, None)
                 for g in heads for half in range(2)] for n, qi in enumerate(qis)]
        prev = [[(g, q_rows(n, g, 0, tq), pl.multiple_of(jnp.maximum(qi - 1, 0) * tq, tq), 2, None, qi > 0)
                 for g in heads] for n, qi in enumerate(qis)]
        units = [u for n in range(len(qis)) for u in diag[n] + prev[n]]
        scans = scan_keys(units)
        per_block = len(diag[0]) + len(prev[0])
        zero_run, zero_acc = jnp.zeros((tk, tk), F32), jnp.zeros((tk, HEAD_DIM), F32)
        state = []
        for n in range(len(qis)):
            sc = scans[n * per_block : (n + 1) * per_block]
            n_diag = len(diag[n])
            runs, accs = weigh(diag[n], sc[:n_diag], [zero_run] * n_diag, [zero_acc] * n_diag)
            runs = [jnp.concatenate([runs[2 * g], runs[2 * g + 1]], axis=0) for g in heads]
            accs = [jnp.concatenate([accs[2 * g], accs[2 * g + 1]], axis=0) for g in heads]
            state.append(weigh(prev[n], sc[n_diag:], runs, accs))

        def cond(c):
            return jnp.logical_and(c[0] >= 0, jnp.min(functools.reduce(jnp.minimum, c[1])) < SB_EXP2_UNDERFLOW)

        for n, qi in enumerate(qis):

            def body(c, n=n):
                kblk, runs, accs = c
                kstart = pl.multiple_of(kblk * tq, tq)
                older = [(g, q_rows(n, g, 0, tq), kstart, 2, None, None) for g in heads]
                runs, accs = weigh(older, scan_keys(older), runs, accs)
                return kblk - 1, tuple(runs), tuple(accs)

            runs, accs = state[n]
            _, _, accs = lax.while_loop(cond, body, (qi - 2, tuple(runs), tuple(accs)))
            for g in heads:
                o_ref[0, pl.ds(row0s[n], tq), g * HEAD_DIM : (g + 1) * HEAD_DIM] = _rms(accs[g], gn).astype(o_ref.dtype)
        return carry

    lax.fori_loop(0, n_sub // SB_BLOCKS_IN_FLIGHT, query_blocks, 0)


def _stick_breaking(sqk, v_all, gn, *, tk=128, group=4, blocks_per_step=8):
    B, H2, S, d = sqk.shape
    H = H2 // 2
    tq = 2 * tk
    group = min(group, H)
    n_groups = H // group
    idx = jnp.arange(tk)
    w2 = jnp.concatenate([(idx[:, None] >= idx[None, :]).astype(BF16), jnp.ones((tk, tk), BF16)], axis=1)
    w2 = jnp.concatenate([w2, w2], axis=0)
    n_sub = min(blocks_per_step, S // tq)
    assert S % (tq * n_sub) == 0 and n_sub % SB_BLOCKS_IN_FLIGHT == 0
    kern = functools.partial(_sb_kernel, tq=tq, tk=tk, group=group, n_sub=n_sub)
    kv_spec = lambda: pl.BlockSpec((1, group, S, d), lambda b, h, qi: (b, n_groups + h, 0, 0))
    return pl.pallas_call(
        kern,
        grid=(B, n_groups, S // (tq * n_sub)),
        in_specs=[
            pl.BlockSpec((1, group, tq * n_sub, d), lambda b, h, qi: (b, h, qi, 0)),
            kv_spec(),
            kv_spec(),
            pl.BlockSpec((2 * tk, 2 * tk), lambda b, h, qi: (0, 0)),
            pl.BlockSpec((1, d), lambda b, h, qi: (0, 0)),
        ],
        out_specs=pl.BlockSpec((1, tq * n_sub, group * d), lambda b, h, qi: (b, qi, h)),
        out_shape=jax.ShapeDtypeStruct((B, S, H * d), BF16),
        compiler_params=_cparams("parallel", "parallel", "arbitrary"),
        name="stick_breaking",
    )(sqk, sqk, v_all, w2, gn.reshape(1, d))


def _outproj_kernel(a1_ref, a2_ref, w1_ref, w2_ref, x_ref, o_ref):
    for rows in _row_chunks(o_ref.shape[0]):
        acc = _dot(a1_ref[rows, :], w1_ref[...]) + _dot(a2_ref[rows, :], w2_ref[...])
        o_ref[rows, :] = x_ref[rows, :] + acc


def _outproj(ret2d, sb2d, w_out, x2d, *, tm=PROJ_TILE_ROWS, tn=PROJ_TILE_COLS):
    T, K = ret2d.shape
    N = w_out.shape[1]
    tm, tn = min(tm, T), min(tn, N)
    return pl.pallas_call(
        _outproj_kernel,
        grid=(T // tm, N // tn),
        in_specs=[
            pl.BlockSpec((tm, K), lambda i, j: (i, 0)),
            pl.BlockSpec((tm, K), lambda i, j: (i, 0)),
            pl.BlockSpec((K, tn), lambda i, j: (0, j)),
            pl.BlockSpec((K, tn), lambda i, j: (1, j)),
            pl.BlockSpec((tm, tn), lambda i, j: (i, j)),
        ],
        out_specs=pl.BlockSpec((tm, tn), lambda i, j: (i, j)),
        out_shape=jax.ShapeDtypeStruct((T, N), F32),
        compiler_params=_cparams("parallel", "arbitrary"),
        name="outproj",
    )(ret2d, sb2d, w_out, w_out, x2d)


def _memkv_kernel(m_ref, gm_ref, w_ref, gk_ref, k_ref, v_ref):
    mn = _rms(m_ref[0], gm_ref[...]).astype(BF16)
    kv = _dot(mn, w_ref[...])
    cw = kv.shape[1] // 2
    gk = gk_ref[...]
    for hh in range(cw // HEAD_DIM):
        sl = slice(hh * HEAD_DIM, (hh + 1) * HEAD_DIM)
        k_ref[0, :, sl] = _rms(kv[:, sl], gk).astype(k_ref.dtype)
    v_ref[0] = kv[:, cw:].astype(v_ref.dtype)


def _memkv(mem, g_mem, w_kv, g_k):
    B, M, D = mem.shape
    cw = w_kv.shape[1] // 2
    return pl.pallas_call(
        _memkv_kernel,
        grid=(B,),
        in_specs=[
            pl.BlockSpec((1, M, D), lambda b: (b, 0, 0)),
            pl.BlockSpec((1, D), lambda b: (0, 0)),
            pl.BlockSpec((D, 2 * cw), lambda b: (0, 0)),
            pl.BlockSpec((1, HEAD_DIM), lambda b: (0, 0)),
        ],
        out_specs=[pl.BlockSpec((1, M, cw), lambda b: (b, 0, 0))] * 2,
        out_shape=[jax.ShapeDtypeStruct((B, M, cw), BF16)] * 2,
        compiler_params=_cparams("parallel"),
        name="memkv",
    )(mem, g_mem.reshape(1, D), w_kv, g_k.reshape(1, HEAD_DIM))


def _cross_kernel(h_ref, gc_ref, wq_ref, gq_ref, k_ref, v_ref, wo_ref, gf_ref, h2_ref, xn_ref):
    n_rows = h_ref.shape[0]
    part_rows = min(LANES, n_rows)
    parts = [slice(r0, r0 + part_rows) for r0 in range(0, n_rows, part_rows)]
    gq = gq_ref[...]
    scale = HEAD_DIM**-0.5
    h = [h_ref[rows, :] for rows in parts]
    xn = [_rms(hp, gc_ref[...]).astype(BF16) for hp in h]
    q = [_dot(x, wq_ref[...]) for x in xn]
    head_cols = [slice(c0, c0 + HEAD_DIM) for c0 in range(0, wq_ref.shape[1], HEAD_DIM)]
    s = [[_dot_nt(_rms(qp[:, sl], gq).astype(BF16), k_ref[0, :, sl]) * scale for sl in head_cols] for qp in q]
    o = []
    for sp in s:
        heads_out = []
        for sh, sl in zip(sp, head_cols):
            e = jnp.exp(sh - jnp.max(sh, axis=-1, keepdims=True))
            p = e / jnp.sum(e, axis=-1, keepdims=True)
            heads_out.append(_dot(p.astype(BF16), v_ref[0, :, sl]).astype(BF16))
        o.append(jnp.concatenate(heads_out, axis=-1))
    h2 = [hp + _dot(op, wo_ref[...]) for hp, op in zip(h, o)]
    for rows, h2p in zip(parts, h2):
        h2_ref[rows, :] = h2p
        xn_ref[rows, :] = _rms(h2p, gf_ref[...]).astype(xn_ref.dtype)


def _cross_attention(h2d, seq, g_cross, w_q, g_q, kmem, vmem, w_o, g_ffn, *, tm=256):
    T, D = h2d.shape
    _, M, cw = kmem.shape
    tm = min(tm, seq)
    tiles_per_seq = seq // tm
    const = lambda i: (0, 0)
    return pl.pallas_call(
        _cross_kernel,
        grid=(T // tm,),
        in_specs=[
            pl.BlockSpec((tm, D), lambda i: (i, 0)),
            pl.BlockSpec((1, D), const),
            pl.BlockSpec((D, cw), const),
            pl.BlockSpec((1, HEAD_DIM), const),
            pl.BlockSpec((1, M, cw), lambda i: (i // tiles_per_seq, 0, 0)),
            pl.BlockSpec((1, M, cw), lambda i: (i // tiles_per_seq, 0, 0)),
            pl.BlockSpec((cw, D), const),
            pl.BlockSpec((1, D), const),
        ],
        out_specs=[pl.BlockSpec((tm, D), lambda i: (i, 0))] * 2,
        out_shape=[jax.ShapeDtypeStruct((T, D), F32), jax.ShapeDtypeStruct((T, D), BF16)],
        compiler_params=_cparams("parallel"),
        name="cross_attention",
    )(h2d, g_cross.reshape(1, D), w_q, g_q.reshape(1, HEAD_DIM), kmem, vmem, w_o, g_ffn.reshape(1, D))


CONV_PIECE_ROWS = 64


def _ffn_up_kernel(a_ref, wg_ref, wv_ref, cwg_ref, cwv_ref, cbg_ref, cbv_ref, src_ref, o_ref, dst_ref, ubuf_ref,
                   carry_ref, *, tiles_per_seq, cast_blocks):
    i = pl.program_id(0)
    j = pl.program_id(1)

    @pl.when(jnp.logical_and(i == 0, j == 0))
    def _():
        carry_ref[...] = jnp.zeros_like(carry_ref)

    _cast_block(src_ref, dst_ref, i * pl.num_programs(1) + j, cast_blocks[0], cast_blocks[1])
    seq_start = (i % tiles_per_seq) == 0
    halo = SUBLANES
    chunks = _row_chunks(a_ref.shape[0])
    size = lambda c: chunks[c].stop - chunks[c].start
    lane_tiles = range(o_ref.shape[1] // LANES)
    projections = ((wg_ref, cwg_ref, cbg_ref), (wv_ref, cwv_ref, cbv_ref))

    def project(c):
        par = c % 2
        a = a_ref[chunks[c], :]
        for slot, (w_ref, _, _) in enumerate(projections):
            u = _dot(a, w_ref[...])
            for lt in lane_tiles:
                if c == 0:
                    ubuf_ref[par, slot, lt, 0:halo, :] = jnp.where(seq_start, 0.0, carry_ref[j, slot, lt])
                else:
                    ubuf_ref[par, slot, lt, 0:halo, :] = ubuf_ref[1 - par, slot, lt, size(c - 1) : size(c - 1) + halo, :]
                ubuf_ref[par, slot, lt, halo : halo + size(c), :] = u[:, lt * LANES : (lt + 1) * LANES]

    def activate(c):
        par = c % 2
        piece = min(CONV_PIECE_ROWS, size(c))
        for lt in lane_tiles:
            cols = slice(lt * LANES, (lt + 1) * LANES)
            for p0 in range(0, size(c), piece):

                def conv(slot):
                    _, cw_ref, cb_ref = projections[slot]
                    cw = cw_ref[:, cols]
                    c2 = cb_ref[:, cols] + ubuf_ref[par, slot, lt, pl.ds(p0 + halo - 2, piece), :] * cw[0:1, :]
                    c2 = c2 + ubuf_ref[par, slot, lt, pl.ds(p0 + halo - 1, piece), :] * cw[1:2, :]
                    return c2 + ubuf_ref[par, slot, lt, pl.ds(p0 + halo, piece), :] * cw[2:3, :]

                out_rows = slice(chunks[c].start + p0, chunks[c].start + p0 + piece)
                o_ref[out_rows, cols] = (_silu(conv(0)) * conv(1)).astype(o_ref.dtype)

    project(0)
    for c in range(1, len(chunks)):
        project(c)
        activate(c - 1)
    activate(len(chunks) - 1)
    last = len(chunks) - 1
    for slot in range(2):
        for lt in lane_tiles:
            carry_ref[j, slot, lt] = ubuf_ref[last % 2, slot, lt, size(last) : size(last) + halo, :]


def _ffn_up(xn, seq, w_up, conv_w, conv_b, cast, *, tm=2048, tf=256):
    T, D = xn.shape
    F = w_up.shape[1] // 2
    tm = min(tm, seq)
    assert F % tf == 0 and seq % tm == 0
    n_j = F // tf
    n_steps = (T // tm) * n_j
    n_blocks, src_spec, dst_spec, dst_shape = _cast_plan(cast, n_steps, lambda i, j: i * n_j + j)
    kern = functools.partial(_ffn_up_kernel, tiles_per_seq=seq // tm, cast_blocks=(n_blocks, n_steps))
    gate_col = lambda i, j: (0, j)
    value_col = lambda i, j: (0, n_j + j)
    return pl.pallas_call(
        kern,
        grid=(T // tm, n_j),
        in_specs=[
            pl.BlockSpec((tm, D), lambda i, j: (i, 0)),
            pl.BlockSpec((D, tf), gate_col),
            pl.BlockSpec((D, tf), value_col),
            pl.BlockSpec((CONV_WIDTH, tf), gate_col),
            pl.BlockSpec((CONV_WIDTH, tf), value_col),
            pl.BlockSpec((1, tf), gate_col),
            pl.BlockSpec((1, tf), value_col),
            src_spec,
        ],
        out_specs=[pl.BlockSpec((tm, tf), lambda i, j: (i, j)), dst_spec],
        out_shape=[jax.ShapeDtypeStruct((T, F), BF16), dst_shape],
        scratch_shapes=[
            pltpu.VMEM((2, 2, tf // LANES, SUBLANES + min(ROW_CHUNK, tm), LANES), F32),
            pltpu.VMEM((n_j, 2, tf // LANES, SUBLANES, LANES), F32),
        ],
        compiler_params=_cparams("arbitrary", "arbitrary"),
        name="ffn_up",
    )(xn, w_up, w_up, conv_w, conv_w, conv_b, conv_b, cast)


def _ffn_down_kernel(a_ref, w_ref, h_ref, o_ref):
    for rows in _row_chunks(o_ref.shape[0]):
        o_ref[rows, :] = h_ref[rows, :] + _dot(a_ref[rows, :], w_ref[...])


def _ffn_down(act, w_down, h2d, *, tm=512, tn=1024):
    T, F = act.shape
    N = w_down.shape[1]
    tm, tn = min(tm, T), min(tn, N)
    return pl.pallas_call(
        _ffn_down_kernel,
        grid=(N // tn, T // tm),
        in_specs=[
            pl.BlockSpec((tm, F), lambda j, i: (i, 0)),
            pl.BlockSpec((F, tn), lambda j, i: (0, j), pipeline_mode=pl.Buffered(1)),
            pl.BlockSpec((tm, tn), lambda j, i: (i, j)),
        ],
        out_specs=pl.BlockSpec((tm, tn), lambda j, i: (i, j)),
        out_shape=jax.ShapeDtypeStruct((T, N), F32),
        compiler_params=_cparams("parallel", "arbitrary"),
        name="ffn_down",
    )(act, w_down, h2d)


def _rope_tables(S):
    inv_freq = ROPE_BASE ** (-jnp.linspace(0.0, 1.0, HEAD_DIM // 2, dtype=F32))
    ang = jnp.arange(S, dtype=F32)[:, None] * inv_freq[None, :]
    cos, sin = jnp.cos(ang), jnp.sin(ang)
    return jnp.concatenate([cos, cos], axis=-1), jnp.concatenate([-sin, sin], axis=-1)


def _retention_tables(H):
    C = RET_CHUNK
    log_g = jnp.log1p(-jnp.exp2(-5.0 - jnp.arange(H, dtype=F32)))
    idx = jnp.arange(C, dtype=F32)
    diff = idx[:, None] - idx[None, :]
    dint = jnp.where(diff >= 0, jnp.exp(jnp.maximum(diff, 0.0)[None] * log_g[:, None, None]), 0.0)
    kdec = jnp.exp((C - 1 - idx)[None, :] * log_g[:, None])
    qdec = jnp.exp((idx + 1.0)[None, :] * log_g[:, None])
    cdec = jnp.exp(C * log_g)
    lanes = lambda t: jnp.broadcast_to(t[..., None], t.shape + (HEAD_DIM,))
    return dint, lanes(kdec), lanes(qdec), lanes(cdec[:, None])


def _layer(h, mem, attn_norm, w_in, ret_norm, sb_q_norm, sb_k_norm, sb_out_norm, w_out, cross_norm, mem_norm,
           cross_w_q, cross_w_kv, cross_q_norm, cross_k_norm, cross_w_o, ffn_norm, ffn_w_up, ffn_conv_w,
           ffn_conv_b, ffn_w_down):
    B, S, D = h.shape
    T = B * S
    W = D // 2
    H = W // HEAD_DIM
    tn = min(PROJ_TILE_COLS, W)
    seg = W // tn
    x2d = h.reshape(T, D)

    xn = _rmsnorm_rows(x2d, attn_norm, BF16)
    w_in_b = w_in.astype(BF16)
    cos2, sin2 = _rope_tables(S)
    tm_in = min(PROJ_TILE_ROWS, S)
    tiles_per_seq = S // tm_in
    rope_spec = pl.BlockSpec((tm_in, HEAD_DIM), lambda i, j: (i % tiles_per_seq, 0))
    ret_qk = _inproj(xn, w_in_b, B, S, n_col_tiles=2 * seg, col_map=lambda j: j, mode="rope", out_dtype=F32,
                     extra=(cos2, sin2), extra_specs=(rope_spec, rope_spec), n_scaled_tiles=seg, tn=tn,
                     name="inproj_ret_qk")
    v_all, w_up_b = _inproj(xn, w_in_b, B, S, n_col_tiles=2 * seg,
                            col_map=lambda j: jnp.where(j < seg, 2 * seg + j, 5 * seg + j), mode="plain",
                            out_dtype=BF16, tn=tn, name="inproj_v", cast=ffn_w_up)
    gate, w_out_b = _inproj(xn, w_in_b, B, S, n_col_tiles=seg, col_map=lambda j: 3 * seg + j, mode="plain",
                            out_dtype=F32, tn=tn, name="inproj_gate", cast=w_out)
    sb_gains = jnp.stack([sb_q_norm, sb_k_norm]).reshape(2, 1, HEAD_DIM)
    gain_spec = pl.BlockSpec((1, 1, HEAD_DIM), lambda i, j: (j // seg, 0, 0))
    sb_qk = _inproj(xn, w_in_b, B, S, n_col_tiles=2 * seg, col_map=lambda j: 4 * seg + j, mode="norm",
                    out_dtype=BF16, extra=(sb_gains,), extra_specs=(gain_spec,), tn=tn, name="inproj_sb_qk")

    ret = _retention(ret_qk, v_all, gate, _retention_tables(H), ret_norm)
    sb = _stick_breaking(sb_qk, v_all, sb_out_norm)
    h1 = _outproj(ret.reshape(T, W), sb.reshape(T, W), w_out_b, x2d)

    kmem, vmem = _memkv(mem, mem_norm, cross_w_kv.astype(BF16), cross_k_norm)
    h2, xn_ffn = _cross_attention(h1, S, cross_norm, cross_w_q.astype(BF16), cross_q_norm, kmem, vmem,
                                  cross_w_o.astype(BF16), ffn_norm)

    act, w_down_b = _ffn_up(xn_ffn, S, w_up_b, ffn_conv_w, ffn_conv_b.reshape(1, -1), ffn_w_down)
    out = _ffn_down(act, w_down_b, h2)
    return out.reshape(B, S, D)


def kernel(x, mem, attn_norm, w_in, ret_norm, sb_q_norm, sb_k_norm, sb_out_norm, w_out, cross_norm, mem_norm,
           cross_w_q, cross_w_kv, cross_q_norm, cross_k_norm, cross_w_o, ffn_norm, ffn_w_up, ffn_conv_w,
           ffn_conv_b, ffn_w_down):
    h = x
    for l in range(attn_norm.shape[0]):
        h = _layer(h, mem, attn_norm[l], w_in[l], ret_norm[l], sb_q_norm[l], sb_k_norm[l], sb_out_norm[l],
                   w_out[l], cross_norm[l], mem_norm[l], cross_w_q[l], cross_w_kv[l], cross_q_norm[l],
                   cross_k_norm[l], cross_w_o[l], ffn_norm[l], ffn_w_up[l], ffn_conv_w[l], ffn_conv_b[l],
                   ffn_w_down[l])
    return h
```

```python
import functools

import jax
import jax.numpy as jnp
from jax import lax
from jax.experimental import pallas as pl
from jax.experimental.pallas import tpu as pltpu

HEAD_DIM = 128
EPS = 1e-6
ROPE_BASE = 10000.0
RET_CHUNK = 128
CONV_WIDTH = 3
SUBLANES = 8
LANES = 128

VMEM_LIMIT_BYTES = 60 * 1024 * 1024

PROJ_TILE_ROWS = 1024
PROJ_TILE_COLS = 1024

ROW_CHUNK = 256

SB_EXP2_UNDERFLOW = 150.0
LOG2_E = 1.4426950408889634
SB_BLOCKS_IN_FLIGHT = 4

F32 = jnp.float32
BF16 = jnp.bfloat16


def _cparams(*sem):
    return pltpu.CompilerParams(dimension_semantics=sem, vmem_limit_bytes=VMEM_LIMIT_BYTES)


def _rms(x, g):
    return x * lax.rsqrt(jnp.mean(x * x, axis=-1, keepdims=True) + EPS) * g


def _silu(x):
    return x * (1.0 / (1.0 + jnp.exp(-x)))


def _dot(a, b):
    return jnp.dot(a, b, preferred_element_type=F32)


def _dot_nt(a, b):
    return lax.dot_general(a, b, (((1,), (1,)), ((), ())), preferred_element_type=F32)


def _row_chunks(n_rows):
    rc = min(ROW_CHUNK, n_rows)
    return [slice(r0, r0 + rc) for r0 in range(0, n_rows, rc)]


def _rmsnorm_kernel(x_ref, g_ref, o_ref):
    o_ref[...] = _rms(x_ref[...], g_ref[...]).astype(o_ref.dtype)


def _rmsnorm_rows(x2d, g, out_dtype, tm=512):
    T, D = x2d.shape
    tm = min(tm, T)
    return pl.pallas_call(
        _rmsnorm_kernel,
        grid=(T // tm,),
        in_specs=[pl.BlockSpec((tm, D), lambda i: (i, 0)), pl.BlockSpec((1, D), lambda i: (0, 0))],
        out_specs=pl.BlockSpec((tm, D), lambda i: (i, 0)),
        out_shape=jax.ShapeDtypeStruct((T, D), out_dtype),
        compiler_params=_cparams("parallel"),
        name="rmsnorm_rows",
    )(x2d, g.reshape(1, D))


def _cast_block(src_ref, dst_ref, step, n_blocks, n_steps):
    if n_blocks == n_steps:
        dst_ref[...] = src_ref[...].astype(dst_ref.dtype)
    else:

        @pl.when(step < n_blocks)
        def _():
            dst_ref[...] = src_ref[...].astype(dst_ref.dtype)


def _cast_plan(w, n_steps, step_of):
    n_rows = w.shape[0]
    packed_rows = 16
    n_blocks = max(n for n in range(1, n_steps + 1) if n_rows % n == 0 and (n_rows // n) % packed_rows == 0)
    spec = lambda: pl.BlockSpec((n_rows // n_blocks, w.shape[1]),
                                lambda *g: (jnp.minimum(step_of(*g), n_blocks - 1), 0))
    return n_blocks, spec(), spec(), jax.ShapeDtypeStruct(w.shape, BF16)


def _inproj_kernel(a_ref, w_ref, *rest, mode, heads_per_tile, n_scaled_tiles, n_extra, cast_blocks):
    j = pl.program_id(1)
    extra = rest[:n_extra]
    if cast_blocks:
        src_ref, o_ref, dst_ref = rest[n_extra:]
        _cast_block(src_ref, dst_ref, pl.program_id(0) * pl.num_programs(1) + j, cast_blocks[0], cast_blocks[1])
    else:
        (o_ref,) = rest[n_extra:]
    if mode == "rope":
        scale = jnp.where(j < n_scaled_tiles, HEAD_DIM**-0.5, 1.0).astype(F32)
    elif mode == "norm":
        gain = extra[0][0]
    for rows in _row_chunks(a_ref.shape[0]):
        acc = _dot(a_ref[rows, :], w_ref[...])
        for hh in range(heads_per_tile):
            xh = acc[:, hh * HEAD_DIM : (hh + 1) * HEAD_DIM]
            if mode == "rope":
                xh = (xh * extra[0][rows, :] + pltpu.roll(xh, HEAD_DIM // 2, axis=1) * extra[1][rows, :]) * scale
            elif mode == "norm":
                xh = _rms(xh, gain)
            o_ref[0, hh, rows, :] = xh.astype(o_ref.dtype)


def _inproj(xn, w, batch, seq, *, n_col_tiles, col_map, mode, out_dtype, extra=(), extra_specs=(),
            n_scaled_tiles=0, tm=PROJ_TILE_ROWS, tn=PROJ_TILE_COLS, name, cast=None):
    T, D = xn.shape
    tm = min(tm, seq)
    hpt = tn // HEAD_DIM
    tiles_per_seq = seq // tm
    n_steps = (T // tm) * n_col_tiles
    in_specs = [
        pl.BlockSpec((tm, D), lambda i, j: (i, 0)),
        pl.BlockSpec((D, tn), lambda i, j: (0, col_map(j))),
        *extra_specs,
    ]
    out_specs = pl.BlockSpec((1, hpt, tm, HEAD_DIM), lambda i, j: (i // tiles_per_seq, j, i % tiles_per_seq, 0))
    out_shape = jax.ShapeDtypeStruct((batch, n_col_tiles * hpt, seq, HEAD_DIM), out_dtype)
    operands = [xn, w, *extra]
    cast_blocks = None
    if cast is not None:
        n_blocks, src_spec, dst_spec, dst_shape = _cast_plan(cast, n_steps, lambda i, j: i * n_col_tiles + j)
        in_specs.append(src_spec)
        out_specs, out_shape = [out_specs, dst_spec], [out_shape, dst_shape]
        operands.append(cast)
        cast_blocks = (n_blocks, n_steps)
    kern = functools.partial(_inproj_kernel, mode=mode, heads_per_tile=hpt, n_scaled_tiles=n_scaled_tiles,
                             n_extra=len(extra), cast_blocks=cast_blocks)
    return pl.pallas_call(
        kern,
        grid=(T // tm, n_col_tiles),
        in_specs=in_specs,
        out_specs=out_specs,
        out_shape=out_shape,
        compiler_params=_cparams("arbitrary", "arbitrary"),
        name=name,
    )(*operands)


def _retention_kernel(q_ref, k_ref, v_ref, g_ref, dint_ref, kdec_ref, qdec_ref, cdec_ref, gn_ref, o_ref,
                      state_ref, *, n_chunks):
    @pl.when(pl.program_id(2) == 0)
    def _():
        state_ref[...] = jnp.zeros_like(state_ref)

    dint = dint_ref[0]
    kdec = kdec_ref[0]
    qdec = qdec_ref[0]
    cdec = cdec_ref[0]
    gn = gn_ref[...]
    C = RET_CHUNK
    chunks = range(n_chunks)
    rows = [pl.ds(n * C, C) for n in chunks]
    q = [q_ref[0, 0, r, :] for r in rows]
    k = [k_ref[0, 0, r, :] for r in rows]
    scores = [_dot_nt(q[n].astype(BF16), k[n].astype(BF16)) for n in chunks]
    kv = [_dot((k[n] * kdec).T.astype(BF16), v_ref[0, 0, rows[n], :]) for n in chunks]
    states = [state_ref[...]]
    for n in chunks:
        states.append(states[n] * cdec + kv[n])
    state_ref[...] = states[n_chunks]
    out = [
        _dot(
            jnp.concatenate([(scores[n] * dint).astype(BF16), (q[n] * qdec).astype(BF16)], axis=1),
            jnp.concatenate([v_ref[0, 0, rows[n], :], states[n].astype(BF16)], axis=0),
        )
        for n in chunks
    ]
    for n in chunks:
        y = _rms(out[n], gn) * _silu(g_ref[0, 0, rows[n], :])
        o_ref[0, rows[n], :] = y.astype(o_ref.dtype)


def _retention(qk, v_all, gate, tables, gn, *, rows_per_step=8192):
    B, H2, S, d = qk.shape
    H = H2 // 2
    tr = min(rows_per_step, S)
    dint, kdec, qdec, cdec = tables
    C = RET_CHUNK
    kern = functools.partial(_retention_kernel, n_chunks=tr // C)
    head_blk = (1, 1, tr, d)
    return pl.pallas_call(
        kern,
        grid=(B, H, S // tr),
        in_specs=[
            pl.BlockSpec(head_blk, lambda b, h, r: (b, h, r, 0)),
            pl.BlockSpec(head_blk, lambda b, h, r: (b, H + h, r, 0)),
            pl.BlockSpec(head_blk, lambda b, h, r: (b, h, r, 0)),
            pl.BlockSpec(head_blk, lambda b, h, r: (b, h, r, 0)),
            pl.BlockSpec((1, C, C), lambda b, h, r: (h, 0, 0)),
            pl.BlockSpec((1, C, d), lambda b, h, r: (h, 0, 0)),
            pl.BlockSpec((1, C, d), lambda b, h, r: (h, 0, 0)),
            pl.BlockSpec((1, 1, d), lambda b, h, r: (h, 0, 0)),
            pl.BlockSpec((1, d), lambda b, h, r: (0, 0)),
        ],
        out_specs=pl.BlockSpec((1, tr, d), lambda b, h, r: (b, r, h)),
        out_shape=jax.ShapeDtypeStruct((B, S, H * d), BF16),
        scratch_shapes=[pltpu.VMEM((d, d), F32)],
        compiler_params=_cparams("parallel", "parallel", "arbitrary"),
        name="retention",
    )(qk, qk, v_all, gate, dint, kdec, qdec, cdec, gn.reshape(1, d))


def _sb_kernel(q_ref, k_ref, v_ref, w2_ref, gn_ref, o_ref, *, tq, tk, group, n_sub):
    scale = HEAD_DIM**-0.5
    w2 = w2_ref[...]
    gn = gn_ref[...]
    heads = range(group)

    def guard(unit, x):
        mask = unit[4]
        if mask is not None:
            head = [] if x.shape[1] == tk else [x[:, : x.shape[1] - tk]]
            x = jnp.concatenate(head + [jnp.where(mask, x[:, x.shape[1] - tk :], 0.0)], axis=1)
        return x

    def key_rows(unit):
        return pl.ds(unit[2], unit[3] * tk)

    def scan_keys(units):
        newest = lambda x: x[:, x.shape[1] - tk :]
        z2 = [_dot_nt(u[1], k_ref[0, u[0], key_rows(u), :]) * (scale * LOG2_E) for u in units]
        hi, lo = [], []
        for u, z in zip(units, z2):
            sp = guard(u, jnp.maximum(z, 0.0) + jnp.log2(1.0 + jnp.exp2(-jnp.abs(z))))
            h = sp.astype(BF16)
            hi.append(h)
            lo.append((sp - h.astype(F32)).astype(BF16))
        cr_new = [_dot(jnp.concatenate([newest(h), newest(l)], axis=1), w2) for h, l in zip(hi, lo)]
        cr_old = [_dot(jnp.concatenate([h[:, :tk], l[:, :tk]], axis=1), w2) if u[3] == 2 else None
                  for u, h, l in zip(units, hi, lo)]
        return list(zip(z2, cr_new, cr_old))

    def weigh(units, scans, runs, accs):
        a, new_runs = [], []
        for u, (z2, cr_new, cr_old), run in zip(units, scans, runs):
            run_mid = run + cr_new[:, tk:]
            if cr_old is not None:
                behind = jnp.concatenate([cr_old[:, :tk] + run_mid, cr_new[:, :tk] + run], axis=1)
                new_runs.append(run_mid + cr_old[:, tk:])
            else:
                behind = cr_new[:, :tk] + run
                new_runs.append(run_mid)
            a.append(guard(u, jnp.exp2(z2 - behind)).astype(BF16))
        new_accs = []
        for u, w, acc in zip(units, a, accs):
            out = _dot(w, v_ref[0, u[0], key_rows(u), :])
            new_accs.append(acc + (out if u[5] is None else jnp.where(u[5], out, 0.0)))
        return new_runs, new_accs

    mask = lax.broadcasted_iota(jnp.int32, (tk, tk), 1) < lax.broadcasted_iota(jnp.int32, (tk, tk), 0)

    def query_blocks(step, carry):
        subs = [step * SB_BLOCKS_IN_FLIGHT + n for n in range(SB_BLOCKS_IN_FLIGHT)]
        qis = [pl.program_id(2) * n_sub + sub for sub in subs]
        row0s = [pl.multiple_of(sub * tq, tq) for sub in subs]

        def q_rows(n, g, start, size):
            return q_ref[0, g, pl.ds(pl.multiple_of(row0s[n] + start, tk), size), :]

        diag = [[(g, q_rows(n, g, half * tk, tk), pl.multiple_of(qi * tq, tq), half + 1, mask, None)
                 for g in heads for half in range(2)] for n, qi in enumerate(qis)]
        prev = [[(g, q_rows(n, g, 0, tq), pl.multiple_of(jnp.maximum(qi - 1, 0) * tq, tq), 2, None, qi > 0)
                 for g in heads] for n, qi in enumerate(qis)]
        units = [u for n in range(len(qis)) for u in diag[n] + prev[n]]
        scans = scan_keys(units)
        per_block = len(diag[0]) + len(prev[0])
        zero_run, zero_acc = jnp.zeros((tk, tk), F32), jnp.zeros((tk, HEAD_DIM), F32)
        state = []
        for n in range(len(qis)):
            sc = scans[n * per_block : (n + 1) * per_block]
            n_diag = len(diag[n])
            runs, accs = weigh(diag[n], sc[:n_diag], [zero_run] * n_diag, [zero_acc] * n_diag)
            runs = [jnp.concatenate([runs[2 * g], runs[2 * g + 1]], axis=0) for g in heads]
            accs = [jnp.concatenate([accs[2 * g], accs[2 * g + 1]], axis=0) for g in heads]
            state.append(weigh(prev[n], sc[n_diag:], runs, accs))

        def cond(c):
            return jnp.logical_and(c[0] >= 0, jnp.min(functools.reduce(jnp.minimum, c[1])) < SB_EXP2_UNDERFLOW)

        for n, qi in enumerate(qis):

            def body(c, n=n):
                kblk, runs, accs = c
                kstart = pl.multiple_of(kblk * tq, tq)
                older = [(g, q_rows(n, g, 0, tq), kstart, 2, None, None) for g in heads]
                runs, accs = weigh(older, scan_keys(older), runs, accs)
                return kblk - 1, tuple(runs), tuple(accs)

            runs, accs = state[n]
            _, _, accs = lax.while_loop(cond, body, (qi - 2, tuple(runs), tuple(accs)))
            for g in heads:
                o_ref[0, pl.ds(row0s[n], tq), g * HEAD_DIM : (g + 1) * HEAD_DIM] = _rms(accs[g], gn).astype(o_ref.dtype)
        return carry

    lax.fori_loop(0, n_sub // SB_BLOCKS_IN_FLIGHT, query_blocks, 0)


def _stick_breaking(sqk, v_all, gn, *, tk=128, group=4, blocks_per_step=8):
    B, H2, S, d = sqk.shape
    H = H2 // 2
    tq = 2 * tk
    group = min(group, H)
    n_groups = H // group
    idx = jnp.arange(tk)
    w2 = jnp.concatenate([(idx[:, None] >= idx[None, :]).astype(BF16), jnp.ones((tk, tk), BF16)], axis=1)
    w2 = jnp.concatenate([w2, w2], axis=0)
    n_sub = min(blocks_per_step, S // tq)
    assert S % (tq * n_sub) == 0 and n_sub % SB_BLOCKS_IN_FLIGHT == 0
    kern = functools.partial(_sb_kernel, tq=tq, tk=tk, group=group, n_sub=n_sub)
    kv_spec = lambda: pl.BlockSpec((1, group, S, d), lambda b, h, qi: (b, n_groups + h, 0, 0))
    return pl.pallas_call(
        kern,
        grid=(B, n_groups, S // (tq * n_sub)),
        in_specs=[
            pl.BlockSpec((1, group, tq * n_sub, d), lambda b, h, qi: (b, h, qi, 0)),
            kv_spec(),
            kv_spec(),
            pl.BlockSpec((2 * tk, 2 * tk), lambda b, h, qi: (0, 0)),
            pl.BlockSpec((1, d), lambda b, h, qi: (0, 0)),
        ],
        out_specs=pl.BlockSpec((1, tq * n_sub, group * d), lambda b, h, qi: (b, qi, h)),
        out_shape=jax.ShapeDtypeStruct((B, S, H * d), BF16),
        compiler_params=_cparams("parallel", "parallel", "arbitrary"),
        name="stick_breaking",
    )(sqk, sqk, v_all, w2, gn.reshape(1, d))


def _outproj_kernel(a1_ref, a2_ref, w1_ref, w2_ref, x_ref, o_ref):
    for rows in _row_chunks(o_ref.shape[0]):
        acc = _dot(a1_ref[rows, :], w1_ref[...]) + _dot(a2_ref[rows, :], w2_ref[...])
        o_ref[rows, :] = x_ref[rows, :] + acc


def _outproj(ret2d, sb2d, w_out, x2d, *, tm=PROJ_TILE_ROWS, tn=PROJ_TILE_COLS):
    T, K = ret2d.shape
    N = w_out.shape[1]
    tm, tn = min(tm, T), min(tn, N)
    return pl.pallas_call(
        _outproj_kernel,
        grid=(T // tm, N // tn),
        in_specs=[
            pl.BlockSpec((tm, K), lambda i, j: (i, 0)),
            pl.BlockSpec((tm, K), lambda i, j: (i, 0)),
            pl.BlockSpec((K, tn), lambda i, j: (0, j)),
            pl.BlockSpec((K, tn), lambda i, j: (1, j)),
            pl.BlockSpec((tm, tn), lambda i, j: (i, j)),
        ],
        out_specs=pl.BlockSpec((tm, tn), lambda i, j: (i, j)),
        out_shape=jax.ShapeDtypeStruct((T, N), F32),
        compiler_params=_cparams("parallel", "arbitrary"),
        name="outproj",
    )(ret2d, sb2d, w_out, w_out, x2d)


def _memkv_kernel(m_ref, gm_ref, w_ref, gk_ref, k_ref, v_ref):
    mn = _rms(m_ref[0], gm_ref[...]).astype(BF16)
    kv = _dot(mn, w_ref[...])
    cw = kv.shape[1] // 2
    gk = gk_ref[...]
    for hh in range(cw // HEAD_DIM):
        sl = slice(hh * HEAD_DIM, (hh + 1) * HEAD_DIM)
        k_ref[0, :, sl] = _rms(kv[:, sl], gk).astype(k_ref.dtype)
    v_ref[0] = kv[:, cw:].astype(v_ref.dtype)


def _memkv(mem, g_mem, w_kv, g_k):
    B, M, D = mem.shape
    cw = w_kv.shape[1] // 2
    return pl.pallas_call(
        _memkv_kernel,
        grid=(B,),
        in_specs=[
            pl.BlockSpec((1, M, D), lambda b: (b, 0, 0)),
            pl.BlockSpec((1, D), lambda b: (0, 0)),
            pl.BlockSpec((D, 2 * cw), lambda b: (0, 0)),
            pl.BlockSpec((1, HEAD_DIM), lambda b: (0, 0)),
        ],
        out_specs=[pl.BlockSpec((1, M, cw), lambda b: (b, 0, 0))] * 2,
        out_shape=[jax.ShapeDtypeStruct((B, M, cw), BF16)] * 2,
        compiler_params=_cparams("parallel"),
        name="memkv",
    )(mem, g_mem.reshape(1, D), w_kv, g_k.reshape(1, HEAD_DIM))


def _cross_kernel(h_ref, gc_ref, wq_ref, gq_ref, k_ref, v_ref, wo_ref, gf_ref, h2_ref, xn_ref):
    n_rows = h_ref.shape[0]
    part_rows = min(LANES, n_rows)
    parts = [slice(r0, r0 + part_rows) for r0 in range(0, n_rows, part_rows)]
    gq = gq_ref[...]
    scale = HEAD_DIM**-0.5
    h = [h_ref[rows, :] for rows in parts]
    xn = [_rms(hp, gc_ref[...]).astype(BF16) for hp in h]
    q = [_dot(x, wq_ref[...]) for x in xn]
    head_cols = [slice(c0, c0 + HEAD_DIM) for c0 in range(0, wq_ref.shape[1], HEAD_DIM)]
    s = [[_dot_nt(_rms(qp[:, sl], gq).astype(BF16), k_ref[0, :, sl]) * scale for sl in head_cols] for qp in q]
    o = []
    for sp in s:
        heads_out = []
        for sh, sl in zip(sp, head_cols):
            e = jnp.exp(sh - jnp.max(sh, axis=-1, keepdims=True))
            p = e / jnp.sum(e, axis=-1, keepdims=True)
            heads_out.append(_dot(p.astype(BF16), v_ref[0, :, sl]).astype(BF16))
        o.append(jnp.concatenate(heads_out, axis=-1))
    h2 = [hp + _dot(op, wo_ref[...]) for hp, op in zip(h, o)]
    for rows, h2p in zip(parts, h2):
        h2_ref[rows, :] = h2p
        xn_ref[rows, :] = _rms(h2p, gf_ref[...]).astype(xn_ref.dtype)


def _cross_attention(h2d, seq, g_cross, w_q, g_q, kmem, vmem, w_o, g_ffn, *, tm=256):
    T, D = h2d.shape
    _, M, cw = kmem.shape
    tm = min(tm, seq)
    tiles_per_seq = seq // tm
    const = lambda i: (0, 0)
    return pl.pallas_call(
        _cross_kernel,
        grid=(T // tm,),
        in_specs=[
            pl.BlockSpec((tm, D), lambda i: (i, 0)),
            pl.BlockSpec((1, D), const),
            pl.BlockSpec((D, cw), const),
            pl.BlockSpec((1, HEAD_DIM), const),
            pl.BlockSpec((1, M, cw), lambda i: (i // tiles_per_seq, 0, 0)),
            pl.BlockSpec((1, M, cw), lambda i: (i // tiles_per_seq, 0, 0)),
            pl.BlockSpec((cw, D), const),
            pl.BlockSpec((1, D), const),
        ],
        out_specs=[pl.BlockSpec((tm, D), lambda i: (i, 0))] * 2,
        out_shape=[jax.ShapeDtypeStruct((T, D), F32), jax.ShapeDtypeStruct((T, D), BF16)],
        compiler_params=_cparams("parallel"),
        name="cross_attention",
    )(h2d, g_cross.reshape(1, D), w_q, g_q.reshape(1, HEAD_DIM), kmem, vmem, w_o, g_ffn.reshape(1, D))


CONV_PIECE_ROWS = 256


def _ffn_up_kernel(a_ref, wg_ref, wv_ref, cwg_ref, cwv_ref, cbg_ref, cbv_ref, src_ref, o_ref, dst_ref, ubuf_ref,
                   carry_ref, *, tiles_per_seq, cast_blocks):
    i = pl.program_id(0)
    j = pl.program_id(1)

    @pl.when(jnp.logical_and(i == 0, j == 0))
    def _():
        carry_ref[...] = jnp.zeros_like(carry_ref)

    _cast_block(src_ref, dst_ref, i * pl.num_programs(1) + j, cast_blocks[0], cast_blocks[1])
    seq_start = (i % tiles_per_seq) == 0
    halo = SUBLANES
    chunks = _row_chunks(a_ref.shape[0])
    size = lambda c: chunks[c].stop - chunks[c].start
    lane_tiles = range(o_ref.shape[1] // LANES)
    projections = ((wg_ref, cwg_ref, cbg_ref), (wv_ref, cwv_ref, cbv_ref))

    def project(c):
        par = c % 2
        a = a_ref[chunks[c], :]
        for slot, (w_ref, _, _) in enumerate(projections):
            u = _dot(a, w_ref[...])
            for lt in lane_tiles:
                if c == 0:
                    ubuf_ref[par, slot, lt, 0:halo, :] = jnp.where(seq_start, 0.0, carry_ref[j, slot, lt])
                else:
                    ubuf_ref[par, slot, lt, 0:halo, :] = ubuf_ref[1 - par, slot, lt, size(c - 1) : size(c - 1) + halo, :]
                ubuf_ref[par, slot, lt, halo : halo + size(c), :] = u[:, lt * LANES : (lt + 1) * LANES]

    def activate(c):
        par = c % 2
        piece = min(CONV_PIECE_ROWS, size(c))
        for lt in lane_tiles:
            cols = slice(lt * LANES, (lt + 1) * LANES)
            for p0 in range(0, size(c), piece):

                def conv(slot):
                    _, cw_ref, cb_ref = projections[slot]
                    cw = cw_ref[:, cols]
                    c2 = cb_ref[:, cols] + ubuf_ref[par, slot, lt, pl.ds(p0 + halo - 2, piece), :] * cw[0:1, :]
                    c2 = c2 + ubuf_ref[par, slot, lt, pl.ds(p0 + halo - 1, piece), :] * cw[1:2, :]
                    return c2 + ubuf_ref[par, slot, lt, pl.ds(p0 + halo, piece), :] * cw[2:3, :]

                out_rows = slice(chunks[c].start + p0, chunks[c].start + p0 + piece)
                o_ref[out_rows, cols] = (_silu(conv(0)) * conv(1)).astype(o_ref.dtype)

    project(0)
    for c in range(1, len(chunks)):
        project(c)
        activate(c - 1)
    activate(len(chunks) - 1)
    last = len(chunks) - 1
    for slot in range(2):
        for lt in lane_tiles:
            carry_ref[j, slot, lt] = ubuf_ref[last % 2, slot, lt, size(last) : size(last) + halo, :]


def _ffn_up(xn, seq, w_up, conv_w, conv_b, cast, *, tm=2048, tf=256):
    T, D = xn.shape
    F = w_up.shape[1] // 2
    tm = min(tm, seq)
    assert F % tf == 0 and seq % tm == 0
    n_j = F // tf
    n_steps = (T // tm) * n_j
    n_blocks, src_spec, dst_spec, dst_shape = _cast_plan(cast, n_steps, lambda i, j: i * n_j + j)
    kern = functools.partial(_ffn_up_kernel, tiles_per_seq=seq // tm, cast_blocks=(n_blocks, n_steps))
    gate_col = lambda i, j: (0, j)
    value_col = lambda i, j: (0, n_j + j)
    return pl.pallas_call(
        kern,
        grid=(T // tm, n_j),
        in_specs=[
            pl.BlockSpec((tm, D), lambda i, j: (i, 0)),
            pl.BlockSpec((D, tf), gate_col),
            pl.BlockSpec((D, tf), value_col),
            pl.BlockSpec((CONV_WIDTH, tf), gate_col),
            pl.BlockSpec((CONV_WIDTH, tf), value_col),
            pl.BlockSpec((1, tf), gate_col),
            pl.BlockSpec((1, tf), value_col),
            src_spec,
        ],
        out_specs=[pl.BlockSpec((tm, tf), lambda i, j: (i, j)), dst_spec],
        out_shape=[jax.ShapeDtypeStruct((T, F), BF16), dst_shape],
        scratch_shapes=[
            pltpu.VMEM((2, 2, tf // LANES, SUBLANES + min(ROW_CHUNK, tm), LANES), F32),
            pltpu.VMEM((n_j, 2, tf // LANES, SUBLANES, LANES), F32),
        ],
        compiler_params=_cparams("arbitrary", "arbitrary"),
        name="ffn_up",
    )(xn, w_up, w_up, conv_w, conv_w, conv_b, conv_b, cast)


def _ffn_down_kernel(a_ref, w_ref, h_ref, o_ref):
    for rows in _row_chunks(o_ref.shape[0]):
        o_ref[rows, :] = h_ref[rows, :] + _dot(a_ref[rows, :], w_ref[...])


def _ffn_down(act, w_down, h2d, *, tm=512, tn=1024):
    T, F = act.shape
    N = w_down.shape[1]
    tm, tn = min(tm, T), min(tn, N)
    return pl.pallas_call(
        _ffn_down_kernel,
        grid=(N // tn, T // tm),
        in_specs=[
            pl.BlockSpec((tm, F), lambda j, i: (i, 0)),
            pl.BlockSpec((F, tn), lambda j, i: (0, j), pipeline_mode=pl.Buffered(1)),
            pl.BlockSpec((tm, tn), lambda j, i: (i, j)),
        ],
        out_specs=pl.BlockSpec((tm, tn), lambda j, i: (i, j)),
        out_shape=jax.ShapeDtypeStruct((T, N), F32),
        compiler_params=_cparams("parallel", "arbitrary"),
        name="ffn_down",
    )(act, w_down, h2d)


def _rope_tables(S):
    inv_freq = ROPE_BASE ** (-jnp.linspace(0.0, 1.0, HEAD_DIM // 2, dtype=F32))
    ang = jnp.arange(S, dtype=F32)[:, None] * inv_freq[None, :]
    cos, sin = jnp.cos(ang), jnp.sin(ang)
    return jnp.concatenate([cos, cos], axis=-1), jnp.concatenate([-sin, sin], axis=-1)


def _retention_tables(H):
    C = RET_CHUNK
    log_g = jnp.log1p(-jnp.exp2(-5.0 - jnp.arange(H, dtype=F32)))
    idx = jnp.arange(C, dtype=F32)
    diff = idx[:, None] - idx[None, :]
    dint = jnp.where(diff >= 0, jnp.exp(jnp.maximum(diff, 0.0)[None] * log_g[:, None, None]), 0.0)
    kdec = jnp.exp((C - 1 - idx)[None, :] * log_g[:, None])
    qdec = jnp.exp((idx + 1.0)[None, :] * log_g[:, None])
    cdec = jnp.exp(C * log_g)
    lanes = lambda t: jnp.broadcast_to(t[..., None], t.shape + (HEAD_DIM,))
    return dint, lanes(kdec), lanes(qdec), lanes(cdec[:, None])


def _layer(h, mem, attn_norm, w_in, ret_norm, sb_q_norm, sb_k_norm, sb_out_norm, w_out, cross_norm, mem_norm,
           cross_w_q, cross_w_kv, cross_q_norm, cross_k_norm, cross_w_o, ffn_norm, ffn_w_up, ffn_conv_w,
           ffn_conv_b, ffn_w_down):
    B, S, D = h.shape
    T = B * S
    W = D // 2
    H = W // HEAD_DIM
    tn = min(PROJ_TILE_COLS, W)
    seg = W // tn
    x2d = h.reshape(T, D)

    xn = _rmsnorm_rows(x2d, attn_norm, BF16)
    w_in_b = w_in.astype(BF16)
    cos2, sin2 = _rope_tables(S)
    tm_in = min(PROJ_TILE_ROWS, S)
    tiles_per_seq = S // tm_in
    rope_spec = pl.BlockSpec((tm_in, HEAD_DIM), lambda i, j: (i % tiles_per_seq, 0))
    ret_qk = _inproj(xn, w_in_b, B, S, n_col_tiles=2 * seg, col_map=lambda j: j, mode="rope", out_dtype=F32,
                     extra=(cos2, sin2), extra_specs=(rope_spec, rope_spec), n_scaled_tiles=seg, tn=tn,
                     name="inproj_ret_qk")
    v_all, w_up_b = _inproj(xn, w_in_b, B, S, n_col_tiles=2 * seg,
                            col_map=lambda j: jnp.where(j < seg, 2 * seg + j, 5 * seg + j), mode="plain",
                            out_dtype=BF16, tn=tn, name="inproj_v", cast=ffn_w_up)
    gate, w_out_b = _inproj(xn, w_in_b, B, S, n_col_tiles=seg, col_map=lambda j: 3 * seg + j, mode="plain",
                            out_dtype=F32, tn=tn, name="inproj_gate", cast=w_out)
    sb_gains = jnp.stack([sb_q_norm, sb_k_norm]).reshape(2, 1, HEAD_DIM)
    gain_spec = pl.BlockSpec((1, 1, HEAD_DIM), lambda i, j: (j // seg, 0, 0))
    sb_qk = _inproj(xn, w_in_b, B, S, n_col_tiles=2 * seg, col_map=lambda j: 4 * seg + j, mode="norm",
                    out_dtype=BF16, extra=(sb_gains,), extra_specs=(gain_spec,), tn=tn, name="inproj_sb_qk")

    ret = _retention(ret_qk, v_all, gate, _retention_tables(H), ret_norm)
    sb = _stick_breaking(sb_qk, v_all, sb_out_norm)
    h1 = _outproj(ret.reshape(T, W), sb.reshape(T, W), w_out_b, x2d)

    kmem, vmem = _memkv(mem, mem_norm, cross_w_kv.astype(BF16), cross_k_norm)
    h2, xn_ffn = _cross_attention(h1, S, cross_norm, cross_w_q.astype(BF16), cross_q_norm, kmem, vmem,
                                  cross_w_o.astype(BF16), ffn_norm)

    act, w_down_b = _ffn_up(xn_ffn, S, w_up_b, ffn_conv_w, ffn_conv_b.reshape(1, -1), ffn_w_down)
    out = _ffn_down(act, w_down_b, h2)
    return out.reshape(B, S, D)


def kernel(x, mem, attn_norm, w_in, ret_norm, sb_q_norm, sb_k_norm, sb_out_norm, w_out, cross_norm, mem_norm,
           cross_w_q, cross_w_kv, cross_q_norm, cross_k_norm, cross_w_o, ffn_norm, ffn_w_up, ffn_conv_w,
           ffn_conv_b, ffn_w_down):
    h = x
    for l in range(attn_norm.shape[0]):
        h = _layer(h, mem, attn_norm[l], w_in[l], ret_norm[l], sb_q_norm[l], sb_k_norm[l], sb_out_norm[l],
                   w_out[l], cross_norm[l], mem_norm[l], cross_w_q[l], cross_w_kv[l], cross_q_norm[l],
                   cross_k_norm[l], cross_w_o[l], ffn_norm[l], ffn_w_up[l], ffn_conv_w[l], ffn_conv_b[l],
                   ffn_w_down[l])
    return h
```

```python
import functools

import jax
import jax.numpy as jnp
from jax import lax
from jax.experimental import pallas as pl
from jax.experimental.pallas import tpu as pltpu

HEAD_DIM = 128
EPS = 1e-6
ROPE_BASE = 10000.0
RET_CHUNK = 128
CONV_WIDTH = 3
SUBLANES = 8
LANES = 128

VMEM_LIMIT_BYTES = 60 * 1024 * 1024

PROJ_TILE_ROWS = 1024
PROJ_TILE_COLS = 1024

ROW_CHUNK = 256

SB_EXP2_UNDERFLOW = 150.0
LOG2_E = 1.4426950408889634
SB_BLOCKS_IN_FLIGHT = 4

F32 = jnp.float32
BF16 = jnp.bfloat16


def _cparams(*sem):
    return pltpu.CompilerParams(dimension_semantics=sem, vmem_limit_bytes=VMEM_LIMIT_BYTES)


def _rms(x, g):
    return x * lax.rsqrt(jnp.mean(x * x, axis=-1, keepdims=True) + EPS) * g


def _silu(x):
    return x * (1.0 / (1.0 + jnp.exp(-x)))


def _dot(a, b):
    return jnp.dot(a, b, preferred_element_type=F32)


def _dot_nt(a, b):
    return lax.dot_general(a, b, (((1,), (1,)), ((), ())), preferred_element_type=F32)


def _row_chunks(n_rows):
    rc = min(ROW_CHUNK, n_rows)
    return [slice(r0, r0 + rc) for r0 in range(0, n_rows, rc)]


def _rmsnorm_kernel(x_ref, g_ref, o_ref):
    o_ref[...] = _rms(x_ref[...], g_ref[...]).astype(o_ref.dtype)


def _rmsnorm_rows(x2d, g, out_dtype, tm=512):
    T, D = x2d.shape
    tm = min(tm, T)
    return pl.pallas_call(
        _rmsnorm_kernel,
        grid=(T // tm,),
        in_specs=[pl.BlockSpec((tm, D), lambda i: (i, 0)), pl.BlockSpec((1, D), lambda i: (0, 0))],
        out_specs=pl.BlockSpec((tm, D), lambda i: (i, 0)),
        out_shape=jax.ShapeDtypeStruct((T, D), out_dtype),
        compiler_params=_cparams("parallel"),
        name="rmsnorm_rows",
    )(x2d, g.reshape(1, D))


def _cast_block(src_ref, dst_ref, step, n_blocks, n_steps):
    if n_blocks == n_steps:
        dst_ref[...] = src_ref[...].astype(dst_ref.dtype)
    else:

        @pl.when(step < n_blocks)
        def _():
            dst_ref[...] = src_ref[...].astype(dst_ref.dtype)


def _cast_plan(w, n_steps, step_of):
    n_rows = w.shape[0]
    packed_rows = 16
    n_blocks = max(n for n in range(1, n_steps + 1) if n_rows % n == 0 and (n_rows // n) % packed_rows == 0)
    spec = lambda: pl.BlockSpec((n_rows // n_blocks, w.shape[1]),
                                lambda *g: (jnp.minimum(step_of(*g), n_blocks - 1), 0))
    return n_blocks, spec(), spec(), jax.ShapeDtypeStruct(w.shape, BF16)


def _inproj_kernel(a_ref, w_ref, *rest, mode, heads_per_tile, n_scaled_tiles, n_extra, cast_blocks):
    j = pl.program_id(1)
    extra = rest[:n_extra]
    if cast_blocks:
        src_ref, o_ref, dst_ref = rest[n_extra:]
        _cast_block(src_ref, dst_ref, pl.program_id(0) * pl.num_programs(1) + j, cast_blocks[0], cast_blocks[1])
    else:
        (o_ref,) = rest[n_extra:]
    if mode == "rope":
        scale = jnp.where(j < n_scaled_tiles, HEAD_DIM**-0.5, 1.0).astype(F32)
    elif mode == "norm":
        gain = extra[0][0]
    for rows in _row_chunks(a_ref.shape[0]):
        acc = _dot(a_ref[rows, :], w_ref[...])
        for hh in range(heads_per_tile):
            xh = acc[:, hh * HEAD_DIM : (hh + 1) * HEAD_DIM]
            if mode == "rope":
                xh = (xh * extra[0][rows, :] + pltpu.roll(xh, HEAD_DIM // 2, axis=1) * extra[1][rows, :]) * scale
            elif mode == "norm":
                xh = _rms(xh, gain)
            o_ref[0, hh, rows, :] = xh.astype(o_ref.dtype)


def _inproj(xn, w, batch, seq, *, n_col_tiles, col_map, mode, out_dtype, extra=(), extra_specs=(),
            n_scaled_tiles=0, tm=PROJ_TILE_ROWS, tn=PROJ_TILE_COLS, name, cast=None):
    T, D = xn.shape
    tm = min(tm, seq)
    hpt = tn // HEAD_DIM
    tiles_per_seq = seq // tm
    n_steps = (T // tm) * n_col_tiles
    in_specs = [
        pl.BlockSpec((tm, D), lambda i, j: (i, 0)),
        pl.BlockSpec((D, tn), lambda i, j: (0, col_map(j))),
        *extra_specs,
    ]
    out_specs = pl.BlockSpec((1, hpt, tm, HEAD_DIM), lambda i, j: (i // tiles_per_seq, j, i % tiles_per_seq, 0))
    out_shape = jax.ShapeDtypeStruct((batch, n_col_tiles * hpt, seq, HEAD_DIM), out_dtype)
    operands = [xn, w, *extra]
    cast_blocks = None
    if cast is not None:
        n_blocks, src_spec, dst_spec, dst_shape = _cast_plan(cast, n_steps, lambda i, j: i * n_col_tiles + j)
        in_specs.append(src_spec)
        out_specs, out_shape = [out_specs, dst_spec], [out_shape, dst_shape]
        operands.append(cast)
        cast_blocks = (n_blocks, n_steps)
    kern = functools.partial(_inproj_kernel, mode=mode, heads_per_tile=hpt, n_scaled_tiles=n_scaled_tiles,
                             n_extra=len(extra), cast_blocks=cast_blocks)
    return pl.pallas_call(
        kern,
        grid=(T // tm, n_col_tiles),
        in_specs=in_specs,
        out_specs=out_specs,
        out_shape=out_shape,
        compiler_params=pltpu.CompilerParams(
            dimension_semantics=("arbitrary", "arbitrary"), vmem_limit_bytes=VMEM_LIMIT_BYTES,
            allow_input_fusion=[n == 1 for n in range(len(operands))]),
        name=name,
    )(*operands)


def _retention_kernel(q_ref, k_ref, v_ref, g_ref, dint_ref, kdec_ref, qdec_ref, cdec_ref, gn_ref, o_ref,
                      state_ref, *, n_chunks):
    @pl.when(pl.program_id(2) == 0)
    def _():
        state_ref[...] = jnp.zeros_like(state_ref)

    dint = dint_ref[0]
    kdec = kdec_ref[0]
    qdec = qdec_ref[0]
    cdec = cdec_ref[0]
    gn = gn_ref[...]
    C = RET_CHUNK
    chunks = range(n_chunks)
    rows = [pl.ds(n * C, C) for n in chunks]
    q = [q_ref[0, 0, r, :] for r in rows]
    k = [k_ref[0, 0, r, :] for r in rows]
    scores = [_dot_nt(q[n].astype(BF16), k[n].astype(BF16)) for n in chunks]
    kv = [_dot((k[n] * kdec).T.astype(BF16), v_ref[0, 0, rows[n], :]) for n in chunks]
    states = [state_ref[...]]
    for n in chunks:
        states.append(states[n] * cdec + kv[n])
    state_ref[...] = states[n_chunks]
    out = [
        _dot(
            jnp.concatenate([(scores[n] * dint).astype(BF16), (q[n] * qdec).astype(BF16)], axis=1),
            jnp.concatenate([v_ref[0, 0, rows[n], :], states[n].astype(BF16)], axis=0),
        )
        for n in chunks
    ]
    for n in chunks:
        y = _rms(out[n], gn) * _silu(g_ref[0, 0, rows[n], :])
        o_ref[0, rows[n], :] = y.astype(o_ref.dtype)


def _retention(qk, v_all, gate, tables, gn, *, rows_per_step=8192):
    B, H2, S, d = qk.shape
    H = H2 // 2
    tr = min(rows_per_step, S)
    dint, kdec, qdec, cdec = tables
    C = RET_CHUNK
    kern = functools.partial(_retention_kernel, n_chunks=tr // C)
    head_blk = (1, 1, tr, d)
    return pl.pallas_call(
        kern,
        grid=(B, H, S // tr),
        in_specs=[
            pl.BlockSpec(head_blk, lambda b, h, r: (b, h, r, 0)),
            pl.BlockSpec(head_blk, lambda b, h, r: (b, H + h, r, 0)),
            pl.BlockSpec(head_blk, lambda b, h, r: (b, h, r, 0)),
            pl.BlockSpec(head_blk, lambda b, h, r: (b, h, r, 0)),
            pl.BlockSpec((1, C, C), lambda b, h, r: (h, 0, 0)),
            pl.BlockSpec((1, C, d), lambda b, h, r: (h, 0, 0)),
            pl.BlockSpec((1, C, d), lambda b, h, r: (h, 0, 0)),
            pl.BlockSpec((1, 1, d), lambda b, h, r: (h, 0, 0)),
            pl.BlockSpec((1, d), lambda b, h, r: (0, 0)),
        ],
        out_specs=pl.BlockSpec((1, tr, d), lambda b, h, r: (b, r, h)),
        out_shape=jax.ShapeDtypeStruct((B, S, H * d), BF16),
        scratch_shapes=[pltpu.VMEM((d, d), F32)],
        compiler_params=_cparams("parallel", "parallel", "arbitrary"),
        name="retention",
    )(qk, qk, v_all, gate, dint, kdec, qdec, cdec, gn.reshape(1, d))


def _sb_kernel(q_ref, k_ref, v_ref, w2_ref, gn_ref, o_ref, *, tq, tk, group, n_sub):
    scale = HEAD_DIM**-0.5
    w2 = w2_ref[...]
    gn = gn_ref[...]
    heads = range(group)

    def guard(unit, x):
        mask = unit[4]
        if mask is not None:
            head = [] if x.shape[1] == tk else [x[:, : x.shape[1] - tk]]
            x = jnp.concatenate(head + [jnp.where(mask, x[:, x.shape[1] - tk :], 0.0)], axis=1)
        return x

    def key_rows(unit):
        return pl.ds(unit[2], unit[3] * tk)

    def scan_keys(units):
        newest = lambda x: x[:, x.shape[1] - tk :]
        z2 = [_dot_nt(u[1], k_ref[0, u[0], key_rows(u), :]) * (scale * LOG2_E) for u in units]
        hi, lo = [], []
        for u, z in zip(units, z2):
            sp = guard(u, jnp.maximum(z, 0.0) + jnp.log2(1.0 + jnp.exp2(-jnp.abs(z))))
            h = sp.astype(BF16)
            hi.append(h)
            lo.append((sp - h.astype(F32)).astype(BF16))
        cr_new = [_dot(jnp.concatenate([newest(h), newest(l)], axis=1), w2) for h, l in zip(hi, lo)]
        cr_old = [_dot(jnp.concatenate([h[:, :tk], l[:, :tk]], axis=1), w2) if u[3] == 2 else None
                  for u, h, l in zip(units, hi, lo)]
        return list(zip(z2, cr_new, cr_old))

    def weigh(units, scans, runs, accs):
        a, new_runs = [], []
        for u, (z2, cr_new, cr_old), run in zip(units, scans, runs):
            run_mid = run + cr_new[:, tk:]
            if cr_old is not None:
                behind = jnp.concatenate([cr_old[:, :tk] + run_mid, cr_new[:, :tk] + run], axis=1)
                new_runs.append(run_mid + cr_old[:, tk:])
            else:
                behind = cr_new[:, :tk] + run
                new_runs.append(run_mid)
            a.append(guard(u, jnp.exp2(z2 - behind)).astype(BF16))
        new_accs = []
        for u, w, acc in zip(units, a, accs):
            out = _dot(w, v_ref[0, u[0], key_rows(u), :])
            new_accs.append(acc + (out if u[5] is None else jnp.where(u[5], out, 0.0)))
        return new_runs, new_accs

    mask = lax.broadcasted_iota(jnp.int32, (tk, tk), 1) < lax.broadcasted_iota(jnp.int32, (tk, tk), 0)

    def query_blocks(step, carry):
        subs = [step * SB_BLOCKS_IN_FLIGHT + n for n in range(SB_BLOCKS_IN_FLIGHT)]
        qis = [pl.program_id(2) * n_sub + sub for sub in subs]
        row0s = [pl.multiple_of(sub * tq, tq) for sub in subs]

        def q_rows(n, g, start, size):
            return q_ref[0, g, pl.ds(pl.multiple_of(row0s[n] + start, tk), size), :]

        diag = [[(g, q_rows(n, g, half * tk, tk), pl.multiple_of(qi * tq, tq), half + 1, mask, None)
                 for g in heads for half in range(2)] for n, qi in enumerate(qis)]
        prev = [[(g, q_rows(n, g, 0, tq), pl.multiple_of(jnp.maximum(qi - 1, 0) * tq, tq), 2, None, qi > 0)
                 for g in heads] for n, qi in enumerate(qis)]
        units = [u for n in range(len(qis)) for u in diag[n] + prev[n]]
        scans = scan_keys(units)
        per_block = len(diag[0]) + len(prev[0])
        zero_run, zero_acc = jnp.zeros((tk, tk), F32), jnp.zeros((tk, HEAD_DIM), F32)
        state = []
        for n in range(len(qis)):
            sc = scans[n * per_block : (n + 1) * per_block]
            n_diag = len(diag[n])
            runs, accs = weigh(diag[n], sc[:n_diag], [zero_run] * n_diag, [zero_acc] * n_diag)
            runs = [jnp.concatenate([runs[2 * g], runs[2 * g + 1]], axis=0) for g in heads]
            accs = [jnp.concatenate([accs[2 * g], accs[2 * g + 1]], axis=0) for g in heads]
            state.append(weigh(prev[n], sc[n_diag:], runs, accs))

        def cond(c):
            return jnp.logical_and(c[0] >= 0, jnp.min(functools.reduce(jnp.minimum, c[1])) < SB_EXP2_UNDERFLOW)

        for n, qi in enumerate(qis):

            def body(c, n=n):
                kblk, runs, accs = c
                kstart = pl.multiple_of(kblk * tq, tq)
                older = [(g, q_rows(n, g, 0, tq), kstart, 2, None, None) for g in heads]
                runs, accs = weigh(older, scan_keys(older), runs, accs)
                return kblk - 1, tuple(runs), tuple(accs)

            runs, accs = state[n]
            _, _, accs = lax.while_loop(cond, body, (qi - 2, tuple(runs), tuple(accs)))
            for g in heads:
                o_ref[0, pl.ds(row0s[n], tq), g * HEAD_DIM : (g + 1) * HEAD_DIM] = _rms(accs[g], gn).astype(o_ref.dtype)
        return carry

    lax.fori_loop(0, n_sub // SB_BLOCKS_IN_FLIGHT, query_blocks, 0)


def _stick_breaking(sqk, v_all, gn, *, tk=128, group=4, blocks_per_step=8):
    B, H2, S, d = sqk.shape
    H = H2 // 2
    tq = 2 * tk
    group = min(group, H)
    n_groups = H // group
    idx = jnp.arange(tk)
    w2 = jnp.concatenate([(idx[:, None] >= idx[None, :]).astype(BF16), jnp.ones((tk, tk), BF16)], axis=1)
    w2 = jnp.concatenate([w2, w2], axis=0)
    n_sub = min(blocks_per_step, S // tq)
    assert S % (tq * n_sub) == 0 and n_sub % SB_BLOCKS_IN_FLIGHT == 0
    kern = functools.partial(_sb_kernel, tq=tq, tk=tk, group=group, n_sub=n_sub)
    kv_spec = lambda: pl.BlockSpec((1, group, S, d), lambda b, h, qi: (b, n_groups + h, 0, 0))
    return pl.pallas_call(
        kern,
        grid=(B, n_groups, S // (tq * n_sub)),
        in_specs=[
            pl.BlockSpec((1, group, tq * n_sub, d), lambda b, h, qi: (b, h, qi, 0)),
            kv_spec(),
            kv_spec(),
            pl.BlockSpec((2 * tk, 2 * tk), lambda b, h, qi: (0, 0)),
            pl.BlockSpec((1, d), lambda b, h, qi: (0, 0)),
        ],
        out_specs=pl.BlockSpec((1, tq * n_sub, group * d), lambda b, h, qi: (b, qi, h)),
        out_shape=jax.ShapeDtypeStruct((B, S, H * d), BF16),
        compiler_params=_cparams("parallel", "parallel", "arbitrary"),
        name="stick_breaking",
    )(sqk, sqk, v_all, w2, gn.reshape(1, d))


def _outproj_kernel(a1_ref, a2_ref, w1_ref, w2_ref, x_ref, o_ref):
    for rows in _row_chunks(o_ref.shape[0]):
        acc = _dot(a1_ref[rows, :], w1_ref[...]) + _dot(a2_ref[rows, :], w2_ref[...])
        o_ref[rows, :] = x_ref[rows, :] + acc


def _outproj(ret2d, sb2d, w_out, x2d, *, tm=PROJ_TILE_ROWS, tn=PROJ_TILE_COLS):
    T, K = ret2d.shape
    N = w_out.shape[1]
    tm, tn = min(tm, T), min(tn, N)
    return pl.pallas_call(
        _outproj_kernel,
        grid=(T // tm, N // tn),
        in_specs=[
            pl.BlockSpec((tm, K), lambda i, j: (i, 0)),
            pl.BlockSpec((tm, K), lambda i, j: (i, 0)),
            pl.BlockSpec((K, tn), lambda i, j: (0, j)),
            pl.BlockSpec((K, tn), lambda i, j: (1, j)),
            pl.BlockSpec((tm, tn), lambda i, j: (i, j)),
        ],
        out_specs=pl.BlockSpec((tm, tn), lambda i, j: (i, j)),
        out_shape=jax.ShapeDtypeStruct((T, N), F32),
        compiler_params=_cparams("parallel", "arbitrary"),
        name="outproj",
    )(ret2d, sb2d, w_out, w_out, x2d)


def _memkv_kernel(m_ref, gm_ref, w_ref, gk_ref, k_ref, v_ref):
    mn = _rms(m_ref[0], gm_ref[...]).astype(BF16)
    kv = _dot(mn, w_ref[...])
    cw = kv.shape[1] // 2
    gk = gk_ref[...]
    for hh in range(cw // HEAD_DIM):
        sl = slice(hh * HEAD_DIM, (hh + 1) * HEAD_DIM)
        k_ref[0, :, sl] = _rms(kv[:, sl], gk).astype(k_ref.dtype)
    v_ref[0] = kv[:, cw:].astype(v_ref.dtype)


def _memkv(mem, g_mem, w_kv, g_k):
    B, M, D = mem.shape
    cw = w_kv.shape[1] // 2
    return pl.pallas_call(
        _memkv_kernel,
        grid=(B,),
        in_specs=[
            pl.BlockSpec((1, M, D), lambda b: (b, 0, 0)),
            pl.BlockSpec((1, D), lambda b: (0, 0)),
            pl.BlockSpec((D, 2 * cw), lambda b: (0, 0)),
            pl.BlockSpec((1, HEAD_DIM), lambda b: (0, 0)),
        ],
        out_specs=[pl.BlockSpec((1, M, cw), lambda b: (b, 0, 0))] * 2,
        out_shape=[jax.ShapeDtypeStruct((B, M, cw), BF16)] * 2,
        compiler_params=_cparams("parallel"),
        name="memkv",
    )(mem, g_mem.reshape(1, D), w_kv, g_k.reshape(1, HEAD_DIM))


def _cross_kernel(h_ref, gc_ref, wq_ref, gq_ref, k_ref, v_ref, wo_ref, gf_ref, h2_ref, xn_ref):
    n_rows = h_ref.shape[0]
    part_rows = min(LANES, n_rows)
    parts = [slice(r0, r0 + part_rows) for r0 in range(0, n_rows, part_rows)]
    gq = gq_ref[...]
    scale = HEAD_DIM**-0.5
    h = [h_ref[rows, :] for rows in parts]
    xn = [_rms(hp, gc_ref[...]).astype(BF16) for hp in h]
    q = [_dot(x, wq_ref[...]) for x in xn]
    head_cols = [slice(c0, c0 + HEAD_DIM) for c0 in range(0, wq_ref.shape[1], HEAD_DIM)]
    s = [[_dot_nt(_rms(qp[:, sl], gq).astype(BF16), k_ref[0, :, sl]) * scale for sl in head_cols] for qp in q]
    o = []
    for sp in s:
        heads_out = []
        for sh, sl in zip(sp, head_cols):
            e = jnp.exp(sh - jnp.max(sh, axis=-1, keepdims=True))
            p = e / jnp.sum(e, axis=-1, keepdims=True)
            heads_out.append(_dot(p.astype(BF16), v_ref[0, :, sl]).astype(BF16))
        o.append(jnp.concatenate(heads_out, axis=-1))
    h2 = [hp + _dot(op, wo_ref[...]) for hp, op in zip(h, o)]
    for rows, h2p in zip(parts, h2):
        h2_ref[rows, :] = h2p
        xn_ref[rows, :] = _rms(h2p, gf_ref[...]).astype(xn_ref.dtype)


def _cross_attention(h2d, seq, g_cross, w_q, g_q, kmem, vmem, w_o, g_ffn, *, tm=256):
    T, D = h2d.shape
    _, M, cw = kmem.shape
    tm = min(tm, seq)
    tiles_per_seq = seq // tm
    const = lambda i: (0, 0)
    return pl.pallas_call(
        _cross_kernel,
        grid=(T // tm,),
        in_specs=[
            pl.BlockSpec((tm, D), lambda i: (i, 0)),
            pl.BlockSpec((1, D), const),
            pl.BlockSpec((D, cw), const),
            pl.BlockSpec((1, HEAD_DIM), const),
            pl.BlockSpec((1, M, cw), lambda i: (i // tiles_per_seq, 0, 0)),
            pl.BlockSpec((1, M, cw), lambda i: (i // tiles_per_seq, 0, 0)),
            pl.BlockSpec((cw, D), const),
            pl.BlockSpec((1, D), const),
        ],
        out_specs=[pl.BlockSpec((tm, D), lambda i: (i, 0))] * 2,
        out_shape=[jax.ShapeDtypeStruct((T, D), F32), jax.ShapeDtypeStruct((T, D), BF16)],
        compiler_params=_cparams("parallel"),
        name="cross_attention",
    )(h2d, g_cross.reshape(1, D), w_q, g_q.reshape(1, HEAD_DIM), kmem, vmem, w_o, g_ffn.reshape(1, D))


CONV_PIECE_ROWS = 256


def _ffn_up_kernel(a_ref, wg_ref, wv_ref, cwg_ref, cwv_ref, cbg_ref, cbv_ref, src_ref, o_ref, dst_ref, ubuf_ref,
                   carry_ref, *, tiles_per_seq, cast_blocks):
    i = pl.program_id(0)
    j = pl.program_id(1)

    @pl.when(jnp.logical_and(i == 0, j == 0))
    def _():
        carry_ref[...] = jnp.zeros_like(carry_ref)

    _cast_block(src_ref, dst_ref, i * pl.num_programs(1) + j, cast_blocks[0], cast_blocks[1])
    seq_start = (i % tiles_per_seq) == 0
    halo = SUBLANES
    chunks = _row_chunks(a_ref.shape[0])
    size = lambda c: chunks[c].stop - chunks[c].start
    lane_tiles = range(o_ref.shape[1] // LANES)
    projections = ((wg_ref, cwg_ref, cbg_ref), (wv_ref, cwv_ref, cbv_ref))

    def project(c):
        par = c % 2
        a = a_ref[chunks[c], :]
        for slot, (w_ref, _, _) in enumerate(projections):
            u = _dot(a, w_ref[...])
            for lt in lane_tiles:
                if c == 0:
                    ubuf_ref[par, slot, lt, 0:halo, :] = jnp.where(seq_start, 0.0, carry_ref[j, slot, lt])
                else:
                    ubuf_ref[par, slot, lt, 0:halo, :] = ubuf_ref[1 - par, slot, lt, size(c - 1) : size(c - 1) + halo, :]
                ubuf_ref[par, slot, lt, halo : halo + size(c), :] = u[:, lt * LANES : (lt + 1) * LANES]

    def activate(c):
        par = c % 2
        piece = min(CONV_PIECE_ROWS, size(c))
        for lt in lane_tiles:
            cols = slice(lt * LANES, (lt + 1) * LANES)
            for p0 in range(0, size(c), piece):

                def conv(slot):
                    _, cw_ref, cb_ref = projections[slot]
                    cw = cw_ref[:, cols]
                    c2 = cb_ref[:, cols] + ubuf_ref[par, slot, lt, pl.ds(p0 + halo - 2, piece), :] * cw[0:1, :]
                    c2 = c2 + ubuf_ref[par, slot, lt, pl.ds(p0 + halo - 1, piece), :] * cw[1:2, :]
                    return c2 + ubuf_ref[par, slot, lt, pl.ds(p0 + halo, piece), :] * cw[2:3, :]

                out_rows = slice(chunks[c].start + p0, chunks[c].start + p0 + piece)
                o_ref[out_rows, cols] = (_silu(conv(0)) * conv(1)).astype(o_ref.dtype)

    project(0)
    for c in range(1, len(chunks)):
        project(c)
        activate(c - 1)
    activate(len(chunks) - 1)
    last = len(chunks) - 1
    for slot in range(2):
        for lt in lane_tiles:
            carry_ref[j, slot, lt] = ubuf_ref[last % 2, slot, lt, size(last) : size(last) + halo, :]


def _ffn_up(xn, seq, w_up, conv_w, conv_b, cast, *, tm=2048, tf=256):
    T, D = xn.shape
    F = w_up.shape[1] // 2
    tm = min(tm, seq)
    assert F % tf == 0 and seq % tm == 0
    n_j = F // tf
    n_steps = (T // tm) * n_j
    n_blocks, src_spec, dst_spec, dst_shape = _cast_plan(cast, n_steps, lambda i, j: i * n_j + j)
    kern = functools.partial(_ffn_up_kernel, tiles_per_seq=seq // tm, cast_blocks=(n_blocks, n_steps))
    gate_col = lambda i, j: (0, j)
    value_col = lambda i, j: (0, n_j + j)
    return pl.pallas_call(
        kern,
        grid=(T // tm, n_j),
        in_specs=[
            pl.BlockSpec((tm, D), lambda i, j: (i, 0)),
            pl.BlockSpec((D, tf), gate_col),
            pl.BlockSpec((D, tf), value_col),
            pl.BlockSpec((CONV_WIDTH, tf), gate_col),
            pl.BlockSpec((CONV_WIDTH, tf), value_col),
            pl.BlockSpec((1, tf), gate_col),
            pl.BlockSpec((1, tf), value_col),
            src_spec,
        ],
        out_specs=[pl.BlockSpec((tm, tf), lambda i, j: (i, j)), dst_spec],
        out_shape=[jax.ShapeDtypeStruct((T, F), BF16), dst_shape],
        scratch_shapes=[
            pltpu.VMEM((2, 2, tf // LANES, SUBLANES + min(ROW_CHUNK, tm), LANES), F32),
            pltpu.VMEM((n_j, 2, tf // LANES, SUBLANES, LANES), F32),
        ],
        compiler_params=_cparams("arbitrary", "arbitrary"),
        name="ffn_up",
    )(xn, w_up, w_up, conv_w, conv_w, conv_b, conv_b, cast)


def _ffn_down_kernel(a_ref, w_ref, h_ref, o_ref):
    for rows in _row_chunks(o_ref.shape[0]):
        o_ref[rows, :] = h_ref[rows, :] + _dot(a_ref[rows, :], w_ref[...])


def _ffn_down(act, w_down, h2d, *, tm=512, tn=1024):
    T, F = act.shape
    N = w_down.shape[1]
    tm, tn = min(tm, T), min(tn, N)
    return pl.pallas_call(
        _ffn_down_kernel,
        grid=(N // tn, T // tm),
        in_specs=[
            pl.BlockSpec((tm, F), lambda j, i: (i, 0)),
            pl.BlockSpec((F, tn), lambda j, i: (0, j), pipeline_mode=pl.Buffered(1)),
            pl.BlockSpec((tm, tn), lambda j, i: (i, j)),
        ],
        out_specs=pl.BlockSpec((tm, tn), lambda j, i: (i, j)),
        out_shape=jax.ShapeDtypeStruct((T, N), F32),
        compiler_params=_cparams("parallel", "arbitrary"),
        name="ffn_down",
    )(act, w_down, h2d)


def _rope_tables(S):
    inv_freq = ROPE_BASE ** (-jnp.linspace(0.0, 1.0, HEAD_DIM // 2, dtype=F32))
    ang = jnp.arange(S, dtype=F32)[:, None] * inv_freq[None, :]
    cos, sin = jnp.cos(ang), jnp.sin(ang)
    return jnp.concatenate([cos, cos], axis=-1), jnp.concatenate([-sin, sin], axis=-1)


def _retention_tables(H):
    C = RET_CHUNK
    log_g = jnp.log1p(-jnp.exp2(-5.0 - jnp.arange(H, dtype=F32)))
    idx = jnp.arange(C, dtype=F32)
    diff = idx[:, None] - idx[None, :]
    dint = jnp.where(diff >= 0, jnp.exp(jnp.maximum(diff, 0.0)[None] * log_g[:, None, None]), 0.0)
    kdec = jnp.exp((C - 1 - idx)[None, :] * log_g[:, None])
    qdec = jnp.exp((idx + 1.0)[None, :] * log_g[:, None])
    cdec = jnp.exp(C * log_g)
    lanes = lambda t: jnp.broadcast_to(t[..., None], t.shape + (HEAD_DIM,))
    return dint, lanes(kdec), lanes(qdec), lanes(cdec[:, None])


def _layer(h, mem, attn_norm, w_in, ret_norm, sb_q_norm, sb_k_norm, sb_out_norm, w_out, cross_norm, mem_norm,
           cross_w_q, cross_w_kv, cross_q_norm, cross_k_norm, cross_w_o, ffn_norm, ffn_w_up, ffn_conv_w,
           ffn_conv_b, ffn_w_down):
    B, S, D = h.shape
    T = B * S
    W = D // 2
    H = W // HEAD_DIM
    tn = min(PROJ_TILE_COLS, W)
    seg = W // tn
    x2d = h.reshape(T, D)

    xn = _rmsnorm_rows(x2d, attn_norm, BF16)
    w_in_b = w_in.astype(BF16)
    cos2, sin2 = _rope_tables(S)
    tm_in = min(PROJ_TILE_ROWS, S)
    tiles_per_seq = S // tm_in
    rope_spec = pl.BlockSpec((tm_in, HEAD_DIM), lambda i, j: (i % tiles_per_seq, 0))
    ret_qk = _inproj(xn, w_in_b, B, S, n_col_tiles=2 * seg, col_map=lambda j: j, mode="rope", out_dtype=F32,
                     extra=(cos2, sin2), extra_specs=(rope_spec, rope_spec), n_scaled_tiles=seg, tn=tn,
                     name="inproj_ret_qk")
    v_all, w_up_b = _inproj(xn, w_in_b, B, S, n_col_tiles=2 * seg,
                            col_map=lambda j: jnp.where(j < seg, 2 * seg + j, 5 * seg + j), mode="plain",
                            out_dtype=BF16, tn=tn, name="inproj_v", cast=ffn_w_up)
    gate, w_out_b = _inproj(xn, w_in_b, B, S, n_col_tiles=seg, col_map=lambda j: 3 * seg + j, mode="plain",
                            out_dtype=F32, tn=tn, name="inproj_gate", cast=w_out)
    sb_gains = jnp.stack([sb_q_norm, sb_k_norm]).reshape(2, 1, HEAD_DIM)
    gain_spec = pl.BlockSpec((1, 1, HEAD_DIM), lambda i, j: (j // seg, 0, 0))
    sb_qk = _inproj(xn, w_in_b, B, S, n_col_tiles=2 * seg, col_map=lambda j: 4 * seg + j, mode="norm",
                    out_dtype=BF16, extra=(sb_gains,), extra_specs=(gain_spec,), tn=tn, name="inproj_sb_qk")

    ret = _retention(ret_qk, v_all, gate, _retention_tables(H), ret_norm)
    sb = _stick_breaking(sb_qk, v_all, sb_out_norm)
    h1 = _outproj(ret.reshape(T, W), sb.reshape(T, W), w_out_b, x2d)

    kmem, vmem = _memkv(mem, mem_norm, cross_w_kv.astype(BF16), cross_k_norm)
    h2, xn_ffn = _cross_attention(h1, S, cross_norm, cross_w_q.astype(BF16), cross_q_norm, kmem, vmem,
                                  cross_w_o.astype(BF16), ffn_norm)

    act, w_down_b = _ffn_up(xn_ffn, S, w_up_b, ffn_conv_w, ffn_conv_b.reshape(1, -1), ffn_w_down)
    out = _ffn_down(act, w_down_b, h2)
    return out.reshape(B, S, D)


def kernel(x, mem, attn_norm, w_in, ret_norm, sb_q_norm, sb_k_norm, sb_out_norm, w_out, cross_norm, mem_norm,
           cross_w_q, cross_w_kv, cross_q_norm, cross_k_norm, cross_w_o, ffn_norm, ffn_w_up, ffn_conv_w,
           ffn_conv_b, ffn_w_down):
    h = x
    for l in range(attn_norm.shape[0]):
        h = _layer(h, mem, attn_norm[l], w_in[l], ret_norm[l], sb_q_norm[l], sb_k_norm[l], sb_out_norm[l],
                   w_out[l], cross_norm[l], mem_norm[l], cross_w_q[l], cross_w_kv[l], cross_q_norm[l],
                   cross_k_norm[l], cross_w_o[l], ffn_norm[l], ffn_w_up[l], ffn_conv_w[l], ffn_conv_b[l],
                   ffn_w_down[l])
    return h
```
